```python
import math
import jax, jax.numpy as jnp
from jax import lax
import numpy as np

D_MODEL = 1024
BATCH = 8
SEQ = 4096
DEPTH = 1

MLSTM_HEADS = 4
MLSTM_HEAD_DIM = D_MODEL // MLSTM_HEADS
MLSTM_WIDTH = MLSTM_HEADS * MLSTM_HEAD_DIM
MLSTM_CONV = 4
MLSTM_CHUNK = 128

ATTN_GROUPS = ((128, 1), (512, 4), (2048, 16))
ATTN_HEADS_PER_GROUP = 8
ATTN_HEAD_DIM = 64
ATTN_N_GROUPS = len(ATTN_GROUPS)
ATTN_HEADS = ATTN_N_GROUPS * ATTN_HEADS_PER_GROUP
ATTN_WIDTH = ATTN_HEADS * ATTN_HEAD_DIM
ATTN_OUT = ATTN_HEADS_PER_GROUP * ATTN_HEAD_DIM
ATTN_BLOCK = 128

N_BUCKETS = 32
MAX_DISTANCE = 2048

D_FF = -(-8 * D_MODEL // (3 * 256)) * 256

EPS = 1e-6

IN_SIZES = (MLSTM_WIDTH, MLSTM_WIDTH, MLSTM_WIDTH, 2 * MLSTM_HEADS, MLSTM_WIDTH,
            ATTN_WIDTH, ATTN_WIDTH, ATTN_WIDTH, 2 * D_MODEL)
IN_WIDTH = sum(IN_SIZES)

kernel_name = "hybrid_mlstm_dilated_swa_gated"


def _split_points(sizes):
    pts, acc = [], 0
    for s in sizes[:-1]:
        acc += s
        pts.append(acc)
    return pts


def rmsnorm(x, g):
    xf = x.astype(jnp.float32)
    r = lax.rsqrt(jnp.mean(xf * xf, axis=-1, keepdims=True) + EPS)
    return (xf * r * g.astype(jnp.float32)).astype(x.dtype)


def rel_bucket(n):
    max_exact = N_BUCKETS // 2
    nf = jnp.maximum(n, 1).astype(jnp.float32)
    large = max_exact + (jnp.log(nf / max_exact) / math.log(MAX_DISTANCE / max_exact)
                         * (N_BUCKETS - max_exact)).astype(jnp.int32)
    large = jnp.minimum(large, N_BUCKETS - 1)
    return jnp.where(n < max_exact, n, large)


def causal_depthwise_conv(x, w, b):
    c = x.shape[-1]
    y = lax.conv_general_dilated(x, w[:, None, :].astype(x.dtype), window_strides=(1,),
                                 padding=[(MLSTM_CONV - 1, 0)],
                                 dimension_numbers=('NWC', 'WIO', 'NWC'),
                                 feature_group_count=c)
    return y + b.astype(x.dtype)


def _mlstm_chunk_step(carry, inp):
    C, n, m = carry
    q, k, v, ig, lf = inp
    L = q.shape[-2]
    b = jnp.cumsum(lf, axis=-1)
    causal = jnp.tril(jnp.ones((L, L), dtype=bool))
    dmat = jnp.where(causal, b[..., :, None] - b[..., None, :] + ig[..., None, :], -jnp.inf)
    inter = b + m[..., None]
    m_t = jnp.maximum(inter, jnp.max(dmat, axis=-1))
    w_intra = jnp.exp(dmat - m_t[..., None]) * jnp.einsum('bhtd,bhsd->bhts', q, k)
    w_inter = jnp.exp(inter - m_t)
    num = (w_inter[..., None] * jnp.einsum('bhtd,bhde->bhte', q, C)
           + jnp.einsum('bhts,bhse->bhte', w_intra, v))
    den = w_inter * jnp.einsum('bhtd,bhd->bht', q, n) + jnp.sum(w_intra, axis=-1)
    h = num / jnp.maximum(jnp.abs(den), jnp.exp(-m_t))[..., None]
    b_last = b[..., -1]
    a = b_last[..., None] - b + ig
    m_new = jnp.maximum(b_last + m, jnp.max(a, axis=-1))
    decay = jnp.exp(b_last + m - m_new)
    wk = jnp.exp(a - m_new[..., None])
    C_new = decay[..., None, None] * C + jnp.einsum('bhs,bhsd,bhse->bhde', wk, k, v)
    n_new = decay[..., None] * n + jnp.einsum('bhs,bhsd->bhd', wk, k)
    return (C_new, n_new, m_new), h


def mlstm_cell(q, k, v, ig, lf):
    B, S, H, d = q.shape
    nc = S // MLSTM_CHUNK
    to_c = lambda t: t.reshape(B, nc, MLSTM_CHUNK, H, d).transpose(1, 0, 3, 2, 4)
    to_cg = lambda t: t.reshape(B, nc, MLSTM_CHUNK, H).transpose(1, 0, 3, 2)
    init = (jnp.zeros((B, H, d, d), jnp.float32), jnp.zeros((B, H, d), jnp.float32),
            jnp.zeros((B, H), jnp.float32))
    _, hs = lax.scan(_mlstm_chunk_step, init, (to_c(q), to_c(k), to_c(v), to_cg(ig), to_cg(lf)))
    return hs.transpose(1, 0, 3, 2, 4).reshape(B, S, H, d)


def dilated_window_attention(q, k, v, bias_tab, window, dilation):
    B, S, Hg, dh = q.shape
    span = window // dilation
    blk = ATTN_BLOCK
    sd = S // dilation
    nb = -(-sd // blk)
    pad = nb * blk - sd

    def to_blocks(t):
        t = t.reshape(B, sd, dilation, Hg, dh).transpose(0, 2, 3, 1, 4)
        t = jnp.pad(t, ((0, 0), (0, 0), (0, 0), (0, pad), (0, 0)))
        return t.reshape(B, dilation, Hg, nb, blk, dh)

    def with_prev(t):
        prev = jnp.pad(t, ((0, 0), (0, 0), (0, 0), (1, 0), (0, 0), (0, 0)))[:, :, :, :-1]
        return jnp.concatenate([prev, t], axis=-2)

    qb = to_blocks(q)
    kk = with_prev(to_blocks(k))
    vv = with_prev(to_blocks(v))

    q_pos = jnp.arange(blk)[:, None] + blk
    k_pos = jnp.arange(2 * blk)[None, :]
    dist = q_pos - k_pos
    key_idx = jnp.arange(nb)[:, None, None] * blk - blk + k_pos[None]
    valid = (dist >= 0) & (dist <= span) & (key_idx >= 0)
    bias = bias_tab[rel_bucket(jnp.maximum(dist, 0) * dilation)]
    bias = jnp.moveaxis(bias, -1, 0).astype(jnp.float32)[:, None]

    s = jnp.einsum('brhnqd,brhnkd->brhnqk', qb, kk) * (dh ** -0.5) + bias
    s = jnp.where(valid, s, -jnp.inf)
    mx = jnp.max(s, axis=-1, keepdims=True)
    p = jnp.exp(s - mx)
    l = jnp.sum(p, axis=-1)
    o = jnp.einsum('brhnqk,brhnkd->brhnqd', p, vv) / l[..., None]
    lse = mx[..., 0] + jnp.log(l)

    o = o.reshape(B, dilation, Hg, nb * blk, dh)[:, :, :, :sd]
    o = o.transpose(0, 3, 1, 2, 4).reshape(B, S, Hg, dh)
    lse = lse.reshape(B, dilation, Hg, nb * blk)[:, :, :, :sd]
    lse = lse.transpose(0, 3, 1, 2).reshape(B, S, Hg)
    return o, lse


def hybrid_layer(x, rel_bias, norm_mix_g, w_in, b_gate_if, conv_w, conv_b, mlstm_norm_g,
                 w_proj_a, w_proj_b, w_out, norm_ffn_g, w_gate, w_up, w_down):
    B, S, _ = x.shape
    f32 = jnp.float32
    h = rmsnorm(x, norm_mix_g)
    proj = h @ w_in
    q_m, k_m, v_m, if_pre, o_pre, q_a, k_a, v_a, gate_pre = jnp.split(
        proj, _split_points(IN_SIZES), axis=-1)

    qk = jax.nn.silu(causal_depthwise_conv(jnp.concatenate([q_m, k_m], axis=-1), conv_w, conv_b))
    q_m, k_m = jnp.split(qk.astype(f32), 2, axis=-1)
    q_m = q_m.reshape(B, S, MLSTM_HEADS, MLSTM_HEAD_DIM) * (MLSTM_HEAD_DIM ** -0.5)
    k_m = k_m.reshape(B, S, MLSTM_HEADS, MLSTM_HEAD_DIM)
    v_m = v_m.astype(f32).reshape(B, S, MLSTM_HEADS, MLSTM_HEAD_DIM)
    if_pre = if_pre.astype(f32) + b_gate_if.astype(f32)
    ig = if_pre[..., :MLSTM_HEADS]
    lf = jax.nn.log_sigmoid(if_pre[..., MLSTM_HEADS:])
    hm = mlstm_cell(q_m, k_m, v_m, ig, lf)
    mu = jnp.mean(hm, axis=-1, keepdims=True)
    var = jnp.mean(jnp.square(hm - mu), axis=-1, keepdims=True)
    hm = (hm - mu) * lax.rsqrt(var + EPS)
    hm = hm.reshape(B, S, MLSTM_WIDTH) * mlstm_norm_g.astype(f32)
    y_a = (jax.nn.sigmoid(o_pre.astype(f32)) * hm).astype(x.dtype)

    shp = (B, S, ATTN_N_GROUPS, ATTN_HEADS_PER_GROUP, ATTN_HEAD_DIM)
    q_a = q_a.astype(f32).reshape(shp)
    k_a = k_a.astype(f32).reshape(shp)
    v_a = v_a.astype(f32).reshape(shp)
    outs, lses = [], []
    for g, (window, dilation) in enumerate(ATTN_GROUPS):
        tab = rel_bias[:, g * ATTN_HEADS_PER_GROUP:(g + 1) * ATTN_HEADS_PER_GROUP]
        o_g, lse_g = dilated_window_attention(q_a[:, :, g], k_a[:, :, g], v_a[:, :, g],
                                              tab, window, dilation)
        outs.append(o_g)
        lses.append(lse_g)
    wts = jax.nn.softmax(jnp.stack(lses, axis=0), axis=0)
    y_b = jnp.sum(wts[..., None] * jnp.stack(outs, axis=0), axis=0)
    y_b = y_b.reshape(B, S, ATTN_OUT).astype(x.dtype)

    g_a, g_b = jnp.split(jax.nn.sigmoid(gate_pre), 2, axis=-1)
    merged = g_a * (y_a @ w_proj_a) + g_b * (y_b @ w_proj_b)
    x = x + merged @ w_out

    hf = rmsnorm(x, norm_ffn_g)
    x = x + (jax.nn.silu(hf @ w_gate) * (hf @ w_up)) @ w_down
    return x


def setup_inputs(seed: int = 0) -> dict:
    key = jax.random.key(seed)
    ks = jax.random.split(key, 20)
    nrm = lambda k, shape, scale: jax.random.normal(k, shape, jnp.float32) * scale
    x = jax.random.normal(ks[0], (BATCH, SEQ, D_MODEL), jnp.float32)
    norm_mix_g = 1.0 + nrm(ks[1], (DEPTH, D_MODEL), 0.02)
    w_in = nrm(ks[2], (DEPTH, D_MODEL, IN_WIDTH), D_MODEL ** -0.5)
    ig_bias = nrm(ks[3], (DEPTH, MLSTM_HEADS), 0.1)
    fg_bias = jnp.linspace(3.0, 6.0, MLSTM_HEADS, dtype=jnp.float32)[None] + nrm(ks[4], (DEPTH, MLSTM_HEADS), 0.1)
    b_gate_if = jnp.concatenate([ig_bias, fg_bias], axis=-1)
    conv_w = nrm(ks[5], (DEPTH, MLSTM_CONV, 2 * MLSTM_WIDTH), MLSTM_CONV ** -0.5)
    conv_b = nrm(ks[6], (DEPTH, 2 * MLSTM_WIDTH), 0.02)
    mlstm_norm_g = 1.0 + nrm(ks[7], (DEPTH, MLSTM_WIDTH), 0.02)
    w_proj_a = nrm(ks[8], (DEPTH, MLSTM_WIDTH, D_MODEL), MLSTM_WIDTH ** -0.5)
    w_proj_b = nrm(ks[9], (DEPTH, ATTN_OUT, D_MODEL), ATTN_OUT ** -0.5)
    w_out = nrm(ks[10], (DEPTH, D_MODEL, D_MODEL), D_MODEL ** -0.5)
    norm_ffn_g = 1.0 + nrm(ks[11], (DEPTH, D_MODEL), 0.02)
    w_gate = nrm(ks[12], (DEPTH, D_MODEL, D_FF), D_MODEL ** -0.5)
    w_up = nrm(ks[13], (DEPTH, D_MODEL, D_FF), D_MODEL ** -0.5)
    w_down = nrm(ks[14], (DEPTH, D_FF, D_MODEL), D_FF ** -0.5)
    rel_bias = nrm(ks[15], (N_BUCKETS, ATTN_HEADS), 0.5)
    norm_final_g = 1.0 + nrm(ks[16], (D_MODEL,), 0.02)
    return {"x": x, "norm_mix_g": norm_mix_g, "w_in": w_in, "b_gate_if": b_gate_if,
            "conv_w": conv_w, "conv_b": conv_b, "mlstm_norm_g": mlstm_norm_g,
            "w_proj_a": w_proj_a, "w_proj_b": w_proj_b, "w_out": w_out,
            "norm_ffn_g": norm_ffn_g, "w_gate": w_gate, "w_up": w_up, "w_down": w_down,
            "rel_bias": rel_bias, "norm_final_g": norm_final_g}


def reference(x, norm_mix_g, w_in, b_gate_if, conv_w, conv_b, mlstm_norm_g, w_proj_a,
              w_proj_b, w_out, norm_ffn_g, w_gate, w_up, w_down, rel_bias, norm_final_g):
    for layer in range(DEPTH):
        x = hybrid_layer(x, rel_bias, norm_mix_g[layer], w_in[layer], b_gate_if[layer],
                         conv_w[layer], conv_b[layer], mlstm_norm_g[layer], w_proj_a[layer],
                         w_proj_b[layer], w_out[layer], norm_ffn_g[layer], w_gate[layer],
                         w_up[layer], w_down[layer])
    return rmsnorm(x, norm_final_g)
```

```python
import functools
import math

import numpy as np
import jax
import jax.numpy as jnp
from jax import lax
from jax.experimental import pallas as pl
from jax.experimental.pallas import tpu as pltpu

F32 = jnp.float32
BF16 = jnp.bfloat16

D_MODEL = 1024
M_HEADS = 4
M_DH = 256
M_WIDTH = M_HEADS * M_DH
CONV_K = 4
CHUNK = 128
GROUPS = ((128, 1), (512, 4), (2048, 16))
N_GROUPS = len(GROUPS)
A_HG = 8
A_DH = 64
A_GW = A_HG * A_DH
A_WIDTH = N_GROUPS * A_GW
A_BLK = 128
N_BUCKETS = 32
MAX_DISTANCE = 2048
D_FF = 2816
EPS = 1e-6
NEG = -1e30

LANES = 128
SUBLANES = 8

P_GATE = 0
P_QM = 2048
P_KM = 3072
P_VM = 4096
P_OM = 5120
P_QA = 6144
P_KA = P_QA + A_WIDTH
P_VA = P_KA + A_WIDTH
P_WIDTH = P_VA + A_WIDTH

VMEM_LIMIT = 56 * 1024 * 1024


def _bucket_tables():
    i = np.arange(A_BLK)[:, None]
    j = np.arange(2 * A_BLK)[None, :]
    dist = i + A_BLK - j
    buckets = []
    for window, dil in GROUPS:
        n = np.maximum(dist, 0) * dil
        nf = np.maximum(n, 1).astype(np.float32)
        max_exact = N_BUCKETS // 2
        large = max_exact + (np.log(nf / max_exact) / math.log(MAX_DISTANCE / max_exact)
                             * (N_BUCKETS - max_exact)).astype(np.int32)
        large = np.minimum(large, N_BUCKETS - 1)
        buckets.append(np.where(n < max_exact, n, large).astype(np.int32))
    span = GROUPS[0][0] // GROUPS[0][1]
    assert all(w // d == span for w, d in GROUPS)
    valid = ((dist >= 0) & (dist <= span)).astype(np.int32)
    valid_first = (valid.astype(bool) & (j >= A_BLK)).astype(np.int32)
    return np.stack(buckets), np.stack([valid, valid_first])


def _bias_kernel(tab_ref, bucket_ref, valid_ref, out_ref):
    g = pl.program_id(0)
    h = pl.program_id(1)
    col = g * A_HG + h
    bucket = bucket_ref[...]
    acc = jnp.zeros(bucket.shape, F32)
    for b in range(N_BUCKETS):
        acc = jnp.where(bucket == b, tab_ref[b, col], acc)
    out_ref[0] = jnp.where(valid_ref[0] > 0, acc, NEG)
    out_ref[1] = jnp.where(valid_ref[1] > 0, acc, NEG)


def _bias_tables(rel_bias):
    buckets, valid = _bucket_tables()
    return pl.pallas_call(
        _bias_kernel,
        grid=(N_GROUPS, A_HG),
        in_specs=[
            pl.BlockSpec(memory_space=pltpu.SMEM),
            pl.BlockSpec((None, A_BLK, 2 * A_BLK), lambda g, h: (g, 0, 0)),
            pl.BlockSpec((2, A_BLK, 2 * A_BLK), lambda g, h: (0, 0, 0)),
        ],
        out_specs=pl.BlockSpec((None, 2, None, A_BLK, 2 * A_BLK), lambda g, h: (g, 0, h, 0, 0)),
        out_shape=jax.ShapeDtypeStruct((N_GROUPS, 2, A_HG, A_BLK, 2 * A_BLK), F32),
        name="bias_tables",
    )(rel_bias, jnp.asarray(buckets), jnp.asarray(valid))


def _inproj_kernel(x_ref, g_ref, w_ref, wif_ref, p_ref, if_ref, h_ref):
    @pl.when(pl.program_id(1) == 0)
    def _():
        x = x_ref[...]
        r = lax.rsqrt(jnp.mean(x * x, axis=-1, keepdims=True) + EPS)
        h = x * r * g_ref[...]
        h_ref[...] = h.astype(BF16)
        if_ref[...] = jnp.dot(h, wif_ref[...], preferred_element_type=F32,
                              precision=lax.Precision.HIGHEST)

    p_ref[...] = jnp.dot(h_ref[...], w_ref[...], preferred_element_type=F32).astype(BF16)


def _inproj(x2, g, w_p, w_if, tm=1024, tn=1536):
    n = x2.shape[0]
    return pl.pallas_call(
        _inproj_kernel,
        grid=(n // tm, P_WIDTH // tn),
        in_specs=[
            pl.BlockSpec((tm, D_MODEL), lambda i, j: (i, 0)),
            pl.BlockSpec((1, D_MODEL), lambda i, j: (0, 0)),
            pl.BlockSpec((D_MODEL, tn), lambda i, j: (0, j)),
            pl.BlockSpec((D_MODEL, LANES), lambda i, j: (0, 0)),
        ],
        out_specs=[
            pl.BlockSpec((tm, tn), lambda i, j: (i, j)),
            pl.BlockSpec((tm, LANES), lambda i, j: (i, 0)),
        ],
        out_shape=[
            jax.ShapeDtypeStruct((n, P_WIDTH), BF16),
            jax.ShapeDtypeStruct((n, LANES), F32),
        ],
        scratch_shapes=[pltpu.VMEM((tm, D_MODEL), BF16)],
        compiler_params=pltpu.CompilerParams(
            dimension_semantics=("parallel", "arbitrary"), vmem_limit_bytes=VMEM_LIMIT),
        name="inproj",
    )(x2, g, w_p, w_if)


def _lane_scan(x, op, fill):
    lane = lax.broadcasted_iota(jnp.int32, x.shape, 1)
    sh = 1
    while sh < x.shape[1]:
        x = op(x, jnp.where(lane >= sh, pltpu.roll(x, sh, 1), fill))
        sh *= 2
    return x


def _conv_silu(x, tail, w, b):
    ext = jnp.concatenate([tail, x], axis=0)
    L = x.shape[0]
    y = x * w[CONV_K - 1:CONV_K] + b
    for j in range(1, CONV_K):
        y = y + ext[SUBLANES - j:SUBLANES - j + L] * w[CONV_K - 1 - j:CONV_K - j]
    return y * jax.nn.sigmoid(y)


def _mlstm_kernel(q_ref, k_ref, v_ref, o_ref, if_ref, cw_ref, cb_ref, bif_ref, ng_ref,
                  y_ref, c_ref, n_ref, m_ref, qt_ref, kt_ref):
    L = CHUNK

    @pl.when(pl.program_id(1) == 0)
    def _():
        c_ref[...] = jnp.zeros_like(c_ref)
        n_ref[...] = jnp.zeros_like(n_ref)
        m_ref[...] = jnp.zeros_like(m_ref)
        qt_ref[...] = jnp.zeros_like(qt_ref)
        kt_ref[...] = jnp.zeros_like(kt_ref)

    xq = q_ref[...].astype(F32)
    xk = k_ref[...].astype(F32)
    cw = cw_ref[...]
    cb = cb_ref[...]
    q_all = _conv_silu(xq, qt_ref[...], cw[:, :M_WIDTH], cb[:, :M_WIDTH]) * (M_DH ** -0.5)
    k_all = _conv_silu(xk, kt_ref[...], cw[:, M_WIDTH:], cb[:, M_WIDTH:])
    qt_ref[...] = xq[L - SUBLANES:]
    kt_ref[...] = xk[L - SUBLANES:]

    gates = jnp.transpose(if_ref[...])[:SUBLANES] + bif_ref[...]
    ig = gates[:M_HEADS]
    lf = jax.nn.log_sigmoid(gates[M_HEADS:])
    b = _lane_scan(lf, jnp.add, 0.0)
    a = ig - b
    m_prev = m_ref[...][:M_HEADS]
    mx = jnp.maximum(m_prev, _lane_scan(a, jnp.maximum, NEG))
    w_inter = jnp.exp(m_prev - mx)
    e_neg = jnp.exp(-(b + mx))
    b_last = jnp.broadcast_to(b[:, L - 1:L], b.shape)
    a_end = b_last + a
    m_new = jnp.maximum(b_last + m_prev, jnp.max(a_end, axis=1, keepdims=True))
    decay = jnp.exp(b_last + m_prev - m_new)
    wk = jnp.exp(a_end - m_new)
    rows = jnp.concatenate([-mx, w_inter, e_neg, wk,
                            jnp.zeros((L - 4 * M_HEADS, L), F32)], axis=0)
    cols = jnp.transpose(rows)
    m_ref[...] = jnp.concatenate([m_new, m_new], axis=0)

    tri = (lax.broadcasted_iota(jnp.int32, (L, L), 0)
           >= lax.broadcasted_iota(jnp.int32, (L, L), 1))

    for h in range(M_HEADS):
        sl = slice(h * M_DH, (h + 1) * M_DH)
        q = q_all[:, sl]
        k = k_all[:, sl]
        qb = q.astype(BF16)
        kb = k.astype(BF16)
        vb = v_ref[:, sl]
        u_col = cols[:, h:h + 1]
        wi_col = cols[:, M_HEADS + h:M_HEADS + h + 1]
        en_col = cols[:, 2 * M_HEADS + h:2 * M_HEADS + h + 1]
        wk_col = cols[:, 3 * M_HEADS + h:3 * M_HEADS + h + 1]

        dmat = jnp.where(tri, jnp.exp(u_col + a[h:h + 1]), 0.0)
        s = lax.dot_general(qb, kb, (((1,), (1,)), ((), ())), preferred_element_type=F32)
        w_intra = dmat * s
        c_old = c_ref[h]
        n_old = n_ref[h:h + 1]
        num = (wi_col * jnp.dot(qb, c_old.astype(BF16), preferred_element_type=F32)
               + jnp.dot(w_intra.astype(BF16), vb, preferred_element_type=F32))
        den = (wi_col * jnp.sum(q * n_old, axis=1, keepdims=True)
               + jnp.sum(w_intra, axis=1, keepdims=True))
        hout = num / jnp.maximum(jnp.abs(den), en_col)

        kw = k * wk_col
        dec = jnp.concatenate([decay[h:h + 1], decay[h:h + 1]], axis=1)
        c_ref[h] = dec * c_old + lax.dot_general(
            kw.astype(BF16), vb, (((0,), (0,)), ((), ())), preferred_element_type=F32)
        n_ref[h:h + 1] = dec * n_old + jnp.sum(kw, axis=0, keepdims=True)

        mu = jnp.mean(hout, axis=1, keepdims=True)
        cen = hout - mu
        var = jnp.mean(cen * cen, axis=1, keepdims=True)
        hn = cen * lax.rsqrt(var + EPS) * ng_ref[:, sl]
        y_ref[:, sl] = (jax.nn.sigmoid(o_ref[:, sl].astype(F32)) * hn).astype(BF16)


def _mlstm(p3, if3, conv_w, conv_b, bif, ng):
    B, S, _ = p3.shape
    wblk = lambda col: pl.BlockSpec((None, CHUNK, M_WIDTH), lambda b, c: (b, c, col // M_WIDTH))
    full = lambda shape: pl.BlockSpec(shape, lambda b, c: (0,) * len(shape))
    return pl.pallas_call(
        _mlstm_kernel,
        grid=(B, S // CHUNK),
        in_specs=[
            wblk(P_QM), wblk(P_KM), wblk(P_VM), wblk(P_OM),
            pl.BlockSpec((None, CHUNK, LANES), lambda b, c: (b, c, 0)),
            full((CONV_K, 2 * M_WIDTH)), full((1, 2 * M_WIDTH)),
            full((SUBLANES, LANES)), full((1, M_WIDTH)),
        ],
        out_specs=pl.BlockSpec((None, CHUNK, M_WIDTH), lambda b, c: (b, c, 0)),
        out_shape=jax.ShapeDtypeStruct((B, S, M_WIDTH), BF16),
        scratch_shapes=[
            pltpu.VMEM((M_HEADS, M_DH, M_DH), F32),
            pltpu.VMEM((SUBLANES, M_DH), F32),
            pltpu.VMEM((SUBLANES, LANES), F32),
            pltpu.VMEM((SUBLANES, M_WIDTH), F32),
            pltpu.VMEM((SUBLANES, M_WIDTH), F32),
        ],
        compiler_params=pltpu.CompilerParams(
            dimension_semantics=("parallel", "arbitrary"), vmem_limit_bytes=VMEM_LIMIT),
        name="mlstm",
    )(p3, p3, p3, p3, if3, conv_w, conv_b, bif, ng)


def _attn_kernel(q_ref, kc_ref, kp_ref, vc_ref, vp_ref, bias_ref, o_ref, lse_ref, *, tq):
    first = (pl.program_id(2) == 0).astype(jnp.int32)
    lane = lax.broadcasted_iota(jnp.int32, (1, LANES), 1)
    low = lane < A_DH
    for i in range(tq // A_BLK):
        rows = slice(i * A_BLK, (i + 1) * A_BLK)
        if i == 0:
            k2 = jnp.concatenate([kp_ref[...], kc_ref[rows]], axis=0)
            v2 = jnp.concatenate([vp_ref[...], vc_ref[rows]], axis=0)
            var = first
        else:
            k2 = kc_ref[(i - 1) * A_BLK:(i + 1) * A_BLK]
            v2 = vc_ref[(i - 1) * A_BLK:(i + 1) * A_BLK]
            var = 0
        q = q_ref[rows] * (A_DH ** -0.5)
        lse_blk = jnp.zeros((A_BLK, LANES), F32)
        for p in range(A_HG // 2):
            cs = slice(p * LANES, (p + 1) * LANES)
            qp, kp2, vp2 = q[:, cs], k2[:, cs], v2[:, cs]
            o_pair = jnp.zeros((A_BLK, LANES), F32)
            for e in range(2):
                h = 2 * p + e
                sel = low if e == 0 else jnp.logical_not(low)
                qm = jnp.where(sel, qp, jnp.zeros_like(qp))
                vm = jnp.where(sel, vp2, jnp.zeros_like(vp2))
                s = lax.dot_general(qm, kp2, (((1,), (1,)), ((), ())),
                                    preferred_element_type=F32) + bias_ref[var, h]
                mx = jnp.max(s, axis=1, keepdims=True)
                pr = jnp.exp(s - mx)
                l = jnp.sum(pr, axis=1, keepdims=True)
                o_pair = o_pair + jnp.dot(pr.astype(BF16), vm, preferred_element_type=F32) / l
                lse_blk = jnp.where(lane == h, mx + jnp.log(l), lse_blk)
            o_ref[rows, cs] = o_pair.astype(BF16)
        lse_ref[rows] = lse_blk


def _attn_group(p_flat, bias_g, g, B, S, tq=256):
    _, d = GROUPS[g]
    sd = S // d
    tq = min(tq, sd)
    nblk = P_WIDTH // A_GW
    pv = p_flat.reshape(B, sd, d * P_WIDTH)
    cur = lambda off: pl.BlockSpec(
        (None, tq, A_GW), lambda b, r, n: (b, n, r * nblk + off // A_GW + g))
    prev = lambda off: pl.BlockSpec(
        (None, A_BLK, A_GW),
        lambda b, r, n: (b, jnp.maximum(n * (tq // A_BLK) - 1, 0), r * nblk + off // A_GW + g))
    o, lse = pl.pallas_call(
        functools.partial(_attn_kernel, tq=tq),
        grid=(B, d, sd // tq),
        in_specs=[
            cur(P_QA), cur(P_KA), prev(P_KA), cur(P_VA), prev(P_VA),
            pl.BlockSpec((2, A_HG, A_BLK, 2 * A_BLK), lambda b, r, n: (0, 0, 0, 0)),
        ],
        out_specs=[
            pl.BlockSpec((None, tq, A_GW), lambda b, r, n: (b, n, r)),
            pl.BlockSpec((None, tq, LANES), lambda b, r, n: (b, n, r)),
        ],
        out_shape=[
            jax.ShapeDtypeStruct((B, sd, d * A_GW), BF16),
            jax.ShapeDtypeStruct((B, sd, d * LANES), F32),
        ],
        compiler_params=pltpu.CompilerParams(
            dimension_semantics=("parallel", "parallel", "arbitrary"),
            vmem_limit_bytes=VMEM_LIMIT),
        name=f"attn_g{g}",
    )(pv, pv, pv, pv, pv, bias_g)
    return o.reshape(B * S, A_GW), lse.reshape(B * S, LANES)


def _merge_kernel(x_ref, gate_ref, ya_ref, o0_ref, o1_ref, o2_ref, l0_ref, l1_ref, l2_ref,
                  e_ref, wa_ref, wb_ref, wo_ref, out_ref):
    l0, l1, l2 = l0_ref[...], l1_ref[...], l2_ref[...]
    lm = jnp.maximum(jnp.maximum(l0, l1), l2)
    e0, e1, e2 = jnp.exp(l0 - lm), jnp.exp(l1 - lm), jnp.exp(l2 - lm)
    den = e0 + e1 + e2
    yb = jnp.zeros(o0_ref.shape, F32)
    for e, o_ref in ((e0, o0_ref), (e1, o1_ref), (e2, o2_ref)):
        w = e / den
        w_hi = w.astype(BF16)
        w_lo = (w - w_hi.astype(F32)).astype(BF16)
        wide = (jnp.dot(w_hi, e_ref[...], preferred_element_type=F32)
                + jnp.dot(w_lo, e_ref[...], preferred_element_type=F32))
        yb = yb + wide * o_ref[...].astype(F32)
    pa = jnp.dot(ya_ref[...], wa_ref[...], preferred_element_type=F32)
    pb = jnp.dot(yb.astype(BF16), wb_ref[...], preferred_element_type=F32)
    ga = jax.nn.sigmoid(gate_ref[:, :D_MODEL].astype(F32))
    gb = jax.nn.sigmoid(gate_ref[:, D_MODEL:].astype(F32))
    merged = (ga * pa + gb * pb).astype(BF16)
    out_ref[...] = x_ref[...] + jnp.dot(merged, wo_ref[...], preferred_element_type=F32)


def _merge(x2, p_flat, ya, os_, ls_, expand, wa, wb, wo, tm=512):
    n = x2.shape[0]
    row = lambda w: pl.BlockSpec((tm, w), lambda i: (i, 0))
    full = lambda a: pl.BlockSpec(a.shape, lambda i: (0, 0))
    return pl.pallas_call(
        _merge_kernel,
        grid=(n // tm,),
        in_specs=[row(D_MODEL), row(2 * D_MODEL), row(M_WIDTH),
                  row(A_GW), row(A_GW), row(A_GW), row(LANES), row(LANES), row(LANES),
                  full(expand), full(wa), full(wb), full(wo)],
        out_specs=row(D_MODEL),
        out_shape=jax.ShapeDtypeStruct((n, D_MODEL), F32),
        compiler_params=pltpu.CompilerParams(
            dimension_semantics=("parallel",), vmem_limit_bytes=VMEM_LIMIT),
        name="merge",
    )(x2, p_flat, ya, *os_, *ls_, expand, wa, wb, wo)


FF_CHUNKS = ((0, 1024), (1024, 1024), (2048, 768))


def _rms(x, g):
    return x * lax.rsqrt(jnp.mean(x * x, axis=-1, keepdims=True) + EPS) * g


def _ffn_kernel(x_ref, gf_ref, wg_ref, wu_ref, wd_ref, gl_ref, out_ref):
    x = x_ref[...]
    hf = _rms(x, gf_ref[...]).astype(BF16)
    acc = x
    for start, size in FF_CHUNKS:
        cs = slice(start, start + size)
        gt = jnp.dot(hf, wg_ref[:, cs], preferred_element_type=F32)
        up = jnp.dot(hf, wu_ref[:, cs], preferred_element_type=F32)
        act = (gt * jax.nn.sigmoid(gt) * up).astype(BF16)
        acc = acc + jnp.dot(act, wd_ref[cs, :], preferred_element_type=F32)
    out_ref[...] = _rms(acc, gl_ref[...])


def _ffn(x1, gf, wg, wu, wd, gl, tm=512):
    n = x1.shape[0]
    row = pl.BlockSpec((tm, D_MODEL), lambda i: (i, 0))
    once = lambda a: pl.BlockSpec(a.shape, lambda i: (0, 0), pipeline_mode=pl.Buffered(1))
    return pl.pallas_call(
        _ffn_kernel,
        grid=(n // tm,),
        in_specs=[row, once(gf), once(wg), once(wu), once(wd), once(gl)],
        out_specs=row,
        out_shape=jax.ShapeDtypeStruct((n, D_MODEL), F32),
        compiler_params=pltpu.CompilerParams(
            dimension_semantics=("parallel",), vmem_limit_bytes=VMEM_LIMIT),
        name="ffn",
    )(x1, gf, wg, wu, wd, gl)


def _layer(x2, B, S, bias, norm_mix_g, w_in, b_gate_if, conv_w, conv_b, mlstm_norm_g,
           w_proj_a, w_proj_b, w_out, norm_ffn_g, w_gate, w_up, w_down, final_g):
    sizes = (M_WIDTH, M_WIDTH, M_WIDTH, 2 * M_HEADS, M_WIDTH, A_WIDTH, A_WIDTH, A_WIDTH,
             2 * D_MODEL)
    offs = np.concatenate([[0], np.cumsum(sizes)])
    piece = lambda i: w_in[:, offs[i]:offs[i + 1]]
    w_p = jnp.concatenate([piece(8), piece(0), piece(1), piece(2), piece(4),
                           piece(5), piece(6), piece(7)], axis=1).astype(BF16)
    w_if = jnp.pad(piece(3), ((0, 0), (0, LANES - 2 * M_HEADS)))

    p_flat, if_flat = _inproj(x2, norm_mix_g[None], w_p, w_if)

    bif = jnp.broadcast_to(b_gate_if[:, None], (SUBLANES, LANES))
    ya = _mlstm(p_flat.reshape(B, S, P_WIDTH), if_flat.reshape(B, S, LANES), conv_w,
                conv_b[None], bif, mlstm_norm_g[None]).reshape(B * S, M_WIDTH)

    os_, ls_ = [], []
    for g in range(N_GROUPS):
        o, lse = _attn_group(p_flat, bias[g], g, B, S)
        os_.append(o)
        ls_.append(lse)

    expand = (np.arange(LANES)[:, None] == np.arange(A_GW)[None, :] // A_DH)
    x1 = _merge(x2, p_flat, ya, os_, ls_, jnp.asarray(expand, BF16),
                w_proj_a.astype(BF16), w_proj_b.astype(BF16), w_out.astype(BF16))
    return _ffn(x1, norm_ffn_g[None], w_gate.astype(BF16), w_up.astype(BF16),
                w_down.astype(BF16), final_g[None])


def kernel(x, norm_mix_g, w_in, b_gate_if, conv_w, conv_b, mlstm_norm_g, w_proj_a, w_proj_b,
           w_out, norm_ffn_g, w_gate, w_up, w_down, rel_bias, norm_final_g):
    B, S, _ = x.shape
    depth = w_in.shape[0]
    assert depth == 1, "the final norm is fused into the (single) layer's channel mixer"
    bias = _bias_tables(rel_bias)
    out = _layer(x.reshape(B * S, D_MODEL), B, S, bias, norm_mix_g[0], w_in[0], b_gate_if[0],
                 conv_w[0], conv_b[0], mlstm_norm_g[0], w_proj_a[0], w_proj_b[0], w_out[0],
                 norm_ffn_g[0], w_gate[0], w_up[0], w_down[0], norm_final_g)
    return out.reshape(B, S, D_MODEL)
```

```python
import functools
import math

import numpy as np
import jax
import jax.numpy as jnp
from jax import lax
from jax.experimental import pallas as pl
from jax.experimental.pallas import tpu as pltpu

F32 = jnp.float32
BF16 = jnp.bfloat16

D_MODEL = 1024
M_HEADS = 4
M_DH = 256
M_WIDTH = M_HEADS * M_DH
CONV_K = 4
CHUNK = 128
GROUPS = ((128, 1), (512, 4), (2048, 16))
N_GROUPS = len(GROUPS)
A_HG = 8
A_DH = 64
A_GW = A_HG * A_DH
A_WIDTH = N_GROUPS * A_GW
A_BLK = 128
N_BUCKETS = 32
MAX_DISTANCE = 2048
D_FF = 2816
EPS = 1e-6
NEG = -1e30

LANES = 128
SUBLANES = 8

P_GATE = 0
P_QM = 2048
P_KM = 3072
P_VM = 4096
P_OM = 5120
P_A0 = 6144
P_WIDTH = P_A0 + 3 * A_GW
IN_TN = 3 * A_GW
N_DIL = N_GROUPS - 1

VMEM_LIMIT = 56 * 1024 * 1024


def _bucket_tables():
    i = np.arange(A_BLK)[:, None]
    j = np.arange(2 * A_BLK)[None, :]
    dist = i + A_BLK - j
    buckets = []
    for window, dil in GROUPS:
        n = np.maximum(dist, 0) * dil
        nf = np.maximum(n, 1).astype(np.float32)
        max_exact = N_BUCKETS // 2
        large = max_exact + (np.log(nf / max_exact) / math.log(MAX_DISTANCE / max_exact)
                             * (N_BUCKETS - max_exact)).astype(np.int32)
        large = np.minimum(large, N_BUCKETS - 1)
        buckets.append(np.where(n < max_exact, n, large).astype(np.int32))
    span = GROUPS[0][0] // GROUPS[0][1]
    assert all(w // d == span for w, d in GROUPS)
    valid = ((dist >= 0) & (dist <= span)).astype(np.int32)
    valid_first = (valid.astype(bool) & (j >= A_BLK)).astype(np.int32)
    return np.stack(buckets), np.stack([valid, valid_first])


def _bias_kernel(tab_ref, bucket_ref, valid_ref, out_ref):
    g = pl.program_id(0)
    h = pl.program_id(1)
    col = g * A_HG + h
    bucket = bucket_ref[...]
    acc = jnp.zeros(bucket.shape, F32)
    for b in range(N_BUCKETS):
        acc = jnp.where(bucket == b, tab_ref[b, col], acc)
    out_ref[0] = jnp.where(valid_ref[0] > 0, acc, NEG)
    out_ref[1] = jnp.where(valid_ref[1] > 0, acc, NEG)


def _bias_tables(rel_bias):
    buckets, valid = _bucket_tables()
    return pl.pallas_call(
        _bias_kernel,
        grid=(N_GROUPS, A_HG),
        in_specs=[
            pl.BlockSpec(memory_space=pltpu.SMEM),
            pl.BlockSpec((None, A_BLK, 2 * A_BLK), lambda g, h: (g, 0, 0)),
            pl.BlockSpec((2, A_BLK, 2 * A_BLK), lambda g, h: (0, 0, 0)),
        ],
        out_specs=pl.BlockSpec((None, 2, None, A_BLK, 2 * A_BLK), lambda g, h: (g, 0, h, 0, 0)),
        out_shape=jax.ShapeDtypeStruct((N_GROUPS, 2, A_HG, A_BLK, 2 * A_BLK), F32),
        name="bias_tables",
    )(rel_bias, jnp.asarray(buckets), jnp.asarray(valid))


def _deinterleave_matrix(rows, d):
    out = np.arange(rows)
    src = d * (out % (rows // d)) + out // (rows // d)
    return (src[:, None] == np.arange(rows)[None, :]).astype(np.float32)


def _inproj_kernel(x_ref, g_ref, w_ref, wif_ref, *rest, n_nat):
    perm_refs = rest[:N_DIL]
    p_ref, if_ref = rest[N_DIL:N_DIL + 2]
    a_refs = rest[N_DIL + 2:2 * N_DIL + 2]
    h_ref = rest[-1]
    j = pl.program_id(1)

    @pl.when(j == 0)
    def _():
        x = x_ref[...]
        r = lax.rsqrt(jnp.mean(x * x, axis=-1, keepdims=True) + EPS)
        h = x * r * g_ref[...]
        h_ref[...] = h.astype(BF16)
        if_ref[...] = jnp.dot(h, wif_ref[...], preferred_element_type=F32,
                              precision=lax.Precision.HIGHEST)

    @pl.when(j < n_nat)
    def _():
        p_ref[...] = jnp.dot(h_ref[...], w_ref[...], preferred_element_type=F32).astype(BF16)

    for idx in range(N_DIL):
        @pl.when(j == n_nat + idx)
        def _(perm_ref=perm_refs[idx], a_ref=a_refs[idx]):
            hp = jnp.dot(perm_ref[...], h_ref[...], preferred_element_type=F32).astype(BF16)
            a = jnp.dot(hp, w_ref[...], preferred_element_type=F32).astype(BF16)
            a_ref[...] = a.reshape(a_ref.shape)


def _inproj(x2, g, w_p, w_if, B, S, tm=1024):
    n = x2.shape[0]
    nt = S // tm
    n_nat = P_WIDTH // IN_TN
    dils = [d for _, d in GROUPS[1:]]
    perms = [jnp.asarray(_deinterleave_matrix(tm, d), BF16) for d in dils]
    return pl.pallas_call(
        functools.partial(_inproj_kernel, n_nat=n_nat),
        grid=(n // tm, n_nat + N_DIL),
        in_specs=[
            pl.BlockSpec((tm, D_MODEL), lambda i, j: (i, 0)),
            pl.BlockSpec((1, D_MODEL), lambda i, j: (0, 0)),
            pl.BlockSpec((D_MODEL, IN_TN), lambda i, j: (0, j)),
            pl.BlockSpec((D_MODEL, LANES), lambda i, j: (0, 0)),
        ] + [pl.BlockSpec((tm, tm), lambda i, j: (0, 0)) for _ in dils],
        out_specs=[
            pl.BlockSpec((tm, IN_TN), lambda i, j: (i, jnp.minimum(j, n_nat - 1))),
            pl.BlockSpec((tm, LANES), lambda i, j: (i, 0)),
        ] + [pl.BlockSpec((None, d, tm // d, IN_TN), lambda i, j: (i // nt, 0, i % nt, 0))
             for d in dils],
        out_shape=[
            jax.ShapeDtypeStruct((n, P_WIDTH), BF16),
            jax.ShapeDtypeStruct((n, LANES), F32),
        ] + [jax.ShapeDtypeStruct((B, d, S // d, IN_TN), BF16) for d in dils],
        scratch_shapes=[pltpu.VMEM((tm, D_MODEL), BF16)],
        compiler_params=pltpu.CompilerParams(
            dimension_semantics=("parallel", "arbitrary"), vmem_limit_bytes=VMEM_LIMIT),
        name="inproj",
    )(x2, g, w_p, w_if, *perms)


def _lane_scan(x, op, fill):
    lane = lax.broadcasted_iota(jnp.int32, x.shape, 1)
    sh = 1
    while sh < x.shape[1]:
        x = op(x, jnp.where(lane >= sh, pltpu.roll(x, sh, 1), fill))
        sh *= 2
    return x


def _conv_silu(x, tail, w, b):
    ext = jnp.concatenate([tail, x], axis=0)
    L = x.shape[0]
    y = x * w[CONV_K - 1:CONV_K] + b
    for j in range(1, CONV_K):
        y = y + ext[SUBLANES - j:SUBLANES - j + L] * w[CONV_K - 1 - j:CONV_K - j]
    return y * jax.nn.sigmoid(y)


def _mlstm_kernel(q_ref, k_ref, v_ref, o_ref, if_ref, cw_ref, cb_ref, bif_ref, ng_ref,
                  y_ref, c_ref, n_ref, m_ref, qt_ref, kt_ref):
    L = CHUNK

    @pl.when(pl.program_id(1) == 0)
    def _():
        c_ref[...] = jnp.zeros_like(c_ref)
        n_ref[...] = jnp.zeros_like(n_ref)
        m_ref[...] = jnp.zeros_like(m_ref)
        qt_ref[...] = jnp.zeros_like(qt_ref)
        kt_ref[...] = jnp.zeros_like(kt_ref)

    xq = q_ref[...].astype(F32)
    xk = k_ref[...].astype(F32)
    cw = cw_ref[...]
    cb = cb_ref[...]
    q_all = _conv_silu(xq, qt_ref[...], cw[:, :M_WIDTH], cb[:, :M_WIDTH]) * (M_DH ** -0.5)
    k_all = _conv_silu(xk, kt_ref[...], cw[:, M_WIDTH:], cb[:, M_WIDTH:])
    qt_ref[...] = xq[L - SUBLANES:]
    kt_ref[...] = xk[L - SUBLANES:]

    gates = jnp.transpose(if_ref[...])[:SUBLANES] + bif_ref[...]
    ig = gates[:M_HEADS]
    lf = jax.nn.log_sigmoid(gates[M_HEADS:])
    b = _lane_scan(lf, jnp.add, 0.0)
    a = ig - b
    m_prev = m_ref[...][:M_HEADS]
    mx = jnp.maximum(m_prev, _lane_scan(a, jnp.maximum, NEG))
    w_inter = jnp.exp(m_prev - mx)
    e_neg = jnp.exp(-(b + mx))
    b_last = jnp.broadcast_to(b[:, L - 1:L], b.shape)
    a_end = b_last + a
    m_new = jnp.maximum(b_last + m_prev, jnp.max(a_end, axis=1, keepdims=True))
    decay = jnp.exp(b_last + m_prev - m_new)
    wk = jnp.exp(a_end - m_new)
    rows = jnp.concatenate([-mx, w_inter, e_neg, wk,
                            jnp.zeros((L - 4 * M_HEADS, L), F32)], axis=0)
    cols = jnp.transpose(rows)
    m_ref[...] = jnp.concatenate([m_new, m_new], axis=0)

    tri = (lax.broadcasted_iota(jnp.int32, (L, L), 0)
           >= lax.broadcasted_iota(jnp.int32, (L, L), 1))

    for h in range(M_HEADS):
        sl = slice(h * M_DH, (h + 1) * M_DH)
        q = q_all[:, sl]
        k = k_all[:, sl]
        qb = q.astype(BF16)
        kb = k.astype(BF16)
        vb = v_ref[:, sl]
        u_col = cols[:, h:h + 1]
        wi_col = cols[:, M_HEADS + h:M_HEADS + h + 1]
        en_col = cols[:, 2 * M_HEADS + h:2 * M_HEADS + h + 1]
        wk_col = cols[:, 3 * M_HEADS + h:3 * M_HEADS + h + 1]

        dmat = jnp.where(tri, jnp.exp(u_col + a[h:h + 1]), 0.0)
        s = lax.dot_general(qb, kb, (((1,), (1,)), ((), ())), preferred_element_type=F32)
        w_intra = dmat * s
        c_old = c_ref[h]
        n_old = n_ref[h:h + 1]
        num = (wi_col * jnp.dot(qb, c_old.astype(BF16), preferred_element_type=F32)
               + jnp.dot(w_intra.astype(BF16), vb, preferred_element_type=F32))
        den = (wi_col * jnp.sum(q * n_old, axis=1, keepdims=True)
               + jnp.sum(w_intra, axis=1, keepdims=True))
        hout = num / jnp.maximum(jnp.abs(den), en_col)

        kw = k * wk_col
        dec = jnp.concatenate([decay[h:h + 1], decay[h:h + 1]], axis=1)
        c_ref[h] = dec * c_old + lax.dot_general(
            kw.astype(BF16), vb, (((0,), (0,)), ((), ())), preferred_element_type=F32)
        n_ref[h:h + 1] = dec * n_old + jnp.sum(kw, axis=0, keepdims=True)

        mu = jnp.mean(hout, axis=1, keepdims=True)
        cen = hout - mu
        var = jnp.mean(cen * cen, axis=1, keepdims=True)
        hn = cen * lax.rsqrt(var + EPS) * ng_ref[:, sl]
        y_ref[:, sl] = (jax.nn.sigmoid(o_ref[:, sl].astype(F32)) * hn).astype(BF16)


def _mlstm(p3, if3, conv_w, conv_b, bif, ng):
    B, S, _ = p3.shape
    wblk = lambda col: pl.BlockSpec((None, CHUNK, M_WIDTH), lambda b, c: (b, c, col // M_WIDTH))
    full = lambda shape: pl.BlockSpec(shape, lambda b, c: (0,) * len(shape))
    return pl.pallas_call(
        _mlstm_kernel,
        grid=(B, S // CHUNK),
        in_specs=[
            wblk(P_QM), wblk(P_KM), wblk(P_VM), wblk(P_OM),
            pl.BlockSpec((None, CHUNK, LANES), lambda b, c: (b, c, 0)),
            full((CONV_K, 2 * M_WIDTH)), full((1, 2 * M_WIDTH)),
            full((SUBLANES, LANES)), full((1, M_WIDTH)),
        ],
        out_specs=pl.BlockSpec((None, CHUNK, M_WIDTH), lambda b, c: (b, c, 0)),
        out_shape=jax.ShapeDtypeStruct((B, S, M_WIDTH), BF16),
        scratch_shapes=[
            pltpu.VMEM((M_HEADS, M_DH, M_DH), F32),
            pltpu.VMEM((SUBLANES, M_DH), F32),
            pltpu.VMEM((SUBLANES, LANES), F32),
            pltpu.VMEM((SUBLANES, M_WIDTH), F32),
            pltpu.VMEM((SUBLANES, M_WIDTH), F32),
        ],
        compiler_params=pltpu.CompilerParams(
            dimension_semantics=("parallel", "arbitrary"), vmem_limit_bytes=VMEM_LIMIT),
        name="mlstm",
    )(p3, p3, p3, p3, if3, conv_w, conv_b, bif, ng)


def _attn_kernel(q_ref, kc_ref, kp_ref, vc_ref, vp_ref, bias_ref, o_ref, lse_ref, *, tq):
    first = (pl.program_id(2) == 0).astype(jnp.int32)
    lane = lax.broadcasted_iota(jnp.int32, (1, LANES), 1)
    low = lane < A_DH
    for i in range(tq // A_BLK):
        rows = slice(i * A_BLK, (i + 1) * A_BLK)
        if i == 0:
            k2 = jnp.concatenate([kp_ref[...], kc_ref[rows]], axis=0)
            v2 = jnp.concatenate([vp_ref[...], vc_ref[rows]], axis=0)
            var = first
        else:
            k2 = kc_ref[(i - 1) * A_BLK:(i + 1) * A_BLK]
            v2 = vc_ref[(i - 1) * A_BLK:(i + 1) * A_BLK]
            var = 0
        q = q_ref[rows] * (A_DH ** -0.5)
        lse_blk = jnp.zeros((A_BLK, LANES), F32)
        for p in range(A_HG // 2):
            cs = slice(p * LANES, (p + 1) * LANES)
            qp, kp2, vp2 = q[:, cs], k2[:, cs], v2[:, cs]
            o_pair = jnp.zeros((A_BLK, LANES), F32)
            for e in range(2):
                h = 2 * p + e
                sel = low if e == 0 else jnp.logical_not(low)
                qm = jnp.where(sel, qp, jnp.zeros_like(qp))
                vm = jnp.where(sel, vp2, jnp.zeros_like(vp2))
                s = lax.dot_general(qm, kp2, (((1,), (1,)), ((), ())),
                                    preferred_element_type=F32) + bias_ref[var, h]
                mx = jnp.max(s, axis=1, keepdims=True)
                pr = jnp.exp(s - mx)
                l = jnp.sum(pr, axis=1, keepdims=True)
                o_pair = o_pair + jnp.dot(pr.astype(BF16), vm, preferred_element_type=F32) / l
                lse_blk = jnp.where(lane == h, mx + jnp.log(l), lse_blk)
            o_ref[rows, cs] = o_pair.astype(BF16)
        lse_ref[rows] = lse_blk


def _attn_group(src, col0, bias_g, g, tq=256):
    B, d, sd, _ = src.shape
    tq = min(tq, sd)
    cur = lambda c: pl.BlockSpec((None, None, tq, A_GW), lambda b, r, n: (b, r, n, col0 + c))
    prev = lambda c: pl.BlockSpec(
        (None, None, A_BLK, A_GW),
        lambda b, r, n: (b, r, jnp.maximum(n * (tq // A_BLK) - 1, 0), col0 + c))
    return pl.pallas_call(
        functools.partial(_attn_kernel, tq=tq),
        grid=(B, d, sd // tq),
        in_specs=[
            cur(0), cur(1), prev(1), cur(2), prev(2),
            pl.BlockSpec((2, A_HG, A_BLK, 2 * A_BLK), lambda b, r, n: (0, 0, 0, 0)),
        ],
        out_specs=[
            pl.BlockSpec((None, None, tq, A_GW), lambda b, r, n: (b, r, n, 0)),
            pl.BlockSpec((None, None, tq, LANES), lambda b, r, n: (b, r, n, 0)),
        ],
        out_shape=[
            jax.ShapeDtypeStruct((B, d, sd, A_GW), BF16),
            jax.ShapeDtypeStruct((B, d, sd, LANES), F32),
        ],
        compiler_params=pltpu.CompilerParams(
            dimension_semantics=("parallel", "parallel", "arbitrary"),
            vmem_limit_bytes=VMEM_LIMIT),
        name=f"attn_g{g}",
    )(src, src, src, src, src, bias_g)


def _split_bf16(x, parts):
    out = []
    for _ in range(parts):
        hi = x.astype(BF16)
        out.append(hi)
        x = x - hi.astype(F32)
    return out


def _merge_kernel(x_ref, gate_ref, ya_ref, *rest):
    o_refs = rest[:N_GROUPS]
    l_refs = rest[N_GROUPS:2 * N_GROUPS]
    unperm_refs = rest[2 * N_GROUPS:2 * N_GROUPS + N_DIL]
    e_ref, wa_ref, wb_ref, wo_ref, out_ref = rest[2 * N_GROUPS + N_DIL:]
    tm = x_ref.shape[0]

    outs, lses = [], []
    for g in range(N_GROUPS):
        o = o_refs[g][...].reshape(tm, A_GW)
        l = l_refs[g][...].reshape(tm, LANES)
        if g == 0:
            outs.append(o.astype(F32))
            lses.append(l)
        else:
            u = unperm_refs[g - 1][...]
            outs.append(jnp.dot(u, o, preferred_element_type=F32))
            lses.append(sum(jnp.dot(u, part, preferred_element_type=F32)
                            for part in _split_bf16(l, 3)))

    lm = functools.reduce(jnp.maximum, lses)
    es = [jnp.exp(l - lm) for l in lses]
    den = functools.reduce(jnp.add, es)
    yb = jnp.zeros((tm, A_GW), F32)
    for e, o in zip(es, outs):
        w_parts = _split_bf16(e / den, 2)
        wide = sum(jnp.dot(part, e_ref[...], preferred_element_type=F32) for part in w_parts)
        yb = yb + wide * o
    pa = jnp.dot(ya_ref[...], wa_ref[...], preferred_element_type=F32)
    pb = jnp.dot(yb.astype(BF16), wb_ref[...], preferred_element_type=F32)
    ga = jax.nn.sigmoid(gate_ref[:, :D_MODEL].astype(F32))
    gb = jax.nn.sigmoid(gate_ref[:, D_MODEL:].astype(F32))
    merged = (ga * pa + gb * pb).astype(BF16)
    out_ref[...] = x_ref[...] + jnp.dot(merged, wo_ref[...], preferred_element_type=F32)


def _merge(x2, p_flat, ya, os_, ls_, wa, wb, wo, S, tm=512):
    n = x2.shape[0]
    nt = S // tm
    row = lambda w: pl.BlockSpec((tm, w), lambda i: (i, 0))
    full = lambda a: pl.BlockSpec(a.shape, lambda i: (0, 0))
    grp = lambda a: pl.BlockSpec((None, a.shape[1], tm // a.shape[1], a.shape[3]),
                                 lambda i: (i // nt, 0, i % nt, 0))
    unperms = [jnp.asarray(_deinterleave_matrix(tm, d).T, BF16) for _, d in GROUPS[1:]]
    expand = jnp.asarray(np.arange(LANES)[:, None] == np.arange(A_GW)[None, :] // A_DH, BF16)
    consts = unperms + [expand, wa, wb, wo]
    return pl.pallas_call(
        _merge_kernel,
        grid=(n // tm,),
        in_specs=[row(D_MODEL), row(2 * D_MODEL), row(M_WIDTH)]
                 + [grp(a) for a in os_] + [grp(a) for a in ls_] + [full(a) for a in consts],
        out_specs=row(D_MODEL),
        out_shape=jax.ShapeDtypeStruct((n, D_MODEL), F32),
        compiler_params=pltpu.CompilerParams(
            dimension_semantics=("parallel",), vmem_limit_bytes=VMEM_LIMIT),
        name="merge",
    )(x2, p_flat, ya, *os_, *ls_, *consts)


FF_CHUNKS = ((0, 1024), (1024, 1024), (2048, 768))


def _rms(x, g):
    return x * lax.rsqrt(jnp.mean(x * x, axis=-1, keepdims=True) + EPS) * g


def _ffn_kernel(x_ref, gf_ref, wg_ref, wu_ref, wd_ref, gl_ref, out_ref):
    x = x_ref[...]
    hf = _rms(x, gf_ref[...]).astype(BF16)
    acc = x
    for start, size in FF_CHUNKS:
        cs = slice(start, start + size)
        gt = jnp.dot(hf, wg_ref[:, cs], preferred_element_type=F32)
        up = jnp.dot(hf, wu_ref[:, cs], preferred_element_type=F32)
        act = (gt * jax.nn.sigmoid(gt) * up).astype(BF16)
        acc = acc + jnp.dot(act, wd_ref[cs, :], preferred_element_type=F32)
    out_ref[...] = _rms(acc, gl_ref[...])


def _ffn(x1, gf, wg, wu, wd, gl, tm=512):
    n = x1.shape[0]
    row = pl.BlockSpec((tm, D_MODEL), lambda i: (i, 0))
    once = lambda a: pl.BlockSpec(a.shape, lambda i: (0, 0), pipeline_mode=pl.Buffered(1))
    return pl.pallas_call(
        _ffn_kernel,
        grid=(n // tm,),
        in_specs=[row, once(gf), once(wg), once(wu), once(wd), once(gl)],
        out_specs=row,
        out_shape=jax.ShapeDtypeStruct((n, D_MODEL), F32),
        compiler_params=pltpu.CompilerParams(
            dimension_semantics=("parallel",), vmem_limit_bytes=VMEM_LIMIT),
        name="ffn",
    )(x1, gf, wg, wu, wd, gl)


def _layer(x2, B, S, bias, norm_mix_g, w_in, b_gate_if, conv_w, conv_b, mlstm_norm_g,
           w_proj_a, w_proj_b, w_out, norm_ffn_g, w_gate, w_up, w_down, final_g):
    sizes = (M_WIDTH, M_WIDTH, M_WIDTH, 2 * M_HEADS, M_WIDTH, A_WIDTH, A_WIDTH, A_WIDTH,
             2 * D_MODEL)
    offs = np.concatenate([[0], np.cumsum(sizes)])
    piece = lambda i: w_in[:, offs[i]:offs[i + 1]]
    qkv = lambda g: [piece(i)[:, g * A_GW:(g + 1) * A_GW] for i in (5, 6, 7)]
    w_p = jnp.concatenate([piece(8), piece(0), piece(1), piece(2), piece(4)]
                          + [w for g in range(N_GROUPS) for w in qkv(g)], axis=1).astype(BF16)
    w_if = jnp.pad(piece(3), ((0, 0), (0, LANES - 2 * M_HEADS)))

    p_flat, if_flat, *dilated = _inproj(x2, norm_mix_g[None], w_p, w_if, B, S)

    bif = jnp.broadcast_to(b_gate_if[:, None], (SUBLANES, LANES))
    ya = _mlstm(p_flat.reshape(B, S, P_WIDTH), if_flat.reshape(B, S, LANES), conv_w,
                conv_b[None], bif, mlstm_norm_g[None]).reshape(B * S, M_WIDTH)

    os_, ls_ = [], []
    for g in range(N_GROUPS):
        if g == 0:
            o, lse = _attn_group(p_flat.reshape(B, 1, S, P_WIDTH), P_A0 // A_GW, bias[g], g)
        else:
            o, lse = _attn_group(dilated[g - 1], 0, bias[g], g)
        os_.append(o)
        ls_.append(lse)

    x1 = _merge(x2, p_flat, ya, os_, ls_, w_proj_a.astype(BF16), w_proj_b.astype(BF16),
                w_out.astype(BF16), S)
    return _ffn(x1, norm_ffn_g[None], w_gate.astype(BF16), w_up.astype(BF16),
                w_down.astype(BF16), final_g[None])


def kernel(x, norm_mix_g, w_in, b_gate_if, conv_w, conv_b, mlstm_norm_g, w_proj_a, w_proj_b,
           w_out, norm_ffn_g, w_gate, w_up, w_down, rel_bias, norm_final_g):
    B, S, _ = x.shape
    depth = w_in.shape[0]
    assert depth == 1, "the final norm is fused into the (single) layer's channel mixer"
    bias = _bias_tables(rel_bias)
    out = _layer(x.reshape(B * S, D_MODEL), B, S, bias, norm_mix_g[0], w_in[0], b_gate_if[0],
                 conv_w[0], conv_b[0], mlstm_norm_g[0], w_proj_a[0], w_proj_b[0], w_out[0],
                 norm_ffn_g[0], w_gate[0], w_up[0], w_down[0], norm_final_g)
    return out.reshape(B, S, D_MODEL)
```

```python
import functools
import math

import numpy as np
import jax
import jax.numpy as jnp
from jax import lax
from jax.experimental import pallas as pl
from jax.experimental.pallas import tpu as pltpu

F32 = jnp.float32
BF16 = jnp.bfloat16

D_MODEL = 1024
M_HEADS = 4
M_DH = 256
M_WIDTH = M_HEADS * M_DH
CONV_K = 4
CHUNK = 128
GROUPS = ((128, 1), (512, 4), (2048, 16))
N_GROUPS = len(GROUPS)
A_HG = 8
A_DH = 64
A_GW = A_HG * A_DH
A_WIDTH = N_GROUPS * A_GW
A_BLK = 128
N_BUCKETS = 32
MAX_DISTANCE = 2048
D_FF = 2816
EPS = 1e-6
NEG = -1e30

LANES = 128
SUBLANES = 8

P_QM = 0
P_KM = 1024
P_VM = 2048
P_OM = 3072
P_GATE = 4096
P_A0 = 6144
P_WIDTH = P_A0 + 3 * A_GW
IN_TN = 3 * A_GW
N_DIL = N_GROUPS - 1

VMEM_LIMIT = 56 * 1024 * 1024


def _split_bf16(x, parts):
    out = []
    for _ in range(parts):
        hi = x.astype(BF16)
        out.append(hi)
        x = x - hi.astype(F32)
    return out


def _bucket_tables():
    i = np.arange(A_BLK)[:, None]
    j = np.arange(2 * A_BLK)[None, :]
    dist = i + A_BLK - j
    buckets = []
    for window, dil in GROUPS:
        n = np.maximum(dist, 0) * dil
        nf = np.maximum(n, 1).astype(np.float32)
        max_exact = N_BUCKETS // 2
        large = max_exact + (np.log(nf / max_exact) / math.log(MAX_DISTANCE / max_exact)
                             * (N_BUCKETS - max_exact)).astype(np.int32)
        large = np.minimum(large, N_BUCKETS - 1)
        buckets.append(np.where(n < max_exact, n, large).astype(np.int32))
    span = GROUPS[0][0] // GROUPS[0][1]
    assert all(w // d == span for w, d in GROUPS)
    valid = ((dist >= 0) & (dist <= span)).astype(np.int32)
    valid_first = (valid.astype(bool) & (j >= A_BLK)).astype(np.int32)
    return np.stack(buckets), np.stack([valid, valid_first])


def _bias_kernel(tab_ref, bucket_ref, valid_ref, out_ref):
    g = pl.program_id(0)
    h = pl.program_id(1)
    col = g * A_HG + h
    bucket = bucket_ref[...]
    acc = jnp.zeros(bucket.shape, F32)
    for b in range(N_BUCKETS):
        acc = jnp.where(bucket == b, tab_ref[b, col], acc)
    out_ref[0] = jnp.where(valid_ref[0] > 0, acc, NEG)
    out_ref[1] = jnp.where(valid_ref[1] > 0, acc, NEG)


def _bias_tables(rel_bias):
    buckets, valid = _bucket_tables()
    return pl.pallas_call(
        _bias_kernel,
        grid=(N_GROUPS, A_HG),
        in_specs=[
            pl.BlockSpec(memory_space=pltpu.SMEM),
            pl.BlockSpec((None, A_BLK, 2 * A_BLK), lambda g, h: (g, 0, 0)),
            pl.BlockSpec((2, A_BLK, 2 * A_BLK), lambda g, h: (0, 0, 0)),
        ],
        out_specs=pl.BlockSpec((None, 2, None, A_BLK, 2 * A_BLK), lambda g, h: (g, 0, h, 0, 0)),
        out_shape=jax.ShapeDtypeStruct((N_GROUPS, 2, A_HG, A_BLK, 2 * A_BLK), F32),
        name="bias_tables",
    )(rel_bias, jnp.asarray(buckets), jnp.asarray(valid))


def _deinterleave_matrix(rows, d):
    out = np.arange(rows)
    src = d * (out % (rows // d)) + out // (rows // d)
    return (src[:, None] == np.arange(rows)[None, :]).astype(np.float32)


PERM_BLK = 256


def _inproj_kernel(x_ref, g_ref, w_ref, wif_ref, *rest, n_nat, dils):
    perm_refs = rest[:N_DIL]
    p_ref, if_ref = rest[N_DIL:N_DIL + 2]
    a_refs = rest[N_DIL + 2:2 * N_DIL + 2]
    h_ref, hp_ref = rest[-2:]
    tm = x_ref.shape[0]
    j = pl.program_id(1)

    @pl.when(j == 0)
    def _():
        x = x_ref[...]
        r = lax.rsqrt(jnp.mean(x * x, axis=-1, keepdims=True) + EPS)
        h = x * r * g_ref[...]
        h_hi, h_lo = _split_bf16(h, 2)
        h_ref[...] = h_hi
        nt_dims = (((1,), (1,)), ((), ()))
        acc = (lax.dot_general(wif_ref[...], h_hi, nt_dims, preferred_element_type=F32)
               + lax.dot_general(wif_ref[...], h_lo, nt_dims, preferred_element_type=F32))
        if_ref[...] = acc[:SUBLANES] + acc[SUBLANES:]

    is_gate = (j >= P_OM // IN_TN) & (j < P_A0 // IN_TN)

    @pl.when((j < n_nat) & jnp.logical_not(is_gate))
    def _():
        acc = jnp.dot(h_ref[...], w_ref[...], preferred_element_type=F32)
        p_ref[...] = acc.astype(BF16)

    @pl.when(is_gate)
    def _():
        acc = jnp.dot(h_ref[...], w_ref[...], preferred_element_type=F32)
        p_ref[...] = jax.nn.sigmoid(acc).astype(BF16)

    for idx, d in enumerate(dils):
        @pl.when(j == n_nat + idx)
        def _(perm_ref=perm_refs[idx], a_ref=a_refs[idx], d=d):
            piece = PERM_BLK // d
            for c in range(tm // PERM_BLK):
                hp = jnp.dot(perm_ref[...], h_ref[c * PERM_BLK:(c + 1) * PERM_BLK],
                             preferred_element_type=F32).astype(BF16)
                for r in range(d):
                    dst = r * (tm // d) + c * piece
                    hp_ref[dst:dst + piece] = hp[r * piece:(r + 1) * piece]
            a = jnp.dot(hp_ref[...], w_ref[...], preferred_element_type=F32).astype(BF16)
            a_ref[...] = a.reshape(a_ref.shape)


def _inproj(x2, g, w_p, w_if, B, S, tm=1024):
    n = x2.shape[0]
    nt = S // tm
    n_nat = P_WIDTH // IN_TN
    dils = tuple(d for _, d in GROUPS[1:])
    perms = [jnp.asarray(_deinterleave_matrix(PERM_BLK, d), BF16) for d in dils]
    return pl.pallas_call(
        functools.partial(_inproj_kernel, n_nat=n_nat, dils=dils),
        grid=(n // tm, n_nat + N_DIL),
        in_specs=[
            pl.BlockSpec((tm, D_MODEL), lambda i, j: (i, 0)),
            pl.BlockSpec((1, D_MODEL), lambda i, j: (0, 0)),
            pl.BlockSpec((D_MODEL, IN_TN), lambda i, j: (0, j)),
            pl.BlockSpec((2 * SUBLANES, D_MODEL), lambda i, j: (0, 0)),
        ] + [pl.BlockSpec((PERM_BLK, PERM_BLK), lambda i, j: (0, 0)) for _ in dils],
        out_specs=[
            pl.BlockSpec((tm, IN_TN), lambda i, j: (i, jnp.minimum(j, n_nat - 1))),
            pl.BlockSpec((SUBLANES, tm), lambda i, j: (0, i)),
        ] + [pl.BlockSpec((None, d, tm // d, IN_TN), lambda i, j: (i // nt, 0, i % nt, 0))
             for d in dils],
        out_shape=[
            jax.ShapeDtypeStruct((n, P_WIDTH), BF16),
            jax.ShapeDtypeStruct((SUBLANES, n), F32),
        ] + [jax.ShapeDtypeStruct((B, d, S // d, IN_TN), BF16) for d in dils],
        scratch_shapes=[pltpu.VMEM((tm, D_MODEL), BF16), pltpu.VMEM((tm, D_MODEL), BF16)],
        compiler_params=pltpu.CompilerParams(
            dimension_semantics=("parallel", "arbitrary"), vmem_limit_bytes=VMEM_LIMIT),
        name="inproj",
    )(x2, g, w_p, w_if, *perms)


def _conv_shift_matrix(L):
    return np.concatenate([np.eye(L, k=-(CONV_K - 1 - j)) for j in range(CONV_K)], axis=1)


def _conv_silu(x_ref, tail_ref, shift_ref, cw, cb):
    L, C = x_ref.shape
    pack = 2 * SUBLANES
    x3 = x_ref[...].reshape(L // pack, pack, C)
    prods = []
    for j in range(CONV_K):
        wj = jnp.broadcast_to(cw[j:j + 1], (pack, C)).astype(BF16)
        prods.append((x3 * wj[None]).reshape(L, C))
    y = jnp.dot(shift_ref[...], jnp.concatenate(prods, axis=0),
                preferred_element_type=F32) + cb
    tail = tail_ref[...]
    row = lax.broadcasted_iota(jnp.int32, tail.shape, 0)
    fix = jnp.zeros(tail.shape, F32)
    for k in range(1, CONV_K):
        tap = pltpu.roll(tail, k, 0) * cw[CONV_K - 1 - k:CONV_K - k]
        fix = fix + jnp.where(row < k, tap, 0.0)
    y = jnp.concatenate([y[:SUBLANES] + fix, y[SUBLANES:]], axis=0)
    tail_ref[...] = x_ref[L - pack:].astype(F32)[pack - SUBLANES:]
    return y * jax.nn.sigmoid(y)


GATE_ROWS = 32


def _gates_kernel(ift_ref, bif_ref, out_ref):
    S = ift_ref.shape[1]
    nc = S // CHUNK
    g = ift_ref[...] + jnp.concatenate([bif_ref[...]] * nc, axis=1)
    ig = g[:M_HEADS]
    lf = jax.nn.log_sigmoid(g[M_HEADS:])
    pos = lax.broadcasted_iota(jnp.int32, lf.shape, 1) % CHUNK

    def scan(x, op, fill):
        sh = 1
        while sh < CHUNK:
            x = op(x, jnp.where(pos >= sh, pltpu.roll(x, sh, 1), fill))
            sh *= 2
        return x

    def last(x):
        x = jnp.where(pos == CHUNK - 1, x, NEG)
        sh = 1
        while sh < CHUNK:
            x = jnp.maximum(x, jnp.where(pos < CHUNK - sh, pltpu.roll(x, S - sh, 1), NEG))
            sh *= 2
        return x

    b = scan(lf, jnp.add, 0.0)
    a = ig - b
    b_last = last(b)
    a_end = b_last + a
    a_max = last(scan(a_end, jnp.maximum, NEG))
    m = jnp.zeros((M_HEADS, CHUNK), F32)
    m_prev = []
    for c in range(nc):
        m_prev.append(m)
        m = jnp.maximum(b_last[:, c * CHUNK:(c + 1) * CHUNK] + m,
                        a_max[:, c * CHUNK:(c + 1) * CHUNK])
    m_prev = jnp.concatenate(m_prev, axis=1)
    m_new = jnp.maximum(b_last + m_prev, a_max)
    mx = jnp.maximum(m_prev, scan(a, jnp.maximum, NEG))
    out_ref[...] = jnp.concatenate(
        [-mx, jnp.exp(m_prev - mx), jnp.exp(-(b + mx)), jnp.exp(a_end - m_new), a,
         jnp.exp(b_last + m_prev - m_new),
         jnp.zeros((GATE_ROWS - 6 * M_HEADS, S), F32)], axis=0)


def _gates(ift, bif, B, S):
    return pl.pallas_call(
        _gates_kernel,
        grid=(B,),
        in_specs=[pl.BlockSpec((SUBLANES, S), lambda b: (0, b)),
                  pl.BlockSpec((SUBLANES, LANES), lambda b: (0, 0))],
        out_specs=pl.BlockSpec((None, GATE_ROWS, S), lambda b: (b, 0, 0)),
        out_shape=jax.ShapeDtypeStruct((B, GATE_ROWS, S), F32),
        compiler_params=pltpu.CompilerParams(dimension_semantics=("parallel",)),
        name="gates",
    )(ift, bif)


def _mlstm_kernel(q_ref, k_ref, v_ref, o_ref, g_ref, cw_ref, cb_ref, ng_ref, shift_ref,
                  eye_ref, y_ref, c_ref, n_ref, qt_ref, kt_ref):
    L = CHUNK
    nt_dims = (((1,), (1,)), ((), ()))

    @pl.when(pl.program_id(1) == 0)
    def _():
        c_ref[...] = jnp.zeros_like(c_ref)
        n_ref[...] = jnp.zeros_like(n_ref)
        qt_ref[...] = jnp.zeros_like(qt_ref)
        kt_ref[...] = jnp.zeros_like(kt_ref)

    cw = cw_ref[...]
    cb = cb_ref[...]
    tri = (lax.broadcasted_iota(jnp.int32, (L, L), 0)
           >= lax.broadcasted_iota(jnp.int32, (L, L), 1))

    units = []
    for s in range(q_ref.shape[0]):
        q_all = _conv_silu(q_ref.at[s], qt_ref.at[s], shift_ref, cw[:, :M_WIDTH], cb[:, :M_WIDTH])
        q_all = (q_all * (M_DH ** -0.5)).astype(BF16)
        k_all = _conv_silu(k_ref.at[s], kt_ref.at[s], shift_ref, cw[:, M_WIDTH:], cb[:, M_WIDTH:])
        rows = g_ref[s]
        cols = jnp.transpose(rows)
        for h in range(M_HEADS):
            sl = slice(h * M_DH, (h + 1) * M_DH)
            col = lambda i: cols[:, i * M_HEADS + h:i * M_HEADS + h + 1]
            units.append(dict(
                s=s, h=h, sl=sl, qb=q_all[:, sl], k=k_all[:, sl], vb=v_ref[s, :, sl],
                u_col=col(0), wi_col=col(1), en_col=col(2), wk_col=col(3),
                a_row=rows[4 * M_HEADS + h:4 * M_HEADS + h + 1],
                dec_row=rows[5 * M_HEADS + h:5 * M_HEADS + h + 1]))

    for u in units:
        c_old = c_ref[u["s"], u["h"]]
        n_old = n_ref[u["s"], u["h"]:u["h"] + 1]
        kb = u["k"].astype(BF16)
        kn = jnp.concatenate([kb, jnp.broadcast_to(n_old, (L, M_DH)).astype(BF16)], axis=0)
        u["s_aug"] = lax.dot_general(u["qb"], kn, nt_dims, preferred_element_type=F32)
        u["qc"] = jnp.dot(u["qb"], c_old.astype(BF16), preferred_element_type=F32)
        kw = u["k"] * u["wk_col"]
        u["kw_t"] = lax.dot_general(eye_ref[...], kw.astype(BF16), nt_dims,
                                    preferred_element_type=F32).astype(BF16)
        dec = jnp.concatenate([u["dec_row"], u["dec_row"]], axis=1)
        u["c_dec"] = dec * c_old
        n_ref[u["s"], u["h"]:u["h"] + 1] = dec * n_old + jnp.sum(kw, axis=0, keepdims=True)

    for u in units:
        dmat = jnp.where(tri, jnp.exp(u["u_col"] + u["a_row"]), 0.0)
        w_intra = dmat * u["s_aug"][:, :L]
        u["w_sum"] = jnp.sum(w_intra, axis=1, keepdims=True)
        u["w_intra"] = w_intra.astype(BF16)

    for u in units:
        u["pv"] = jnp.dot(u["w_intra"], u["vb"], preferred_element_type=F32)
        c_ref[u["s"], u["h"]] = u["c_dec"] + jnp.dot(u["kw_t"], u["vb"],
                                                     preferred_element_type=F32)

    for u in units:
        num = u["wi_col"] * u["qc"] + u["pv"]
        den = u["wi_col"] * u["s_aug"][:, L:L + 1] + u["w_sum"]
        hout = num / jnp.maximum(jnp.abs(den), u["en_col"])
        mu = jnp.mean(hout, axis=1, keepdims=True)
        u["cen"] = hout - mu
        u["var"] = jnp.mean(u["cen"] * u["cen"], axis=1, keepdims=True)

    for u in units:
        hn = u["cen"] * lax.rsqrt(u["var"] + EPS) * ng_ref[:, u["sl"]]
        y_ref[u["s"], :, u["sl"]] = (o_ref[u["s"], :, u["sl"]].astype(F32) * hn).astype(BF16)


def _mlstm(p3, gates, conv_w, conv_b, ng, nseq=2):
    B, S, _ = p3.shape
    nseq = nseq if B % nseq == 0 else 1
    wblk = lambda col: pl.BlockSpec((nseq, CHUNK, M_WIDTH), lambda b, c: (b, c, col // M_WIDTH))
    full = lambda shape: pl.BlockSpec(shape, lambda b, c: (0,) * len(shape))
    return pl.pallas_call(
        _mlstm_kernel,
        grid=(B // nseq, S // CHUNK),
        in_specs=[
            wblk(P_QM), wblk(P_KM), wblk(P_VM), wblk(P_OM),
            pl.BlockSpec((nseq, GATE_ROWS, CHUNK), lambda b, c: (b, 0, c)),
            full((CONV_K, 2 * M_WIDTH)), full((1, 2 * M_WIDTH)), full((1, M_WIDTH)),
            full((CHUNK, CONV_K * CHUNK)), full((M_DH, M_DH)),
        ],
        out_specs=pl.BlockSpec((nseq, CHUNK, M_WIDTH), lambda b, c: (b, c, 0)),
        out_shape=jax.ShapeDtypeStruct((B, S, M_WIDTH), BF16),
        scratch_shapes=[
            pltpu.VMEM((nseq, M_HEADS, M_DH, M_DH), F32),
            pltpu.VMEM((nseq, SUBLANES, M_DH), F32),
            pltpu.VMEM((nseq, SUBLANES, M_WIDTH), F32),
            pltpu.VMEM((nseq, SUBLANES, M_WIDTH), F32),
        ],
        compiler_params=pltpu.CompilerParams(
            dimension_semantics=("parallel", "arbitrary"), vmem_limit_bytes=VMEM_LIMIT),
        name="mlstm",
    )(p3, p3, p3, p3, gates, conv_w, conv_b, ng,
      jnp.asarray(_conv_shift_matrix(CHUNK), BF16), jnp.eye(M_DH, dtype=BF16))


def _attn_kernel(q_ref, kc_ref, kp_ref, vc_ref, vp_ref, bias_ref, o_ref, lse_ref, *, tq):
    first = (pl.program_id(2) == 0).astype(jnp.int32)
    lane = lax.broadcasted_iota(jnp.int32, (1, LANES), 1)
    low = lane < A_DH
    for i in range(tq // A_BLK):
        rows = slice(i * A_BLK, (i + 1) * A_BLK)
        if i == 0:
            k2 = jnp.concatenate([kp_ref[...], kc_ref[rows]], axis=0)
            v2 = jnp.concatenate([vp_ref[...], vc_ref[rows]], axis=0)
            var = first
        else:
            k2 = kc_ref[(i - 1) * A_BLK:(i + 1) * A_BLK]
            v2 = vc_ref[(i - 1) * A_BLK:(i + 1) * A_BLK]
            var = 0
        q = q_ref[rows] * (A_DH ** -0.5)
        lse_blk = jnp.zeros((A_BLK, LANES), F32)
        for p in range(A_HG // 2):
            cs = slice(p * LANES, (p + 1) * LANES)
            qp, kp2, vp2 = q[:, cs], k2[:, cs], v2[:, cs]
            o_pair = jnp.zeros((A_BLK, LANES), F32)
            for e in range(2):
                h = 2 * p + e
                sel = low if e == 0 else jnp.logical_not(low)
                qm = jnp.where(sel, qp, jnp.zeros_like(qp))
                vm = jnp.where(sel, vp2, jnp.zeros_like(vp2))
                s = lax.dot_general(qm, kp2, (((1,), (1,)), ((), ())),
                                    preferred_element_type=F32) + bias_ref[var, h]
                mx = jnp.max(s, axis=1, keepdims=True)
                pr = jnp.exp(s - mx)
                l = jnp.sum(pr, axis=1, keepdims=True)
                o_pair = o_pair + jnp.dot(pr.astype(BF16), vm, preferred_element_type=F32) / l
                lse_blk = jnp.where(lane == h, mx + jnp.log(l), lse_blk)
            o_ref[rows, cs] = o_pair.astype(BF16)
        lse_ref[rows] = lse_blk


def _attn_group(src, col0, bias_g, g, tq=256):
    B, d, sd, _ = src.shape
    tq = min(tq, sd)
    cur = lambda c: pl.BlockSpec((None, None, tq, A_GW), lambda b, r, n: (b, r, n, col0 + c))
    prev = lambda c: pl.BlockSpec(
        (None, None, A_BLK, A_GW),
        lambda b, r, n: (b, r, jnp.maximum(n * (tq // A_BLK) - 1, 0), col0 + c))
    return pl.pallas_call(
        functools.partial(_attn_kernel, tq=tq),
        grid=(B, d, sd // tq),
        in_specs=[
            cur(0), cur(1), prev(1), cur(2), prev(2),
            pl.BlockSpec((2, A_HG, A_BLK, 2 * A_BLK), lambda b, r, n: (0, 0, 0, 0)),
        ],
        out_specs=[
            pl.BlockSpec((None, None, tq, A_GW), lambda b, r, n: (b, r, n, 0)),
            pl.BlockSpec((None, None, tq, LANES), lambda b, r, n: (b, r, n, 0)),
        ],
        out_shape=[
            jax.ShapeDtypeStruct((B, d, sd, A_GW), BF16),
            jax.ShapeDtypeStruct((B, d, sd, LANES), F32),
        ],
        compiler_params=pltpu.CompilerParams(
            dimension_semantics=("parallel", "parallel", "arbitrary"),
            vmem_limit_bytes=VMEM_LIMIT),
        name=f"attn_g{g}",
    )(src, src, src, src, src, bias_g)


def _merge_kernel(x_ref, gate_ref, ya_ref, *rest):
    o_refs = rest[:N_GROUPS]
    l_refs = rest[N_GROUPS:2 * N_GROUPS]
    unperm_refs = rest[2 * N_GROUPS:2 * N_GROUPS + N_DIL]
    e_ref, wa_ref, wb_ref, wo_ref, out_ref = rest[2 * N_GROUPS + N_DIL:]
    tm = x_ref.shape[0]

    outs, lses = [], []
    for g in range(N_GROUPS):
        o = o_refs[g][...].reshape(tm, A_GW)
        l = l_refs[g][...].reshape(tm, LANES)
        if g == 0:
            outs.append(o.astype(F32))
            lses.append(l)
        else:
            u = unperm_refs[g - 1][...]
            outs.append(jnp.dot(u, o, preferred_element_type=F32))
            lses.append(sum(jnp.dot(u, part, preferred_element_type=F32)
                            for part in _split_bf16(l, 3)))

    lm = functools.reduce(jnp.maximum, lses)
    es = [jnp.exp(l - lm) for l in lses]
    den = functools.reduce(jnp.add, es)
    yb = jnp.zeros((tm, A_GW), F32)
    for e, o in zip(es, outs):
        w_parts = _split_bf16(e / den, 2)
        wide = sum(jnp.dot(part, e_ref[...], preferred_element_type=F32) for part in w_parts)
        yb = yb + wide * o
    pa = jnp.dot(ya_ref[...], wa_ref[...], preferred_element_type=F32)
    pb = jnp.dot(yb.astype(BF16), wb_ref[...], preferred_element_type=F32)
    ga = gate_ref[:, :D_MODEL].astype(F32)
    gb = gate_ref[:, D_MODEL:].astype(F32)
    merged = (ga * pa + gb * pb).astype(BF16)
    out_ref[...] = x_ref[...] + jnp.dot(merged, wo_ref[...], preferred_element_type=F32)


def _merge(x2, p_flat, ya, os_, ls_, wa, wb, wo, S, tm=512):
    n = x2.shape[0]
    nt = S // tm
    row = lambda w: pl.BlockSpec((tm, w), lambda i: (i, 0))
    full = lambda a: pl.BlockSpec(a.shape, lambda i: (0, 0))
    grp = lambda a: pl.BlockSpec((None, a.shape[1], tm // a.shape[1], a.shape[3]),
                                 lambda i: (i // nt, 0, i % nt, 0))
    unperms = [jnp.asarray(_deinterleave_matrix(tm, d).T, BF16) for _, d in GROUPS[1:]]
    expand = jnp.asarray(np.arange(LANES)[:, None] == np.arange(A_GW)[None, :] // A_DH, BF16)
    consts = unperms + [expand, wa, wb, wo]
    return pl.pallas_call(
        _merge_kernel,
        grid=(n // tm,),
        in_specs=[row(D_MODEL),
                  pl.BlockSpec((tm, 2 * D_MODEL), lambda i: (i, P_GATE // (2 * D_MODEL))),
                  row(M_WIDTH)]
                 + [grp(a) for a in os_] + [grp(a) for a in ls_] + [full(a) for a in consts],
        out_specs=row(D_MODEL),
        out_shape=jax.ShapeDtypeStruct((n, D_MODEL), F32),
        compiler_params=pltpu.CompilerParams(
            dimension_semantics=("parallel",), vmem_limit_bytes=VMEM_LIMIT),
        name="merge",
    )(x2, p_flat, ya, *os_, *ls_, *consts)


FF_CHUNKS = ((0, 1024), (1024, 1024), (2048, 768))


def _rms(x, g):
    return x * lax.rsqrt(jnp.mean(x * x, axis=-1, keepdims=True) + EPS) * g


def _ffn_kernel(x_ref, gf_ref, wg_ref, wu_ref, wd_ref, gl_ref, out_ref):
    x = x_ref[...]
    hf = _rms(x, gf_ref[...]).astype(BF16)
    acc = x
    for start, size in FF_CHUNKS:
        cs = slice(start, start + size)
        gt = jnp.dot(hf, wg_ref[:, cs], preferred_element_type=F32)
        up = jnp.dot(hf, wu_ref[:, cs], preferred_element_type=F32)
        act = (gt * jax.nn.sigmoid(gt) * up).astype(BF16)
        acc = acc + jnp.dot(act, wd_ref[cs, :], preferred_element_type=F32)
    out_ref[...] = _rms(acc, gl_ref[...])


def _ffn(x1, gf, wg, wu, wd, gl, tm=512):
    n = x1.shape[0]
    row = pl.BlockSpec((tm, D_MODEL), lambda i: (i, 0))
    once = lambda a: pl.BlockSpec(a.shape, lambda i: (0, 0), pipeline_mode=pl.Buffered(1))
    return pl.pallas_call(
        _ffn_kernel,
        grid=(n // tm,),
        in_specs=[row, once(gf), once(wg), once(wu), once(wd), once(gl)],
        out_specs=row,
        out_shape=jax.ShapeDtypeStruct((n, D_MODEL), F32),
        compiler_params=pltpu.CompilerParams(
            dimension_semantics=("parallel",), vmem_limit_bytes=VMEM_LIMIT),
        name="ffn",
    )(x1, gf, wg, wu, wd, gl)


def _layer(x2, B, S, bias, norm_mix_g, w_in, b_gate_if, conv_w, conv_b, mlstm_norm_g,
           w_proj_a, w_proj_b, w_out, norm_ffn_g, w_gate, w_up, w_down, final_g):
    sizes = (M_WIDTH, M_WIDTH, M_WIDTH, 2 * M_HEADS, M_WIDTH, A_WIDTH, A_WIDTH, A_WIDTH,
             2 * D_MODEL)
    offs = np.concatenate([[0], np.cumsum(sizes)])
    piece = lambda i: w_in[:, offs[i]:offs[i + 1]]
    qkv = lambda g: [piece(i)[:, g * A_GW:(g + 1) * A_GW] for i in (5, 6, 7)]
    w_p = jnp.concatenate([piece(0), piece(1), piece(2), piece(4), piece(8)]
                          + [w for g in range(N_GROUPS) for w in qkv(g)], axis=1).astype(BF16)
    w_if = jnp.concatenate(_split_bf16(piece(3).T, 2), axis=0)

    p_flat, if_t, *dilated = _inproj(x2, norm_mix_g[None], w_p, w_if, B, S)

    bif = jnp.broadcast_to(b_gate_if[:, None], (SUBLANES, LANES))
    gates = _gates(if_t, bif, B, S)
    ya = _mlstm(p_flat.reshape(B, S, P_WIDTH), gates, conv_w, conv_b[None],
                mlstm_norm_g[None]).reshape(B * S, M_WIDTH)

    os_, ls_ = [], []
    for g in range(N_GROUPS):
        if g == 0:
            o, lse = _attn_group(p_flat.reshape(B, 1, S, P_WIDTH), P_A0 // A_GW, bias[g], g)
        else:
            o, lse = _attn_group(dilated[g - 1], 0, bias[g], g)
        os_.append(o)
        ls_.append(lse)

    x1 = _merge(x2, p_flat, ya, os_, ls_, w_proj_a.astype(BF16), w_proj_b.astype(BF16),
                w_out.astype(BF16), S)
    return _ffn(x1, norm_ffn_g[None], w_gate.astype(BF16), w_up.astype(BF16),
                w_down.astype(BF16), final_g[None])


def kernel(x, norm_mix_g, w_in, b_gate_if, conv_w, conv_b, mlstm_norm_g, w_proj_a, w_proj_b,
           w_out, norm_ffn_g, w_gate, w_up, w_down, rel_bias, norm_final_g):
    B, S, _ = x.shape
    depth = w_in.shape[0]
    assert depth == 1, "the final norm is fused into the (single) layer's channel mixer"
    bias = _bias_tables(rel_bias)
    out = _layer(x.reshape(B * S, D_MODEL), B, S, bias, norm_mix_g[0], w_in[0], b_gate_if[0],
                 conv_w[0], conv_b[0], mlstm_norm_g[0], w_proj_a[0], w_proj_b[0], w_out[0],
                 norm_ffn_g[0], w_gate[0], w_up[0], w_down[0], norm_final_g)
    return out.reshape(B, S, D_MODEL)
```

```python
import functools
import math

import numpy as np
import jax
import jax.numpy as jnp
from jax import lax
from jax.experimental import pallas as pl
from jax.experimental.pallas import tpu as pltpu

F32 = jnp.float32
BF16 = jnp.bfloat16

D_MODEL = 1024
M_HEADS = 4
M_DH = 256
M_WIDTH = M_HEADS * M_DH
CONV_K = 4
CHUNK = 128
GROUPS = ((128, 1), (512, 4), (2048, 16))
N_GROUPS = len(GROUPS)
A_HG = 8
A_DH = 64
A_GW = A_HG * A_DH
A_WIDTH = N_GROUPS * A_GW
A_BLK = 128
N_BUCKETS = 32
MAX_DISTANCE = 2048
D_FF = 2816
EPS = 1e-6
NEG = -1e30
LOG2E = math.log2(math.e)
QK_SCALE = A_DH ** -0.5 * LOG2E
LSE_PARTS = 3

LANES = 128
SUBLANES = 8

P_QM = 0
P_KM = 1024
P_VM = 2048
P_OM = 3072
P_GATE = 4096
P_A0 = 6144
P_WIDTH = P_A0 + 3 * A_GW
IN_TN = 3 * A_GW
N_DIL = N_GROUPS - 1

VMEM_LIMIT = 56 * 1024 * 1024


def _split_bf16(x, parts):
    out = []
    for _ in range(parts):
        hi = x.astype(BF16)
        out.append(hi)
        x = x - hi.astype(F32)
    return out


def _bucket_tables():
    i = np.arange(A_BLK)[:, None]
    j = np.arange(2 * A_BLK)[None, :]
    dist = i + A_BLK - j
    buckets = []
    for window, dil in GROUPS:
        n = np.maximum(dist, 0) * dil
        nf = np.maximum(n, 1).astype(np.float32)
        max_exact = N_BUCKETS // 2
        large = max_exact + (np.log(nf / max_exact) / math.log(MAX_DISTANCE / max_exact)
                             * (N_BUCKETS - max_exact)).astype(np.int32)
        large = np.minimum(large, N_BUCKETS - 1)
        buckets.append(np.where(n < max_exact, n, large).astype(np.int32))
    span = GROUPS[0][0] // GROUPS[0][1]
    assert all(w // d == span for w, d in GROUPS)
    valid = ((dist >= 0) & (dist <= span)).astype(np.int32)
    valid_first = (valid.astype(bool) & (j >= A_BLK)).astype(np.int32)
    return np.stack(buckets), np.stack([valid, valid_first])


def _bias_kernel(tab_ref, bucket_ref, valid_ref, out_ref):
    g = pl.program_id(0)
    h = pl.program_id(1)
    col = g * A_HG + h
    bucket = bucket_ref[...]
    acc = jnp.zeros(bucket.shape, F32)
    for b in range(N_BUCKETS):
        acc = jnp.where(bucket == b, tab_ref[b, col], acc)
    acc = acc * LOG2E
    out_ref[0] = jnp.where(valid_ref[0] > 0, acc, NEG)
    out_ref[1] = jnp.where(valid_ref[1] > 0, acc, NEG)


def _bias_tables(rel_bias):
    buckets, valid = _bucket_tables()
    return pl.pallas_call(
        _bias_kernel,
        grid=(N_GROUPS, A_HG),
        in_specs=[
            pl.BlockSpec(memory_space=pltpu.SMEM),
            pl.BlockSpec((None, A_BLK, 2 * A_BLK), lambda g, h: (g, 0, 0)),
            pl.BlockSpec((2, A_BLK, 2 * A_BLK), lambda g, h: (0, 0, 0)),
        ],
        out_specs=pl.BlockSpec((None, 2, None, A_BLK, 2 * A_BLK), lambda g, h: (g, 0, h, 0, 0)),
        out_shape=jax.ShapeDtypeStruct((N_GROUPS, 2, A_HG, A_BLK, 2 * A_BLK), F32),
        name="bias_tables",
    )(rel_bias, jnp.asarray(buckets), jnp.asarray(valid))


def _deinterleave_matrix(rows, d):
    out = np.arange(rows)
    src = d * (out % (rows // d)) + out // (rows // d)
    return (src[:, None] == np.arange(rows)[None, :]).astype(np.float32)


PERM_BLK = 256


def _inproj_kernel(x_ref, g_ref, w_ref, wif_ref, *rest, n_nat, dils):
    perm_refs = rest[:N_DIL]
    p_ref, if_ref = rest[N_DIL:N_DIL + 2]
    a_refs = rest[N_DIL + 2:2 * N_DIL + 2]
    h_ref, hp_ref = rest[-2:]
    tm = x_ref.shape[0]
    j = pl.program_id(1)

    @pl.when(j == 0)
    def _():
        x = x_ref[...]
        r = lax.rsqrt(jnp.mean(x * x, axis=-1, keepdims=True) + EPS)
        h = x * r * g_ref[...]
        h_hi, h_lo = _split_bf16(h, 2)
        h_ref[...] = h_hi
        nt_dims = (((1,), (1,)), ((), ()))
        acc = (lax.dot_general(wif_ref[...], h_hi, nt_dims, preferred_element_type=F32)
               + lax.dot_general(wif_ref[...], h_lo, nt_dims, preferred_element_type=F32))
        if_ref[...] = acc[:SUBLANES] + acc[SUBLANES:]

    is_gate = (j >= P_OM // IN_TN) & (j < P_A0 // IN_TN)

    @pl.when((j < n_nat) & jnp.logical_not(is_gate))
    def _():
        acc = jnp.dot(h_ref[...], w_ref[...], preferred_element_type=F32)
        p_ref[...] = acc.astype(BF16)

    @pl.when(is_gate)
    def _():
        acc = jnp.dot(h_ref[...], w_ref[...], preferred_element_type=F32)
        p_ref[...] = jax.nn.sigmoid(acc).astype(BF16)

    for idx, d in enumerate(dils):
        @pl.when(j == n_nat + idx)
        def _(perm_ref=perm_refs[idx], a_ref=a_refs[idx], d=d):
            piece = PERM_BLK // d
            for c in range(tm // PERM_BLK):
                hp = jnp.dot(perm_ref[...], h_ref[c * PERM_BLK:(c + 1) * PERM_BLK],
                             preferred_element_type=F32).astype(BF16)
                for r in range(d):
                    dst = r * (tm // d) + c * piece
                    hp_ref[dst:dst + piece] = hp[r * piece:(r + 1) * piece]
            a = jnp.dot(hp_ref[...], w_ref[...], preferred_element_type=F32).astype(BF16)
            a_ref[...] = a.reshape(a_ref.shape)


def _inproj(x2, g, w_p, w_if, B, S, tm=1024):
    n = x2.shape[0]
    nt = S // tm
    n_nat = P_WIDTH // IN_TN
    dils = tuple(d for _, d in GROUPS[1:])
    perms = [jnp.asarray(_deinterleave_matrix(PERM_BLK, d), BF16) for d in dils]
    return pl.pallas_call(
        functools.partial(_inproj_kernel, n_nat=n_nat, dils=dils),
        grid=(n // tm, n_nat + N_DIL),
        in_specs=[
            pl.BlockSpec((tm, D_MODEL), lambda i, j: (i, 0)),
            pl.BlockSpec((1, D_MODEL), lambda i, j: (0, 0)),
            pl.BlockSpec((D_MODEL, IN_TN), lambda i, j: (0, j)),
            pl.BlockSpec((2 * SUBLANES, D_MODEL), lambda i, j: (0, 0)),
        ] + [pl.BlockSpec((PERM_BLK, PERM_BLK), lambda i, j: (0, 0)) for _ in dils],
        out_specs=[
            pl.BlockSpec((tm, IN_TN), lambda i, j: (i, jnp.minimum(j, n_nat - 1))),
            pl.BlockSpec((SUBLANES, tm), lambda i, j: (0, i)),
        ] + [pl.BlockSpec((None, d, tm // d, IN_TN), lambda i, j: (i // nt, 0, i % nt, 0))
             for d in dils],
        out_shape=[
            jax.ShapeDtypeStruct((n, P_WIDTH), BF16),
            jax.ShapeDtypeStruct((SUBLANES, n), F32),
        ] + [jax.ShapeDtypeStruct((B, d, S // d, IN_TN), BF16) for d in dils],
        scratch_shapes=[pltpu.VMEM((tm, D_MODEL), BF16), pltpu.VMEM((tm, D_MODEL), BF16)],
        compiler_params=pltpu.CompilerParams(
            dimension_semantics=("parallel", "arbitrary"), vmem_limit_bytes=VMEM_LIMIT),
        name="inproj",
    )(x2, g, w_p, w_if, *perms)


def _conv_shift_matrix(L):
    return np.concatenate([np.eye(L, k=-(CONV_K - 1 - j)) for j in range(CONV_K)], axis=1)


def _conv_silu(x_ref, tail_ref, shift_ref, cw, cb):
    L, C = x_ref.shape
    pack = 2 * SUBLANES
    x3 = x_ref[...].reshape(L // pack, pack, C)
    prods = []
    for j in range(CONV_K):
        wj = jnp.broadcast_to(cw[j:j + 1], (pack, C)).astype(BF16)
        prods.append((x3 * wj[None]).reshape(L, C))
    y = jnp.dot(shift_ref[...], jnp.concatenate(prods, axis=0),
                preferred_element_type=F32) + cb
    tail = tail_ref[...]
    row = lax.broadcasted_iota(jnp.int32, tail.shape, 0)
    fix = jnp.zeros(tail.shape, F32)
    for k in range(1, CONV_K):
        tap = pltpu.roll(tail, k, 0) * cw[CONV_K - 1 - k:CONV_K - k]
        fix = fix + jnp.where(row < k, tap, 0.0)
    y = jnp.concatenate([y[:SUBLANES] + fix, y[SUBLANES:]], axis=0)
    tail_ref[...] = x_ref[L - pack:].astype(F32)[pack - SUBLANES:]
    return y * jax.nn.sigmoid(y)


GATE_ROWS = 32


def _gates_kernel(ift_ref, bif_ref, out_ref):
    S = ift_ref.shape[1]
    nc = S // CHUNK
    g = ift_ref[...] + jnp.concatenate([bif_ref[...]] * nc, axis=1)
    ig = g[:M_HEADS]
    lf = jax.nn.log_sigmoid(g[M_HEADS:])
    pos = lax.broadcasted_iota(jnp.int32, lf.shape, 1) % CHUNK

    def scan(x, op, fill):
        sh = 1
        while sh < CHUNK:
            x = op(x, jnp.where(pos >= sh, pltpu.roll(x, sh, 1), fill))
            sh *= 2
        return x

    def last(x):
        x = jnp.where(pos == CHUNK - 1, x, NEG)
        sh = 1
        while sh < CHUNK:
            x = jnp.maximum(x, jnp.where(pos < CHUNK - sh, pltpu.roll(x, S - sh, 1), NEG))
            sh *= 2
        return x

    b = scan(lf, jnp.add, 0.0)
    a = ig - b
    b_last = last(b)
    a_end = b_last + a
    a_max = last(scan(a_end, jnp.maximum, NEG))
    m = jnp.zeros((M_HEADS, CHUNK), F32)
    m_prev = []
    for c in range(nc):
        m_prev.append(m)
        m = jnp.maximum(b_last[:, c * CHUNK:(c + 1) * CHUNK] + m,
                        a_max[:, c * CHUNK:(c + 1) * CHUNK])
    m_prev = jnp.concatenate(m_prev, axis=1)
    m_new = jnp.maximum(b_last + m_prev, a_max)
    mx = jnp.maximum(m_prev, scan(a, jnp.maximum, NEG))
    out_ref[...] = jnp.concatenate(
        [-mx, jnp.exp(m_prev - mx), jnp.exp(-(b + mx)), jnp.exp(a_end - m_new), a,
         jnp.exp(b_last + m_prev - m_new),
         jnp.zeros((GATE_ROWS - 6 * M_HEADS, S), F32)], axis=0)


def _gates(ift, bif, B, S):
    return pl.pallas_call(
        _gates_kernel,
        grid=(B,),
        in_specs=[pl.BlockSpec((SUBLANES, S), lambda b: (0, b)),
                  pl.BlockSpec((SUBLANES, LANES), lambda b: (0, 0))],
        out_specs=pl.BlockSpec((None, GATE_ROWS, S), lambda b: (b, 0, 0)),
        out_shape=jax.ShapeDtypeStruct((B, GATE_ROWS, S), F32),
        compiler_params=pltpu.CompilerParams(dimension_semantics=("parallel",)),
        name="gates",
    )(ift, bif)


def _mlstm_kernel(q_ref, k_ref, v_ref, o_ref, g_ref, cw_ref, cb_ref, ng_ref, shift_ref,
                  eye_ref, y_ref, c_ref, n_ref, qt_ref, kt_ref):
    L = CHUNK
    nt_dims = (((1,), (1,)), ((), ()))

    @pl.when(pl.program_id(1) == 0)
    def _():
        c_ref[...] = jnp.zeros_like(c_ref)
        n_ref[...] = jnp.zeros_like(n_ref)
        qt_ref[...] = jnp.zeros_like(qt_ref)
        kt_ref[...] = jnp.zeros_like(kt_ref)

    cw = cw_ref[...]
    cb = cb_ref[...]
    tri = (lax.broadcasted_iota(jnp.int32, (L, L), 0)
           >= lax.broadcasted_iota(jnp.int32, (L, L), 1))

    units = []
    for s in range(q_ref.shape[0]):
        q_all = _conv_silu(q_ref.at[s], qt_ref.at[s], shift_ref, cw[:, :M_WIDTH], cb[:, :M_WIDTH])
        q_all = (q_all * (M_DH ** -0.5)).astype(BF16)
        k_all = _conv_silu(k_ref.at[s], kt_ref.at[s], shift_ref, cw[:, M_WIDTH:], cb[:, M_WIDTH:])
        rows = g_ref[s]
        cols = jnp.transpose(rows)
        for h in range(M_HEADS):
            sl = slice(h * M_DH, (h + 1) * M_DH)
            col = lambda i: cols[:, i * M_HEADS + h:i * M_HEADS + h + 1]
            units.append(dict(
                s=s, h=h, sl=sl, qb=q_all[:, sl], k=k_all[:, sl], vb=v_ref[s, :, sl],
                u_col=col(0), wi_col=col(1), en_col=col(2), wk_col=col(3),
                a_row=rows[4 * M_HEADS + h:4 * M_HEADS + h + 1],
                dec_row=rows[5 * M_HEADS + h:5 * M_HEADS + h + 1]))

    for u in units:
        c_old = c_ref[u["s"], u["h"]]
        n_old = n_ref[u["s"], u["h"]:u["h"] + 1]
        kb = u["k"].astype(BF16)
        kn = jnp.concatenate([kb, jnp.broadcast_to(n_old, (L, M_DH)).astype(BF16)], axis=0)
        u["s_aug"] = lax.dot_general(u["qb"], kn, nt_dims, preferred_element_type=F32)
        u["qc"] = jnp.dot(u["qb"], c_old.astype(BF16), preferred_element_type=F32)
        kw = u["k"] * u["wk_col"]
        u["kw_t"] = lax.dot_general(eye_ref[...], kw.astype(BF16), nt_dims,
                                    preferred_element_type=F32).astype(BF16)
        dec = jnp.concatenate([u["dec_row"], u["dec_row"]], axis=1)
        u["c_dec"] = dec * c_old
        n_ref[u["s"], u["h"]:u["h"] + 1] = dec * n_old + jnp.sum(kw, axis=0, keepdims=True)

    for u in units:
        dmat = jnp.where(tri, jnp.exp(u["u_col"] + u["a_row"]), 0.0)
        w_intra = dmat * u["s_aug"][:, :L]
        u["w_sum"] = jnp.sum(w_intra, axis=1, keepdims=True)
        u["w_intra"] = w_intra.astype(BF16)

    for u in units:
        u["pv"] = jnp.dot(u["w_intra"], u["vb"], preferred_element_type=F32)
        c_ref[u["s"], u["h"]] = u["c_dec"] + jnp.dot(u["kw_t"], u["vb"],
                                                     preferred_element_type=F32)

    for u in units:
        num = u["wi_col"] * u["qc"] + u["pv"]
        den = u["wi_col"] * u["s_aug"][:, L:L + 1] + u["w_sum"]
        hout = num / jnp.maximum(jnp.abs(den), u["en_col"])
        mu = jnp.mean(hout, axis=1, keepdims=True)
        u["cen"] = hout - mu
        u["var"] = jnp.mean(u["cen"] * u["cen"], axis=1, keepdims=True)

    for u in units:
        hn = u["cen"] * lax.rsqrt(u["var"] + EPS) * ng_ref[:, u["sl"]]
        y_ref[u["s"], :, u["sl"]] = (o_ref[u["s"], :, u["sl"]].astype(F32) * hn).astype(BF16)


def _mlstm(p3, gates, conv_w, conv_b, ng, nseq=2):
    B, S, _ = p3.shape
    nseq = nseq if B % nseq == 0 else 1
    wblk = lambda col: pl.BlockSpec((nseq, CHUNK, M_WIDTH), lambda b, c: (b, c, col // M_WIDTH))
    full = lambda shape: pl.BlockSpec(shape, lambda b, c: (0,) * len(shape))
    return pl.pallas_call(
        _mlstm_kernel,
        grid=(B // nseq, S // CHUNK),
        in_specs=[
            wblk(P_QM), wblk(P_KM), wblk(P_VM), wblk(P_OM),
            pl.BlockSpec((nseq, GATE_ROWS, CHUNK), lambda b, c: (b, 0, c)),
            full((CONV_K, 2 * M_WIDTH)), full((1, 2 * M_WIDTH)), full((1, M_WIDTH)),
            full((CHUNK, CONV_K * CHUNK)), full((M_DH, M_DH)),
        ],
        out_specs=pl.BlockSpec((nseq, CHUNK, M_WIDTH), lambda b, c: (b, c, 0)),
        out_shape=jax.ShapeDtypeStruct((B, S, M_WIDTH), BF16),
        scratch_shapes=[
            pltpu.VMEM((nseq, M_HEADS, M_DH, M_DH), F32),
            pltpu.VMEM((nseq, SUBLANES, M_DH), F32),
            pltpu.VMEM((nseq, SUBLANES, M_WIDTH), F32),
            pltpu.VMEM((nseq, SUBLANES, M_WIDTH), F32),
        ],
        compiler_params=pltpu.CompilerParams(
            dimension_semantics=("parallel", "arbitrary"), vmem_limit_bytes=VMEM_LIMIT),
        name="mlstm",
    )(p3, p3, p3, p3, gates, conv_w, conv_b, ng,
      jnp.asarray(_conv_shift_matrix(CHUNK), BF16), jnp.eye(M_DH, dtype=BF16))


def _attn_kernel(q_ref, kc_ref, kp_ref, vc_ref, vp_ref, bias_ref, o_ref, lse_ref, *, tq):
    first = (pl.program_id(2) == 0).astype(jnp.int32)
    lane = lax.broadcasted_iota(jnp.int32, (1, LANES), 1)
    low = lane < A_DH
    nt_dims = (((1,), (1,)), ((), ()))
    n_pairs = A_HG // 2
    krow = lax.broadcasted_iota(jnp.int32, (4 * A_BLK, LANES), 0)
    ones_blk = jnp.where((krow < 2 * A_BLK) == low, 1.0, 0.0).astype(BF16)
    for i in range(tq // A_BLK):
        rows = slice(i * A_BLK, (i + 1) * A_BLK)
        if i == 0:
            k2 = jnp.concatenate([kp_ref[...], kc_ref[rows]], axis=0)
            v2 = jnp.concatenate([vp_ref[...], vc_ref[rows]], axis=0)
            var = first
        else:
            k2 = kc_ref[(i - 1) * A_BLK:(i + 1) * A_BLK]
            v2 = vc_ref[(i - 1) * A_BLK:(i + 1) * A_BLK]
            var = 0
        q = q_ref[rows]
        zero = jnp.zeros((A_BLK, LANES), BF16)

        scores = []
        for p in range(n_pairs):
            cs = slice(p * LANES, (p + 1) * LANES)
            qp = q[:, cs]
            q2 = jnp.concatenate([jnp.where(low, qp, zero), jnp.where(low, zero, qp)], axis=0)
            scores.append(lax.dot_general(q2, k2[:, cs], nt_dims, preferred_element_type=F32))

        probs, maxes = [], []
        for p in range(n_pairs):
            for e in range(2):
                s = scores[p][e * A_BLK:(e + 1) * A_BLK] + bias_ref[var, 2 * p + e]
                mx = jnp.max(s, axis=1, keepdims=True)
                probs.append(jnp.exp2(s - mx).astype(BF16))
                maxes.append(mx)

        lse_blk = jnp.zeros((A_BLK, LANES), F32)
        for p in range(n_pairs):
            cs = slice(p * LANES, (p + 1) * LANES)
            vp2 = v2[:, cs]
            zero2 = jnp.zeros_like(vp2)
            v_cat = jnp.concatenate([jnp.where(low, vp2, zero2), jnp.where(low, zero2, vp2)],
                                    axis=0)
            p_cat = jnp.concatenate([probs[2 * p], probs[2 * p + 1]], axis=1)
            o_ext = jnp.dot(p_cat, jnp.concatenate([v_cat, ones_blk], axis=1),
                            preferred_element_type=F32)
            l_pair = o_ext[:, LANES:]
            o_ref[rows, cs] = (o_ext[:, :LANES] * (1.0 / l_pair)).astype(BF16)
            mx_pair = jnp.where(low, maxes[2 * p], maxes[2 * p + 1])
            lse_pair = (mx_pair + jnp.log2(l_pair)) * (1.0 / LOG2E)
            lse_blk = jnp.where(lane % A_DH == p, lse_pair, lse_blk)

        packed = jnp.zeros((A_BLK, LANES), F32)
        rest = lse_blk
        for part in range(LSE_PARTS):
            hi = rest.astype(BF16).astype(F32)
            rest = rest - hi
            packed = packed + (hi if part == 0 else pltpu.roll(hi, part * SUBLANES, 1))
        lse_ref[rows] = packed.astype(BF16)


def _attn_group(src, col0, bias_g, g, tq=512):
    B, d, sd, _ = src.shape
    tq = min(tq, sd)
    cur = lambda c: pl.BlockSpec((None, None, tq, A_GW), lambda b, r, n: (b, r, n, col0 + c))
    prev = lambda c: pl.BlockSpec(
        (None, None, A_BLK, A_GW),
        lambda b, r, n: (b, r, jnp.maximum(n * (tq // A_BLK) - 1, 0), col0 + c))
    return pl.pallas_call(
        functools.partial(_attn_kernel, tq=tq),
        grid=(B, d, sd // tq),
        in_specs=[
            cur(0), cur(1), prev(1), cur(2), prev(2),
            pl.BlockSpec((2, A_HG, A_BLK, 2 * A_BLK), lambda b, r, n: (0, 0, 0, 0)),
        ],
        out_specs=[
            pl.BlockSpec((None, None, tq, A_GW), lambda b, r, n: (b, r, n, 0)),
            pl.BlockSpec((None, None, tq, LANES), lambda b, r, n: (b, r, n, 0)),
        ],
        out_shape=[
            jax.ShapeDtypeStruct((B, d, sd, A_GW), BF16),
            jax.ShapeDtypeStruct((B, d, sd, LANES), BF16),
        ],
        compiler_params=pltpu.CompilerParams(
            dimension_semantics=("parallel", "parallel", "arbitrary"),
            vmem_limit_bytes=VMEM_LIMIT),
        name=f"attn_g{g}",
    )(src, src, src, src, src, bias_g)


def _merge_kernel(x_ref, gate_ref, ya_ref, *rest):
    o_refs = rest[:N_GROUPS]
    l_refs = rest[N_GROUPS:2 * N_GROUPS]
    unperm_refs = rest[2 * N_GROUPS:2 * N_GROUPS + N_DIL]
    e_ref, wa_ref, wb_ref, wo_ref, out_ref = rest[2 * N_GROUPS + N_DIL:]
    tm = x_ref.shape[0]

    outs, lses = [], []
    for g in range(N_GROUPS):
        o = o_refs[g][...].reshape(tm, A_GW)
        lp = l_refs[g][...].reshape(tm, LANES)
        if g == 0:
            o, lp = o.astype(F32), lp.astype(F32)
        else:
            both = jnp.dot(unperm_refs[g - 1][...], jnp.concatenate([o, lp], axis=1),
                           preferred_element_type=F32)
            o, lp = both[:, :A_GW], both[:, A_GW:]
        outs.append(o)
        lses.append(sum(lp if part == 0 else pltpu.roll(lp, LANES - part * SUBLANES, 1)
                        for part in range(LSE_PARTS)))

    lm = functools.reduce(jnp.maximum, lses)
    es = [jnp.exp(l - lm) for l in lses]
    den = functools.reduce(jnp.add, es)
    yb = jnp.zeros((tm, A_GW), F32)
    for e, o in zip(es, outs):
        wide = jnp.dot((e / den).astype(BF16), e_ref[...], preferred_element_type=F32)
        yb = yb + wide * o
    pa = jnp.dot(ya_ref[...], wa_ref[...], preferred_element_type=F32)
    pb = jnp.dot(yb.astype(BF16), wb_ref[...], preferred_element_type=F32)
    ga = gate_ref[:, :D_MODEL].astype(F32)
    gb = gate_ref[:, D_MODEL:].astype(F32)
    merged = (ga * pa + gb * pb).astype(BF16)
    out_ref[...] = x_ref[...] + jnp.dot(merged, wo_ref[...], preferred_element_type=F32)


def _merge(x2, p_flat, ya, os_, ls_, wa, wb, wo, S, tm=512):
    n = x2.shape[0]
    nt = S // tm
    row = lambda w: pl.BlockSpec((tm, w), lambda i: (i, 0))
    full = lambda a: pl.BlockSpec(a.shape, lambda i: (0, 0))
    grp = lambda a: pl.BlockSpec((None, a.shape[1], tm // a.shape[1], a.shape[3]),
                                 lambda i: (i // nt, 0, i % nt, 0))
    unperms = [jnp.asarray(_deinterleave_matrix(tm, d).T, BF16) for _, d in GROUPS[1:]]
    head_lane = (np.arange(A_HG) % 2) * A_DH + np.arange(A_HG) // 2
    expand = jnp.asarray(
        np.arange(LANES)[:, None] == head_lane[np.arange(A_GW) // A_DH][None, :], BF16)
    consts = unperms + [expand, wa, wb, wo]
    return pl.pallas_call(
        _merge_kernel,
        grid=(n // tm,),
        in_specs=[row(D_MODEL),
                  pl.BlockSpec((tm, 2 * D_MODEL), lambda i: (i, P_GATE // (2 * D_MODEL))),
                  row(M_WIDTH)]
                 + [grp(a) for a in os_] + [grp(a) for a in ls_] + [full(a) for a in consts],
        out_specs=row(D_MODEL),
        out_shape=jax.ShapeDtypeStruct((n, D_MODEL), F32),
        compiler_params=pltpu.CompilerParams(
            dimension_semantics=("parallel",), vmem_limit_bytes=VMEM_LIMIT),
        name="merge",
    )(x2, p_flat, ya, *os_, *ls_, *consts)


FF_CHUNKS = ((0, 1024), (1024, 1024), (2048, 768))


def _rms(x, g):
    return x * lax.rsqrt(jnp.mean(x * x, axis=-1, keepdims=True) + EPS) * g


def _ffn_kernel(x_ref, gf_ref, wg_ref, wu_ref, wd_ref, gl_ref, out_ref):
    x = x_ref[...]
    hf = _rms(x, gf_ref[...]).astype(BF16)
    acc = x
    for start, size in FF_CHUNKS:
        cs = slice(start, start + size)
        gt = jnp.dot(hf, wg_ref[:, cs], preferred_element_type=F32)
        up = jnp.dot(hf, wu_ref[:, cs], preferred_element_type=F32)
        act = (gt * jax.nn.sigmoid(gt) * up).astype(BF16)
        acc = acc + jnp.dot(act, wd_ref[cs, :], preferred_element_type=F32)
    out_ref[...] = _rms(acc, gl_ref[...])


def _ffn(x1, gf, wg, wu, wd, gl, tm=512):
    n = x1.shape[0]
    row = pl.BlockSpec((tm, D_MODEL), lambda i: (i, 0))
    once = lambda a: pl.BlockSpec(a.shape, lambda i: (0, 0), pipeline_mode=pl.Buffered(1))
    return pl.pallas_call(
        _ffn_kernel,
        grid=(n // tm,),
        in_specs=[row, once(gf), once(wg), once(wu), once(wd), once(gl)],
        out_specs=row,
        out_shape=jax.ShapeDtypeStruct((n, D_MODEL), F32),
        compiler_params=pltpu.CompilerParams(
            dimension_semantics=("parallel",), vmem_limit_bytes=VMEM_LIMIT),
        name="ffn",
    )(x1, gf, wg, wu, wd, gl)


def _layer(x2, B, S, bias, norm_mix_g, w_in, b_gate_if, conv_w, conv_b, mlstm_norm_g,
           w_proj_a, w_proj_b, w_out, norm_ffn_g, w_gate, w_up, w_down, final_g):
    sizes = (M_WIDTH, M_WIDTH, M_WIDTH, 2 * M_HEADS, M_WIDTH, A_WIDTH, A_WIDTH, A_WIDTH,
             2 * D_MODEL)
    offs = np.concatenate([[0], np.cumsum(sizes)])
    piece = lambda i: w_in[:, offs[i]:offs[i + 1]]
    qkv = lambda g: [piece(i)[:, g * A_GW:(g + 1) * A_GW] * sc
                     for i, sc in ((5, QK_SCALE), (6, 1.0), (7, 1.0))]
    w_p = jnp.concatenate([piece(0), piece(1), piece(2), piece(4), piece(8)]
                          + [w for g in range(N_GROUPS) for w in qkv(g)], axis=1).astype(BF16)
    w_if = jnp.concatenate(_split_bf16(piece(3).T, 2), axis=0)

    p_flat, if_t, *dilated = _inproj(x2, norm_mix_g[None], w_p, w_if, B, S)

    bif = jnp.broadcast_to(b_gate_if[:, None], (SUBLANES, LANES))
    gates = _gates(if_t, bif, B, S)
    ya = _mlstm(p_flat.reshape(B, S, P_WIDTH), gates, conv_w, conv_b[None],
                mlstm_norm_g[None]).reshape(B * S, M_WIDTH)

    os_, ls_ = [], []
    for g in range(N_GROUPS):
        if g == 0:
            o, lse = _attn_group(p_flat.reshape(B, 1, S, P_WIDTH), P_A0 // A_GW, bias[g], g)
        else:
            o, lse = _attn_group(dilated[g - 1], 0, bias[g], g)
        os_.append(o)
        ls_.append(lse)

    x1 = _merge(x2, p_flat, ya, os_, ls_, w_proj_a.astype(BF16), w_proj_b.astype(BF16),
                w_out.astype(BF16), S)
    return _ffn(x1, norm_ffn_g[None], w_gate.astype(BF16), w_up.astype(BF16),
                w_down.astype(BF16), final_g[None])


def kernel(x, norm_mix_g, w_in, b_gate_if, conv_w, conv_b, mlstm_norm_g, w_proj_a, w_proj_b,
           w_out, norm_ffn_g, w_gate, w_up, w_down, rel_bias, norm_final_g):
    B, S, _ = x.shape
    depth = w_in.shape[0]
    assert depth == 1, "the final norm is fused into the (single) layer's channel mixer"
    bias = _bias_tables(rel_bias)
    out = _layer(x.reshape(B * S, D_MODEL), B, S, bias, norm_mix_g[0], w_in[0], b_gate_if[0],
                 conv_w[0], conv_b[0], mlstm_norm_g[0], w_proj_a[0], w_proj_b[0], w_out[0],
                 norm_ffn_g[0], w_gate[0], w_up[0], w_down[0], norm_final_g)
    return out.reshape(B, S, D_MODEL)
```

```python
import functools
import math

import numpy as np
import jax
import jax.numpy as jnp
from jax import lax
from jax.experimental import pallas as pl
from jax.experimental.pallas import tpu as pltpu

F32 = jnp.float32
BF16 = jnp.bfloat16

D_MODEL = 1024
M_HEADS = 4
M_DH = 256
M_WIDTH = M_HEADS * M_DH
CONV_K = 4
CHUNK = 128
GROUPS = ((128, 1), (512, 4), (2048, 16))
N_GROUPS = len(GROUPS)
A_HG = 8
A_DH = 64
A_GW = A_HG * A_DH
A_WIDTH = N_GROUPS * A_GW
A_BLK = 128
N_BUCKETS = 32
MAX_DISTANCE = 2048
D_FF = 2816
EPS = 1e-6
NEG = -1e30
LOG2E = math.log2(math.e)
QK_SCALE = A_DH ** -0.5 * LOG2E
LSE_PARTS = 3

LANES = 128
SUBLANES = 8

P_QM = 0
P_KM = 1024
P_VM = 2048
P_OM = 3072
P_GATE = 4096
P_A0 = 6144
P_WIDTH = P_A0 + 3 * A_GW
IN_TN = 3 * A_GW
N_DIL = N_GROUPS - 1

VMEM_LIMIT = 56 * 1024 * 1024


def _split_bf16(x, parts):
    out = []
    for _ in range(parts):
        hi = x.astype(BF16)
        out.append(hi)
        x = x - hi.astype(F32)
    return out


def _bucket_tables():
    i = np.arange(A_BLK)[:, None]
    j = np.arange(2 * A_BLK)[None, :]
    dist = i + A_BLK - j
    buckets = []
    for window, dil in GROUPS:
        n = np.maximum(dist, 0) * dil
        nf = np.maximum(n, 1).astype(np.float32)
        max_exact = N_BUCKETS // 2
        large = max_exact + (np.log(nf / max_exact) / math.log(MAX_DISTANCE / max_exact)
                             * (N_BUCKETS - max_exact)).astype(np.int32)
        large = np.minimum(large, N_BUCKETS - 1)
        buckets.append(np.where(n < max_exact, n, large).astype(np.int32))
    span = GROUPS[0][0] // GROUPS[0][1]
    assert all(w // d == span for w, d in GROUPS)
    valid = ((dist >= 0) & (dist <= span)).astype(np.int32)
    valid_first = (valid.astype(bool) & (j >= A_BLK)).astype(np.int32)
    return np.stack(buckets), np.stack([valid, valid_first])


def _bias_kernel(tab_ref, bucket_ref, valid_ref, out_ref):
    g = pl.program_id(0)
    h = pl.program_id(1)
    col = g * A_HG + h
    bucket = bucket_ref[...]
    acc = jnp.zeros(bucket.shape, F32)
    for b in range(N_BUCKETS):
        acc = jnp.where(bucket == b, tab_ref[b, col], acc)
    acc = acc * LOG2E
    out_ref[0] = jnp.where(valid_ref[0] > 0, acc, NEG)
    out_ref[1] = jnp.where(valid_ref[1] > 0, acc, NEG)


def _bias_tables(rel_bias):
    buckets, valid = _bucket_tables()
    return pl.pallas_call(
        _bias_kernel,
        grid=(N_GROUPS, A_HG),
        in_specs=[
            pl.BlockSpec(memory_space=pltpu.SMEM),
            pl.BlockSpec((None, A_BLK, 2 * A_BLK), lambda g, h: (g, 0, 0)),
            pl.BlockSpec((2, A_BLK, 2 * A_BLK), lambda g, h: (0, 0, 0)),
        ],
        out_specs=pl.BlockSpec((None, 2, None, A_BLK, 2 * A_BLK), lambda g, h: (g, 0, h, 0, 0)),
        out_shape=jax.ShapeDtypeStruct((N_GROUPS, 2, A_HG, A_BLK, 2 * A_BLK), F32),
        name="bias_tables",
    )(rel_bias, jnp.asarray(buckets), jnp.asarray(valid))


def _deinterleave_matrix(rows, d):
    out = np.arange(rows)
    src = d * (out % (rows // d)) + out // (rows // d)
    return (src[:, None] == np.arange(rows)[None, :]).astype(np.float32)


PERM_BLK = 256
NORM_RB = 256


def _inproj_kernel(x_ref, g_ref, w_ref, wif_ref, *rest, n_nat, dils):
    perm_refs = rest[:N_DIL]
    p_ref, if_ref = rest[N_DIL:N_DIL + 2]
    a_refs = rest[N_DIL + 2:2 * N_DIL + 2]
    h_ref, hp_ref = rest[-2:]
    tm = x_ref.shape[0]
    j = pl.program_id(1)

    @pl.when(j == 0)
    def _():
        nt_dims = (((1,), (1,)), ((), ()))
        for rc in range(tm // NORM_RB):
            rows = slice(rc * NORM_RB, (rc + 1) * NORM_RB)
            x = x_ref[rows]
            r = lax.rsqrt(jnp.mean(x * x, axis=-1, keepdims=True) + EPS)
            h = x * r * g_ref[...]
            h_hi, h_lo = _split_bf16(h, 2)
            h_ref[rows] = h_hi
            acc = (lax.dot_general(wif_ref[...], h_hi, nt_dims, preferred_element_type=F32)
                   + lax.dot_general(wif_ref[...], h_lo, nt_dims, preferred_element_type=F32))
            if_ref[:, rows] = acc[:SUBLANES] + acc[SUBLANES:]
            p_ref[rows] = jnp.dot(h_hi, w_ref[...], preferred_element_type=F32).astype(BF16)

    is_gate = (j >= P_OM // IN_TN) & (j < P_A0 // IN_TN)

    @pl.when((j > 0) & (j < n_nat) & jnp.logical_not(is_gate))
    def _():
        acc = jnp.dot(h_ref[...], w_ref[...], preferred_element_type=F32)
        p_ref[...] = acc.astype(BF16)

    @pl.when(is_gate)
    def _():
        acc = jnp.dot(h_ref[...], w_ref[...], preferred_element_type=F32)
        p_ref[...] = (0.5 * jnp.tanh(0.5 * acc) + 0.5).astype(BF16)

    for idx, d in enumerate(dils):
        @pl.when(j == n_nat + idx)
        def _(perm_ref=perm_refs[idx], a_ref=a_refs[idx], d=d):
            piece = PERM_BLK // d
            for c in range(tm // PERM_BLK):
                hp = jnp.dot(perm_ref[...], h_ref[c * PERM_BLK:(c + 1) * PERM_BLK],
                             preferred_element_type=F32).astype(BF16)
                for r in range(d):
                    dst = r * (tm // d) + c * piece
                    hp_ref[dst:dst + piece] = hp[r * piece:(r + 1) * piece]
            a = jnp.dot(hp_ref[...], w_ref[...], preferred_element_type=F32).astype(BF16)
            a_ref[...] = a.reshape(a_ref.shape)


def _inproj(x2, g, w_p, w_if, B, S, tm=1024):
    n = x2.shape[0]
    nt = S // tm
    n_nat = P_WIDTH // IN_TN
    dils = tuple(d for _, d in GROUPS[1:])
    perms = [jnp.asarray(_deinterleave_matrix(PERM_BLK, d), BF16) for d in dils]
    return pl.pallas_call(
        functools.partial(_inproj_kernel, n_nat=n_nat, dils=dils),
        grid=(n // tm, n_nat + N_DIL),
        in_specs=[
            pl.BlockSpec((tm, D_MODEL), lambda i, j: (i, 0)),
            pl.BlockSpec((1, D_MODEL), lambda i, j: (0, 0)),
            pl.BlockSpec((D_MODEL, IN_TN), lambda i, j: (0, j)),
            pl.BlockSpec((2 * SUBLANES, D_MODEL), lambda i, j: (0, 0)),
        ] + [pl.BlockSpec((PERM_BLK, PERM_BLK), lambda i, j: (0, 0)) for _ in dils],
        out_specs=[
            pl.BlockSpec((tm, IN_TN), lambda i, j: (i, jnp.minimum(j, n_nat - 1))),
            pl.BlockSpec((SUBLANES, tm), lambda i, j: (0, i)),
        ] + [pl.BlockSpec((None, d, tm // d, IN_TN), lambda i, j: (i // nt, 0, i % nt, 0))
             for d in dils],
        out_shape=[
            jax.ShapeDtypeStruct((n, P_WIDTH), BF16),
            jax.ShapeDtypeStruct((SUBLANES, n), F32),
        ] + [jax.ShapeDtypeStruct((B, d, S // d, IN_TN), BF16) for d in dils],
        scratch_shapes=[pltpu.VMEM((tm, D_MODEL), BF16), pltpu.VMEM((tm, D_MODEL), BF16)],
        compiler_params=pltpu.CompilerParams(
            dimension_semantics=("parallel", "arbitrary"), vmem_limit_bytes=VMEM_LIMIT),
        name="inproj",
    )(x2, g, w_p, w_if, *perms)


def _conv_shift_matrix(L):
    return np.concatenate([np.eye(L, k=-(CONV_K - 1 - j)) for j in range(CONV_K)], axis=1)


def _conv_silu(x_ref, tail_ref, shift_ref, cw, cb):
    L, C = x_ref.shape
    pack = 2 * SUBLANES
    x3 = x_ref[...].reshape(L // pack, pack, C)
    prods = []
    for j in range(CONV_K):
        wj = jnp.broadcast_to(cw[j:j + 1], (pack, C)).astype(BF16)
        prods.append((x3 * wj[None]).reshape(L, C))
    y = jnp.dot(shift_ref[...], jnp.concatenate(prods, axis=0),
                preferred_element_type=F32) + cb
    tail = tail_ref[...]
    row = lax.broadcasted_iota(jnp.int32, tail.shape, 0)
    fix = jnp.zeros(tail.shape, F32)
    for k in range(1, CONV_K):
        tap = pltpu.roll(tail, k, 0) * cw[CONV_K - 1 - k:CONV_K - k]
        fix = fix + jnp.where(row < k, tap, 0.0)
    y = jnp.concatenate([y[:SUBLANES] + fix, y[SUBLANES:]], axis=0)
    tail_ref[...] = x_ref[L - pack:].astype(F32)[pack - SUBLANES:]
    return y * jax.nn.sigmoid(y)


GATE_ROWS = 32


def _gates_kernel(ift_ref, bif_ref, out_ref):
    S = ift_ref.shape[1]
    nc = S // CHUNK
    g = ift_ref[...] + jnp.concatenate([bif_ref[...]] * nc, axis=1)
    ig = g[:M_HEADS]
    lf = jax.nn.log_sigmoid(g[M_HEADS:])
    pos = lax.broadcasted_iota(jnp.int32, lf.shape, 1) % CHUNK

    def scan(x, op, fill):
        sh = 1
        while sh < CHUNK:
            x = op(x, jnp.where(pos >= sh, pltpu.roll(x, sh, 1), fill))
            sh *= 2
        return x

    def last(x):
        x = jnp.where(pos == CHUNK - 1, x, NEG)
        sh = 1
        while sh < CHUNK:
            x = jnp.maximum(x, jnp.where(pos < CHUNK - sh, pltpu.roll(x, S - sh, 1), NEG))
            sh *= 2
        return x

    b = scan(lf, jnp.add, 0.0)
    a = ig - b
    b_last = last(b)
    a_end = b_last + a
    a_max = last(scan(a_end, jnp.maximum, NEG))
    m = jnp.zeros((M_HEADS, CHUNK), F32)
    m_prev = []
    for c in range(nc):
        m_prev.append(m)
        m = jnp.maximum(b_last[:, c * CHUNK:(c + 1) * CHUNK] + m,
                        a_max[:, c * CHUNK:(c + 1) * CHUNK])
    m_prev = jnp.concatenate(m_prev, axis=1)
    m_new = jnp.maximum(b_last + m_prev, a_max)
    mx = jnp.maximum(m_prev, scan(a, jnp.maximum, NEG))
    out_ref[...] = jnp.concatenate(
        [-mx, jnp.exp(m_prev - mx), jnp.exp(-(b + mx)), jnp.exp(a_end - m_new), a,
         jnp.exp(b_last + m_prev - m_new),
         jnp.zeros((GATE_ROWS - 6 * M_HEADS, S), F32)], axis=0)


def _gates(ift, bif, B, S):
    return pl.pallas_call(
        _gates_kernel,
        grid=(B,),
        in_specs=[pl.BlockSpec((SUBLANES, S), lambda b: (0, b)),
                  pl.BlockSpec((SUBLANES, LANES), lambda b: (0, 0))],
        out_specs=pl.BlockSpec((None, GATE_ROWS, S), lambda b: (b, 0, 0)),
        out_shape=jax.ShapeDtypeStruct((B, GATE_ROWS, S), F32),
        compiler_params=pltpu.CompilerParams(dimension_semantics=("parallel",)),
        name="gates",
    )(ift, bif)


def _mlstm_kernel(q_ref, k_ref, v_ref, o_ref, g_ref, cw_ref, cb_ref, ng_ref, shift_ref,
                  eye_ref, y_ref, c_ref, n_ref, qt_ref, kt_ref):
    L = CHUNK
    nt_dims = (((1,), (1,)), ((), ()))

    @pl.when(pl.program_id(1) == 0)
    def _():
        c_ref[...] = jnp.zeros_like(c_ref)
        n_ref[...] = jnp.zeros_like(n_ref)
        qt_ref[...] = jnp.zeros_like(qt_ref)
        kt_ref[...] = jnp.zeros_like(kt_ref)

    cw = cw_ref[...]
    cb = cb_ref[...]
    tri = (lax.broadcasted_iota(jnp.int32, (L, L), 0)
           >= lax.broadcasted_iota(jnp.int32, (L, L), 1))

    units = []
    for s in range(q_ref.shape[0]):
        q_all = _conv_silu(q_ref.at[s], qt_ref.at[s], shift_ref, cw[:, :M_WIDTH], cb[:, :M_WIDTH])
        q_all = (q_all * (M_DH ** -0.5)).astype(BF16)
        k_all = _conv_silu(k_ref.at[s], kt_ref.at[s], shift_ref, cw[:, M_WIDTH:], cb[:, M_WIDTH:])
        rows = g_ref[s]
        cols = jnp.transpose(rows)
        for h in range(M_HEADS):
            sl = slice(h * M_DH, (h + 1) * M_DH)
            col = lambda i: cols[:, i * M_HEADS + h:i * M_HEADS + h + 1]
            units.append(dict(
                s=s, h=h, sl=sl, qb=q_all[:, sl], k=k_all[:, sl], vb=v_ref[s, :, sl],
                u_col=col(0), wi_col=col(1), en_col=col(2), wk_col=col(3),
                a_row=rows[4 * M_HEADS + h:4 * M_HEADS + h + 1],
                dec_row=rows[5 * M_HEADS + h:5 * M_HEADS + h + 1]))

    for u in units:
        c_old = c_ref[u["s"], u["h"]]
        n_old = n_ref[u["s"], u["h"]:u["h"] + 1]
        kb = u["k"].astype(BF16)
        kn = jnp.concatenate([kb, jnp.broadcast_to(n_old, (L, M_DH)).astype(BF16)], axis=0)
        u["s_aug"] = lax.dot_general(u["qb"], kn, nt_dims, preferred_element_type=F32)
        u["qc"] = jnp.dot(u["qb"], c_old.astype(BF16), preferred_element_type=F32)
        kw = u["k"] * u["wk_col"]
        u["kw_t"] = lax.dot_general(eye_ref[...], kw.astype(BF16), nt_dims,
                                    preferred_element_type=F32).astype(BF16)
        dec = jnp.concatenate([u["dec_row"], u["dec_row"]], axis=1)
        u["c_dec"] = dec * c_old
        n_ref[u["s"], u["h"]:u["h"] + 1] = dec * n_old + jnp.sum(kw, axis=0, keepdims=True)

    for u in units:
        dmat = jnp.where(tri, jnp.exp(u["u_col"] + u["a_row"]), 0.0)
        w_intra = dmat * u["s_aug"][:, :L]
        u["w_sum"] = jnp.sum(w_intra, axis=1, keepdims=True)
        u["w_intra"] = w_intra.astype(BF16)

    for u in units:
        u["pv"] = jnp.dot(u["w_intra"], u["vb"], preferred_element_type=F32)
        c_ref[u["s"], u["h"]] = u["c_dec"] + jnp.dot(u["kw_t"], u["vb"],
                                                     preferred_element_type=F32)

    for u in units:
        num = u["wi_col"] * u["qc"] + u["pv"]
        den = u["wi_col"] * u["s_aug"][:, L:L + 1] + u["w_sum"]
        hout = num / jnp.maximum(jnp.abs(den), u["en_col"])
        mu = jnp.mean(hout, axis=1, keepdims=True)
        u["cen"] = hout - mu
        u["var"] = jnp.mean(u["cen"] * u["cen"], axis=1, keepdims=True)

    for u in units:
        hn = u["cen"] * lax.rsqrt(u["var"] + EPS) * ng_ref[:, u["sl"]]
        y_ref[u["s"], :, u["sl"]] = (o_ref[u["s"], :, u["sl"]].astype(F32) * hn).astype(BF16)


def _mlstm(p3, gates, conv_w, conv_b, ng, nseq=2):
    B, S, _ = p3.shape
    nseq = nseq if B % nseq == 0 else 1
    wblk = lambda col: pl.BlockSpec((nseq, CHUNK, M_WIDTH), lambda b, c: (b, c, col // M_WIDTH))
    full = lambda shape: pl.BlockSpec(shape, lambda b, c: (0,) * len(shape))
    return pl.pallas_call(
        _mlstm_kernel,
        grid=(B // nseq, S // CHUNK),
        in_specs=[
            wblk(P_QM), wblk(P_KM), wblk(P_VM), wblk(P_OM),
            pl.BlockSpec((nseq, GATE_ROWS, CHUNK), lambda b, c: (b, 0, c)),
            full((CONV_K, 2 * M_WIDTH)), full((1, 2 * M_WIDTH)), full((1, M_WIDTH)),
            full((CHUNK, CONV_K * CHUNK)), full((M_DH, M_DH)),
        ],
        out_specs=pl.BlockSpec((nseq, CHUNK, M_WIDTH), lambda b, c: (b, c, 0)),
        out_shape=jax.ShapeDtypeStruct((B, S, M_WIDTH), BF16),
        scratch_shapes=[
            pltpu.VMEM((nseq, M_HEADS, M_DH, M_DH), F32),
            pltpu.VMEM((nseq, SUBLANES, M_DH), F32),
            pltpu.VMEM((nseq, SUBLANES, M_WIDTH), F32),
            pltpu.VMEM((nseq, SUBLANES, M_WIDTH), F32),
        ],
        compiler_params=pltpu.CompilerParams(
            dimension_semantics=("parallel", "arbitrary"), vmem_limit_bytes=VMEM_LIMIT),
        name="mlstm",
    )(p3, p3, p3, p3, gates, conv_w, conv_b, ng,
      jnp.asarray(_conv_shift_matrix(CHUNK), BF16), jnp.eye(M_DH, dtype=BF16))


def _attn_kernel(q_ref, kc_ref, kp_ref, vc_ref, vp_ref, bias_ref, o_ref, lse_ref, *, tq):
    for r in range(q_ref.shape[0]):
        _attn_rows(q_ref.at[r], kc_ref.at[r], kp_ref.at[r], vc_ref.at[r], vp_ref.at[r], bias_ref,
                   o_ref.at[r], lse_ref.at[r], tq=tq)


def _attn_rows(q_ref, kc_ref, kp_ref, vc_ref, vp_ref, bias_ref, o_ref, lse_ref, *, tq):
    first = (pl.program_id(2) == 0).astype(jnp.int32)
    lane = lax.broadcasted_iota(jnp.int32, (1, LANES), 1)
    low = lane < A_DH
    nt_dims = (((1,), (1,)), ((), ()))
    n_pairs = A_HG // 2
    krow = lax.broadcasted_iota(jnp.int32, (4 * A_BLK, LANES), 0)
    ones_blk = jnp.where((krow < 2 * A_BLK) == low, 1.0, 0.0).astype(BF16)
    for i in range(tq // A_BLK):
        rows = slice(i * A_BLK, (i + 1) * A_BLK)
        if i == 0:
            k2 = jnp.concatenate([kp_ref[...], kc_ref[rows]], axis=0)
            v2 = jnp.concatenate([vp_ref[...], vc_ref[rows]], axis=0)
            var = first
        else:
            k2 = kc_ref[(i - 1) * A_BLK:(i + 1) * A_BLK]
            v2 = vc_ref[(i - 1) * A_BLK:(i + 1) * A_BLK]
            var = 0
        q = q_ref[rows]
        zero = jnp.zeros((A_BLK, LANES), BF16)

        scores = []
        for p in range(n_pairs):
            cs = slice(p * LANES, (p + 1) * LANES)
            qp = q[:, cs]
            q2 = jnp.concatenate([jnp.where(low, qp, zero), jnp.where(low, zero, qp)], axis=0)
            scores.append(lax.dot_general(q2, k2[:, cs], nt_dims, preferred_element_type=F32))

        probs, maxes = [], []
        for p in range(n_pairs):
            for e in range(2):
                s = scores[p][e * A_BLK:(e + 1) * A_BLK] + bias_ref[var, 2 * p + e]
                mx = jnp.max(s, axis=1, keepdims=True)
                probs.append(jnp.exp2(s - mx).astype(BF16))
                maxes.append(mx)

        lse_blk = jnp.zeros((A_BLK, LANES), F32)
        for p in range(n_pairs):
            cs = slice(p * LANES, (p + 1) * LANES)
            vp2 = v2[:, cs]
            zero2 = jnp.zeros_like(vp2)
            v_cat = jnp.concatenate([jnp.where(low, vp2, zero2), jnp.where(low, zero2, vp2)],
                                    axis=0)
            p_cat = jnp.concatenate([probs[2 * p], probs[2 * p + 1]], axis=1)
            o_ext = jnp.dot(p_cat, jnp.concatenate([v_cat, ones_blk], axis=1),
                            preferred_element_type=F32)
            l_pair = o_ext[:, LANES:]
            o_ref[rows, cs] = (o_ext[:, :LANES] * (1.0 / l_pair)).astype(BF16)
            mx_pair = jnp.where(low, maxes[2 * p], maxes[2 * p + 1])
            lse_pair = (mx_pair + jnp.log2(l_pair)) * (1.0 / LOG2E)
            lse_blk = jnp.where(lane % A_DH == p, lse_pair, lse_blk)

        packed = jnp.zeros((A_BLK, LANES), F32)
        rest = lse_blk
        for part in range(LSE_PARTS):
            hi = rest.astype(BF16).astype(F32)
            rest = rest - hi
            packed = packed + (hi if part == 0 else pltpu.roll(hi, part * SUBLANES, 1))
        lse_ref[rows] = packed.astype(BF16)


def _attn_group(src, col0, bias_g, g, rows=512):
    B, d, sd, _ = src.shape
    tq = min(rows, sd)
    nres = min(rows // tq, d)
    cur = lambda c: pl.BlockSpec((None, nres, tq, A_GW), lambda b, r, n: (b, r, n, col0 + c))
    prev = lambda c: pl.BlockSpec(
        (None, nres, A_BLK, A_GW),
        lambda b, r, n: (b, r, jnp.maximum(n * (tq // A_BLK) - 1, 0), col0 + c))
    return pl.pallas_call(
        functools.partial(_attn_kernel, tq=tq),
        grid=(B, d // nres, sd // tq),
        in_specs=[
            cur(0), cur(1), prev(1), cur(2), prev(2),
            pl.BlockSpec((2, A_HG, A_BLK, 2 * A_BLK), lambda b, r, n: (0, 0, 0, 0)),
        ],
        out_specs=[
            pl.BlockSpec((None, nres, tq, A_GW), lambda b, r, n: (b, r, n, 0)),
            pl.BlockSpec((None, nres, tq, LANES), lambda b, r, n: (b, r, n, 0)),
        ],
        out_shape=[
            jax.ShapeDtypeStruct((B, d, sd, A_GW), BF16),
            jax.ShapeDtypeStruct((B, d, sd, LANES), BF16),
        ],
        compiler_params=pltpu.CompilerParams(
            dimension_semantics=("parallel", "parallel", "arbitrary"),
            vmem_limit_bytes=VMEM_LIMIT),
        name=f"attn_g{g}",
    )(src, src, src, src, src, bias_g)


def _merge_kernel(x_ref, gate_ref, ya_ref, *rest):
    o_refs = rest[:N_GROUPS]
    l_refs = rest[N_GROUPS:2 * N_GROUPS]
    unperm_refs = rest[2 * N_GROUPS:2 * N_GROUPS + N_DIL]
    e_ref, wa_ref, wb_ref, wo_ref, out_ref = rest[2 * N_GROUPS + N_DIL:]
    tm = x_ref.shape[0]

    outs, lses = [], []
    for g in range(N_GROUPS):
        o = o_refs[g][...].reshape(tm, A_GW)
        lp = l_refs[g][...].reshape(tm, LANES)
        if g == 0:
            o, lp = o.astype(F32), lp.astype(F32)
        else:
            both = jnp.dot(unperm_refs[g - 1][...], jnp.concatenate([o, lp], axis=1),
                           preferred_element_type=F32)
            o, lp = both[:, :A_GW], both[:, A_GW:]
        outs.append(o)
        lses.append(sum(lp if part == 0 else pltpu.roll(lp, LANES - part * SUBLANES, 1)
                        for part in range(LSE_PARTS)))

    lm = functools.reduce(jnp.maximum, lses)
    es = [jnp.exp(l - lm) for l in lses]
    den = functools.reduce(jnp.add, es)
    yb = jnp.zeros((tm, A_GW), F32)
    for e, o in zip(es, outs):
        wide = jnp.dot((e / den).astype(BF16), e_ref[...], preferred_element_type=F32)
        yb = yb + wide * o
    pa = jnp.dot(ya_ref[...], wa_ref[...], preferred_element_type=F32)
    pb = jnp.dot(yb.astype(BF16), wb_ref[...], preferred_element_type=F32)
    ga = gate_ref[:, :D_MODEL].astype(F32)
    gb = gate_ref[:, D_MODEL:].astype(F32)
    merged = (ga * pa + gb * pb).astype(BF16)
    out_ref[...] = x_ref[...] + jnp.dot(merged, wo_ref[...], preferred_element_type=F32)


def _merge(x2, p_flat, ya, os_, ls_, wa, wb, wo, S, tm=512):
    n = x2.shape[0]
    nt = S // tm
    row = lambda w: pl.BlockSpec((tm, w), lambda i: (i, 0))
    full = lambda a: pl.BlockSpec(a.shape, lambda i: (0, 0))
    grp = lambda a: pl.BlockSpec((None, a.shape[1], tm // a.shape[1], a.shape[3]),
                                 lambda i: (i // nt, 0, i % nt, 0))
    unperms = [jnp.asarray(_deinterleave_matrix(tm, d).T, BF16) for _, d in GROUPS[1:]]
    head_lane = (np.arange(A_HG) % 2) * A_DH + np.arange(A_HG) // 2
    expand = jnp.asarray(
        np.arange(LANES)[:, None] == head_lane[np.arange(A_GW) // A_DH][None, :], BF16)
    consts = unperms + [expand, wa, wb, wo]
    return pl.pallas_call(
        _merge_kernel,
        grid=(n // tm,),
        in_specs=[row(D_MODEL),
                  pl.BlockSpec((tm, 2 * D_MODEL), lambda i: (i, P_GATE // (2 * D_MODEL))),
                  row(M_WIDTH)]
                 + [grp(a) for a in os_] + [grp(a) for a in ls_] + [full(a) for a in consts],
        out_specs=row(D_MODEL),
        out_shape=jax.ShapeDtypeStruct((n, D_MODEL), F32),
        compiler_params=pltpu.CompilerParams(
            dimension_semantics=("parallel",), vmem_limit_bytes=VMEM_LIMIT),
        name="merge",
    )(x2, p_flat, ya, *os_, *ls_, *consts)


FF_CHUNKS = ((0, 1024), (1024, 1024), (2048, 768))


def _rms(x, g):
    return x * lax.rsqrt(jnp.mean(x * x, axis=-1, keepdims=True) + EPS) * g


def _ffn_kernel(x_ref, gf_ref, wg_ref, wu_ref, wd_ref, gl_ref, out_ref):
    x = x_ref[...]
    hf = _rms(x, gf_ref[...]).astype(BF16)
    acc = x
    for start, size in FF_CHUNKS:
        cs = slice(start, start + size)
        gt = jnp.dot(hf, wg_ref[:, cs], preferred_element_type=F32)
        up = jnp.dot(hf, wu_ref[:, cs], preferred_element_type=F32)
        act = (gt * jax.nn.sigmoid(gt) * up).astype(BF16)
        acc = acc + jnp.dot(act, wd_ref[cs, :], preferred_element_type=F32)
    out_ref[...] = _rms(acc, gl_ref[...])


def _ffn(x1, gf, wg, wu, wd, gl, tm=512):
    n = x1.shape[0]
    row = pl.BlockSpec((tm, D_MODEL), lambda i: (i, 0))
    once = lambda a: pl.BlockSpec(a.shape, lambda i: (0, 0), pipeline_mode=pl.Buffered(1))
    return pl.pallas_call(
        _ffn_kernel,
        grid=(n // tm,),
        in_specs=[row, once(gf), once(wg), once(wu), once(wd), once(gl)],
        out_specs=row,
        out_shape=jax.ShapeDtypeStruct((n, D_MODEL), F32),
        compiler_params=pltpu.CompilerParams(
            dimension_semantics=("parallel",), vmem_limit_bytes=VMEM_LIMIT),
        name="ffn",
    )(x1, gf, wg, wu, wd, gl)


def _layer(x2, B, S, bias, norm_mix_g, w_in, b_gate_if, conv_w, conv_b, mlstm_norm_g,
           w_proj_a, w_proj_b, w_out, norm_ffn_g, w_gate, w_up, w_down, final_g):
    sizes = (M_WIDTH, M_WIDTH, M_WIDTH, 2 * M_HEADS, M_WIDTH, A_WIDTH, A_WIDTH, A_WIDTH,
             2 * D_MODEL)
    offs = np.concatenate([[0], np.cumsum(sizes)])
    piece = lambda i: w_in[:, offs[i]:offs[i + 1]]
    qkv = lambda g: [piece(i)[:, g * A_GW:(g + 1) * A_GW] * sc
                     for i, sc in ((5, QK_SCALE), (6, 1.0), (7, 1.0))]
    w_p = jnp.concatenate([piece(0), piece(1), piece(2), piece(4), piece(8)]
                          + [w for g in range(N_GROUPS) for w in qkv(g)], axis=1).astype(BF16)
    w_if = jnp.concatenate(_split_bf16(piece(3).T, 2), axis=0)

    p_flat, if_t, *dilated = _inproj(x2, norm_mix_g[None], w_p, w_if, B, S)

    bif = jnp.broadcast_to(b_gate_if[:, None], (SUBLANES, LANES))
    gates = _gates(if_t, bif, B, S)
    ya = _mlstm(p_flat.reshape(B, S, P_WIDTH), gates, conv_w, conv_b[None],
                mlstm_norm_g[None]).reshape(B * S, M_WIDTH)

    os_, ls_ = [], []
    for g in range(N_GROUPS):
        if g == 0:
            o, lse = _attn_group(p_flat.reshape(B, 1, S, P_WIDTH), P_A0 // A_GW, bias[g], g)
        else:
            o, lse = _attn_group(dilated[g - 1], 0, bias[g], g)
        os_.append(o)
        ls_.append(lse)

    x1 = _merge(x2, p_flat, ya, os_, ls_, w_proj_a.astype(BF16), w_proj_b.astype(BF16),
                w_out.astype(BF16), S)
    return _ffn(x1, norm_ffn_g[None], w_gate.astype(BF16), w_up.astype(BF16),
                w_down.astype(BF16), final_g[None])


def kernel(x, norm_mix_g, w_in, b_gate_if, conv_w, conv_b, mlstm_norm_g, w_proj_a, w_proj_b,
           w_out, norm_ffn_g, w_gate, w_up, w_down, rel_bias, norm_final_g):
    B, S, _ = x.shape
    depth = w_in.shape[0]
    assert depth == 1, "the final norm is fused into the (single) layer's channel mixer"
    bias = _bias_tables(rel_bias)
    out = _layer(x.reshape(B * S, D_MODEL), B, S, bias, norm_mix_g[0], w_in[0], b_gate_if[0],
                 conv_w[0], conv_b[0], mlstm_norm_g[0], w_proj_a[0], w_proj_b[0], w_out[0],
                 norm_ffn_g[0], w_gate[0], w_up[0], w_down[0], norm_final_g)
    return out.reshape(B, S, D_MODEL)
```

```python
import functools
import math

import numpy as np
import jax
import jax.numpy as jnp
from jax import lax
from jax.experimental import pallas as pl
from jax.experimental.pallas import tpu as pltpu

F32 = jnp.float32
BF16 = jnp.bfloat16

D_MODEL = 1024
M_HEADS = 4
M_DH = 256
M_WIDTH = M_HEADS * M_DH
CONV_K = 4
CHUNK = 128
GROUPS = ((128, 1), (512, 4), (2048, 16))
N_GROUPS = len(GROUPS)
A_HG = 8
A_DH = 64
A_GW = A_HG * A_DH
A_WIDTH = N_GROUPS * A_GW
A_BLK = 128
N_BUCKETS = 32
MAX_DISTANCE = 2048
D_FF = 2816
EPS = 1e-6
NEG = -1e30
LOG2E = math.log2(math.e)
QK_SCALE = A_DH ** -0.5 * LOG2E
LSE_PARTS = 3

LANES = 128
SUBLANES = 8

P_QM = 0
P_KM = 1024
P_VM = 2048
P_OM = 3072
P_GATE = 4096
P_A0 = 6144
P_WIDTH = P_A0 + 3 * A_GW
IN_TN = 3 * A_GW
N_DIL = N_GROUPS - 1

VMEM_LIMIT = 56 * 1024 * 1024


def _split_bf16(x, parts):
    out = []
    for _ in range(parts):
        hi = x.astype(BF16)
        out.append(hi)
        x = x - hi.astype(F32)
    return out


def _bucket_tables():
    i = np.arange(A_BLK)[:, None]
    j = np.arange(2 * A_BLK)[None, :]
    dist = i + A_BLK - j
    buckets = []
    for window, dil in GROUPS:
        n = np.maximum(dist, 0) * dil
        nf = np.maximum(n, 1).astype(np.float32)
        max_exact = N_BUCKETS // 2
        large = max_exact + (np.log(nf / max_exact) / math.log(MAX_DISTANCE / max_exact)
                             * (N_BUCKETS - max_exact)).astype(np.int32)
        large = np.minimum(large, N_BUCKETS - 1)
        buckets.append(np.where(n < max_exact, n, large).astype(np.int32))
    span = GROUPS[0][0] // GROUPS[0][1]
    assert all(w // d == span for w, d in GROUPS)
    valid = ((dist >= 0) & (dist <= span)).astype(np.int32)
    valid_first = (valid.astype(bool) & (j >= A_BLK)).astype(np.int32)
    return np.stack(buckets), np.stack([valid, valid_first])


def _bias_kernel(tab_ref, bucket_ref, valid_ref, out_ref):
    g = pl.program_id(0)
    h = pl.program_id(1)
    col = g * A_HG + h
    bucket = bucket_ref[...]
    acc = jnp.zeros(bucket.shape, F32)
    for b in range(N_BUCKETS):
        acc = jnp.where(bucket == b, tab_ref[b, col], acc)
    acc = acc * LOG2E
    out_ref[0] = jnp.where(valid_ref[0] > 0, acc, NEG)
    out_ref[1] = jnp.where(valid_ref[1] > 0, acc, NEG)


def _bias_tables(rel_bias):
    buckets, valid = _bucket_tables()
    return pl.pallas_call(
        _bias_kernel,
        grid=(N_GROUPS, A_HG),
        in_specs=[
            pl.BlockSpec(memory_space=pltpu.SMEM),
            pl.BlockSpec((None, A_BLK, 2 * A_BLK), lambda g, h: (g, 0, 0)),
            pl.BlockSpec((2, A_BLK, 2 * A_BLK), lambda g, h: (0, 0, 0)),
        ],
        out_specs=pl.BlockSpec((None, 2, None, A_BLK, 2 * A_BLK), lambda g, h: (g, 0, h, 0, 0)),
        out_shape=jax.ShapeDtypeStruct((N_GROUPS, 2, A_HG, A_BLK, 2 * A_BLK), F32),
        name="bias_tables",
    )(rel_bias, jnp.asarray(buckets), jnp.asarray(valid))


def _deinterleave_matrix(rows, d):
    out = np.arange(rows)
    src = d * (out % (rows // d)) + out // (rows // d)
    return (src[:, None] == np.arange(rows)[None, :]).astype(np.float32)


PERM_BLK = 256
NORM_RB = 256


def _inproj_kernel(x_ref, g_ref, w_ref, wif_ref, *rest, n_nat, dils):
    perm_refs = rest[:N_DIL]
    p_ref, if_ref = rest[N_DIL:N_DIL + 2]
    a_refs = rest[N_DIL + 2:2 * N_DIL + 2]
    h_ref, hp_ref = rest[-2:]
    tm = x_ref.shape[0]
    j = pl.program_id(1)
    nt_dims = (((1,), (1,)), ((), ()))

    @pl.when(j == 0)
    def _():
        for rc in range(tm // NORM_RB):
            rows = slice(rc * NORM_RB, (rc + 1) * NORM_RB)
            x = x_ref[rows]
            r = lax.rsqrt(jnp.mean(x * x, axis=-1, keepdims=True) + EPS)
            h = x * r * g_ref[...]
            h_hi, h_lo = _split_bf16(h, 2)
            h_ref[rows] = h_hi
            acc = (lax.dot_general(wif_ref[...], h_hi, nt_dims, preferred_element_type=F32)
                   + lax.dot_general(wif_ref[...], h_lo, nt_dims, preferred_element_type=F32))
            if_ref[:, rows] = acc[:SUBLANES] + acc[SUBLANES:]
            acc = lax.dot_general(h_hi, w_ref[...], nt_dims, preferred_element_type=F32)
            p_ref[rows] = acc.astype(BF16)

    is_gate = (j >= P_OM // IN_TN) & (j < P_A0 // IN_TN)

    @pl.when((j > 0) & (j < n_nat) & jnp.logical_not(is_gate))
    def _():
        acc = lax.dot_general(h_ref[...], w_ref[...], nt_dims, preferred_element_type=F32)
        p_ref[...] = acc.astype(BF16)

    @pl.when(is_gate)
    def _():
        acc = lax.dot_general(h_ref[...], w_ref[...], nt_dims, preferred_element_type=F32)
        p_ref[...] = (0.5 * jnp.tanh(0.5 * acc) + 0.5).astype(BF16)

    for idx, d in enumerate(dils):
        @pl.when(j == n_nat + idx)
        def _(perm_ref=perm_refs[idx], a_ref=a_refs[idx], d=d):
            piece = PERM_BLK // d
            for c in range(tm // PERM_BLK):
                hp = jnp.dot(perm_ref[...], h_ref[c * PERM_BLK:(c + 1) * PERM_BLK],
                             preferred_element_type=F32).astype(BF16)
                for r in range(d):
                    dst = r * (tm // d) + c * piece
                    hp_ref[dst:dst + piece] = hp[r * piece:(r + 1) * piece]
            a = lax.dot_general(hp_ref[...], w_ref[...], nt_dims, preferred_element_type=F32)
            a_ref[...] = a.astype(BF16).reshape(a_ref.shape)


def _inproj(x2, g, w_p, w_if, B, S, tm=1024):
    n = x2.shape[0]
    nt = S // tm
    n_nat = P_WIDTH // IN_TN
    dils = tuple(d for _, d in GROUPS[1:])
    perms = [jnp.asarray(_deinterleave_matrix(PERM_BLK, d), BF16) for d in dils]
    return pl.pallas_call(
        functools.partial(_inproj_kernel, n_nat=n_nat, dils=dils),
        grid=(n // tm, n_nat + N_DIL),
        in_specs=[
            pl.BlockSpec((tm, D_MODEL), lambda i, j: (i, 0)),
            pl.BlockSpec((1, D_MODEL), lambda i, j: (0, 0)),
            pl.BlockSpec((IN_TN, D_MODEL), lambda i, j: (j, 0)),
            pl.BlockSpec((2 * SUBLANES, D_MODEL), lambda i, j: (0, 0)),
        ] + [pl.BlockSpec((PERM_BLK, PERM_BLK), lambda i, j: (0, 0)) for _ in dils],
        out_specs=[
            pl.BlockSpec((tm, IN_TN), lambda i, j: (i, jnp.minimum(j, n_nat - 1))),
            pl.BlockSpec((SUBLANES, tm), lambda i, j: (0, i)),
        ] + [pl.BlockSpec((None, d, tm // d, IN_TN), lambda i, j: (i // nt, 0, i % nt, 0))
             for d in dils],
        out_shape=[
            jax.ShapeDtypeStruct((n, P_WIDTH), BF16),
            jax.ShapeDtypeStruct((SUBLANES, n), F32),
        ] + [jax.ShapeDtypeStruct((B, d, S // d, IN_TN), BF16) for d in dils],
        scratch_shapes=[pltpu.VMEM((tm, D_MODEL), BF16), pltpu.VMEM((tm, D_MODEL), BF16)],
        compiler_params=pltpu.CompilerParams(
            dimension_semantics=("parallel", "arbitrary"), vmem_limit_bytes=VMEM_LIMIT),
        name="inproj",
    )(x2, g, w_p, w_if, *perms)


def _conv_shift_matrix(L):
    return np.concatenate([np.eye(L, k=-(CONV_K - 1 - j)) for j in range(CONV_K)], axis=1)


def _conv_silu(x_ref, tail_ref, shift_ref, cw, cb):
    L, C = x_ref.shape
    pack = 2 * SUBLANES
    x3 = x_ref[...].reshape(L // pack, pack, C)
    prods = []
    for j in range(CONV_K):
        wj = jnp.broadcast_to(cw[j:j + 1], (pack, C)).astype(BF16)
        prods.append((x3 * wj[None]).reshape(L, C))
    y = jnp.dot(shift_ref[...], jnp.concatenate(prods, axis=0),
                preferred_element_type=F32) + cb
    tail = tail_ref[...]
    row = lax.broadcasted_iota(jnp.int32, tail.shape, 0)
    fix = jnp.zeros(tail.shape, F32)
    for k in range(1, CONV_K):
        tap = pltpu.roll(tail, k, 0) * cw[CONV_K - 1 - k:CONV_K - k]
        fix = fix + jnp.where(row < k, tap, 0.0)
    y = jnp.concatenate([y[:SUBLANES] + fix, y[SUBLANES:]], axis=0)
    tail_ref[...] = x_ref[L - pack:].astype(F32)[pack - SUBLANES:]
    return y * jax.nn.sigmoid(y)


GATE_ROWS = 32


def _gates_kernel(ift_ref, bif_ref, out_ref):
    S = ift_ref.shape[1]
    nc = S // CHUNK
    g = ift_ref[...] + jnp.concatenate([bif_ref[...]] * nc, axis=1)
    ig = g[:M_HEADS]
    lf = jax.nn.log_sigmoid(g[M_HEADS:])
    pos = lax.broadcasted_iota(jnp.int32, lf.shape, 1) % CHUNK

    def scan(x, op, fill):
        sh = 1
        while sh < CHUNK:
            x = op(x, jnp.where(pos >= sh, pltpu.roll(x, sh, 1), fill))
            sh *= 2
        return x

    def last(x):
        x = jnp.where(pos == CHUNK - 1, x, NEG)
        sh = 1
        while sh < CHUNK:
            x = jnp.maximum(x, jnp.where(pos < CHUNK - sh, pltpu.roll(x, S - sh, 1), NEG))
            sh *= 2
        return x

    b = scan(lf, jnp.add, 0.0)
    a = ig - b
    b_last = last(b)
    a_end = b_last + a
    a_max = last(scan(a_end, jnp.maximum, NEG))
    m = jnp.zeros((M_HEADS, CHUNK), F32)
    m_prev = []
    for c in range(nc):
        m_prev.append(m)
        m = jnp.maximum(b_last[:, c * CHUNK:(c + 1) * CHUNK] + m,
                        a_max[:, c * CHUNK:(c + 1) * CHUNK])
    m_prev = jnp.concatenate(m_prev, axis=1)
    m_new = jnp.maximum(b_last + m_prev, a_max)
    mx = jnp.maximum(m_prev, scan(a, jnp.maximum, NEG))
    out_ref[...] = jnp.concatenate(
        [-mx, jnp.exp(m_prev - mx), jnp.exp(-(b + mx)), jnp.exp(a_end - m_new), a,
         jnp.exp(b_last + m_prev - m_new),
         jnp.zeros((GATE_ROWS - 6 * M_HEADS, S), F32)], axis=0)


def _gates(ift, bif, B, S):
    return pl.pallas_call(
        _gates_kernel,
        grid=(B,),
        in_specs=[pl.BlockSpec((SUBLANES, S), lambda b: (0, b)),
                  pl.BlockSpec((SUBLANES, LANES), lambda b: (0, 0))],
        out_specs=pl.BlockSpec((None, GATE_ROWS, S), lambda b: (b, 0, 0)),
        out_shape=jax.ShapeDtypeStruct((B, GATE_ROWS, S), F32),
        compiler_params=pltpu.CompilerParams(dimension_semantics=("parallel",)),
        name="gates",
    )(ift, bif)


def _mlstm_kernel(q_ref, k_ref, v_ref, o_ref, g_ref, cw_ref, cb_ref, ng_ref, shift_ref,
                  eye_ref, avg_ref, y_ref, c_ref, n_ref, qt_ref, kt_ref):
    L = CHUNK
    nt_dims = (((1,), (1,)), ((), ()))

    @pl.when(pl.program_id(1) == 0)
    def _():
        c_ref[...] = jnp.zeros_like(c_ref)
        n_ref[...] = jnp.zeros_like(n_ref)
        qt_ref[...] = jnp.zeros_like(qt_ref)
        kt_ref[...] = jnp.zeros_like(kt_ref)

    cw = cw_ref[...]
    cb = cb_ref[...]
    tri = (lax.broadcasted_iota(jnp.int32, (L, L), 0)
           >= lax.broadcasted_iota(jnp.int32, (L, L), 1))

    units = []
    for s in range(q_ref.shape[0]):
        q_all = _conv_silu(q_ref.at[s], qt_ref.at[s], shift_ref, cw[:, :M_WIDTH], cb[:, :M_WIDTH])
        q_all = (q_all * (M_DH ** -0.5)).astype(BF16)
        k_all = _conv_silu(k_ref.at[s], kt_ref.at[s], shift_ref, cw[:, M_WIDTH:], cb[:, M_WIDTH:])
        rows = g_ref[s]
        cols = jnp.transpose(rows)
        for h in range(M_HEADS):
            sl = slice(h * M_DH, (h + 1) * M_DH)
            col = lambda i: cols[:, i * M_HEADS + h:i * M_HEADS + h + 1]
            units.append(dict(
                s=s, h=h, sl=sl, qb=q_all[:, sl], k=k_all[:, sl], vb=v_ref[s, :, sl],
                u_col=col(0), wi_col=col(1), en_col=col(2), wk_col=col(3),
                a_row=rows[4 * M_HEADS + h:4 * M_HEADS + h + 1],
                dec_row=rows[5 * M_HEADS + h:5 * M_HEADS + h + 1]))

    for u in units:
        c_old = c_ref[u["s"], u["h"]]
        n_old = n_ref[u["s"], u["h"]:u["h"] + 1]
        kb = u["k"].astype(BF16)
        kn = jnp.concatenate([kb, jnp.broadcast_to(n_old, (L, M_DH)).astype(BF16)], axis=0)
        u["s_aug"] = lax.dot_general(u["qb"], kn, nt_dims, preferred_element_type=F32)
        u["qc"] = jnp.dot(u["qb"], c_old.astype(BF16), preferred_element_type=F32)
        kw = u["k"] * u["wk_col"]
        u["kw_t"] = lax.dot_general(eye_ref[...], kw.astype(BF16), nt_dims,
                                    preferred_element_type=F32).astype(BF16)
        dec = jnp.concatenate([u["dec_row"], u["dec_row"]], axis=1)
        u["c_dec"] = dec * c_old
        n_ref[u["s"], u["h"]:u["h"] + 1] = dec * n_old + jnp.sum(kw, axis=0, keepdims=True)

    for u in units:
        dmat = jnp.where(tri, jnp.exp(u["u_col"] + u["a_row"]), 0.0)
        w_intra = dmat * u["s_aug"][:, :L]
        u["w_sum"] = jnp.sum(w_intra, axis=1, keepdims=True)
        u["w_intra"] = w_intra.astype(BF16)

    for u in units:
        u["pv"] = jnp.dot(u["w_intra"], u["vb"], preferred_element_type=F32)
        c_ref[u["s"], u["h"]] = u["c_dec"] + jnp.dot(u["kw_t"], u["vb"],
                                                     preferred_element_type=F32)

    for u in units:
        num = u["wi_col"] * u["qc"] + u["pv"]
        den = u["wi_col"] * u["s_aug"][:, L:L + 1] + u["w_sum"]
        hout = num / jnp.maximum(jnp.abs(den), u["en_col"])
        mu = jnp.dot(hout.astype(BF16), avg_ref[...], preferred_element_type=F32)
        u["cen"] = hout - jnp.concatenate([mu] * (M_DH // LANES), axis=1)
        var = jnp.dot((u["cen"] * u["cen"]).astype(BF16), avg_ref[...],
                      preferred_element_type=F32)
        u["var"] = jnp.concatenate([var] * (M_DH // LANES), axis=1)

    for u in units:
        hn = u["cen"] * lax.rsqrt(u["var"] + EPS) * ng_ref[:, u["sl"]]
        y_ref[u["s"], :, u["sl"]] = (o_ref[u["s"], :, u["sl"]].astype(F32) * hn).astype(BF16)


def _mlstm(p3, gates, conv_w, conv_b, ng, nseq=2):
    B, S, _ = p3.shape
    nseq = nseq if B % nseq == 0 else 1
    wblk = lambda col: pl.BlockSpec((nseq, CHUNK, M_WIDTH), lambda b, c: (b, c, col // M_WIDTH))
    full = lambda shape: pl.BlockSpec(shape, lambda b, c: (0,) * len(shape))
    return pl.pallas_call(
        _mlstm_kernel,
        grid=(B // nseq, S // CHUNK),
        in_specs=[
            wblk(P_QM), wblk(P_KM), wblk(P_VM), wblk(P_OM),
            pl.BlockSpec((nseq, GATE_ROWS, CHUNK), lambda b, c: (b, 0, c)),
            full((CONV_K, 2 * M_WIDTH)), full((1, 2 * M_WIDTH)), full((1, M_WIDTH)),
            full((CHUNK, CONV_K * CHUNK)), full((M_DH, M_DH)), full((M_DH, LANES)),
        ],
        out_specs=pl.BlockSpec((nseq, CHUNK, M_WIDTH), lambda b, c: (b, c, 0)),
        out_shape=jax.ShapeDtypeStruct((B, S, M_WIDTH), BF16),
        scratch_shapes=[
            pltpu.VMEM((nseq, M_HEADS, M_DH, M_DH), F32),
            pltpu.VMEM((nseq, SUBLANES, M_DH), F32),
            pltpu.VMEM((nseq, SUBLANES, M_WIDTH), F32),
            pltpu.VMEM((nseq, SUBLANES, M_WIDTH), F32),
        ],
        compiler_params=pltpu.CompilerParams(
            dimension_semantics=("parallel", "arbitrary"), vmem_limit_bytes=VMEM_LIMIT),
        name="mlstm",
    )(p3, p3, p3, p3, gates, conv_w, conv_b, ng,
      jnp.asarray(_conv_shift_matrix(CHUNK), BF16), jnp.eye(M_DH, dtype=BF16),
      jnp.full((M_DH, LANES), 1.0 / M_DH, BF16))


def _attn_kernel(q_ref, kc_ref, kp_ref, vc_ref, vp_ref, bias_ref, o_ref, lse_ref, *, tq):
    for r in range(q_ref.shape[0]):
        _attn_rows(q_ref.at[r], kc_ref.at[r], kp_ref.at[r], vc_ref.at[r], vp_ref.at[r], bias_ref,
                   o_ref.at[r], lse_ref.at[r], tq=tq)


def _attn_rows(q_ref, kc_ref, kp_ref, vc_ref, vp_ref, bias_ref, o_ref, lse_ref, *, tq):
    first = (pl.program_id(2) == 0).astype(jnp.int32)
    lane = lax.broadcasted_iota(jnp.int32, (1, LANES), 1)
    low = lane < A_DH
    nt_dims = (((1,), (1,)), ((), ()))
    n_pairs = A_HG // 2
    krow = lax.broadcasted_iota(jnp.int32, (4 * A_BLK, LANES), 0)
    ones_blk = jnp.where((krow < 2 * A_BLK) == low, 1.0, 0.0).astype(BF16)
    for i in range(tq // A_BLK):
        rows = slice(i * A_BLK, (i + 1) * A_BLK)
        if i == 0:
            k2 = jnp.concatenate([kp_ref[...], kc_ref[rows]], axis=0)
            v2 = jnp.concatenate([vp_ref[...], vc_ref[rows]], axis=0)
            var = first
        else:
            k2 = kc_ref[(i - 1) * A_BLK:(i + 1) * A_BLK]
            v2 = vc_ref[(i - 1) * A_BLK:(i + 1) * A_BLK]
            var = 0
        q = q_ref[rows]
        zero = jnp.zeros((A_BLK, LANES), BF16)

        scores = []
        for p in range(n_pairs):
            cs = slice(p * LANES, (p + 1) * LANES)
            qp = q[:, cs]
            q2 = jnp.concatenate([jnp.where(low, qp, zero), jnp.where(low, zero, qp)], axis=0)
            scores.append(lax.dot_general(q2, k2[:, cs], nt_dims, preferred_element_type=F32))

        probs, maxes = [], []
        for p in range(n_pairs):
            for e in range(2):
                s = scores[p][e * A_BLK:(e + 1) * A_BLK] + bias_ref[var, 2 * p + e]
                mx = jnp.max(s, axis=1, keepdims=True)
                probs.append(jnp.exp2(s - mx).astype(BF16))
                maxes.append(mx)

        lse_blk = jnp.zeros((A_BLK, LANES), F32)
        for p in range(n_pairs):
            cs = slice(p * LANES, (p + 1) * LANES)
            vp2 = v2[:, cs]
            zero2 = jnp.zeros_like(vp2)
            v_cat = jnp.concatenate([jnp.where(low, vp2, zero2), jnp.where(low, zero2, vp2)],
                                    axis=0)
            p_cat = jnp.concatenate([probs[2 * p], probs[2 * p + 1]], axis=1)
            o_ext = jnp.dot(p_cat, jnp.concatenate([v_cat, ones_blk], axis=1),
                            preferred_element_type=F32)
            l_pair = o_ext[:, LANES:]
            o_ref[rows, cs] = (o_ext[:, :LANES] * (1.0 / l_pair)).astype(BF16)
            mx_pair = jnp.where(low, maxes[2 * p], maxes[2 * p + 1])
            lse_pair = (mx_pair + jnp.log2(l_pair)) * (1.0 / LOG2E)
            lse_blk = jnp.where(lane % A_DH == p, lse_pair, lse_blk)

        packed = jnp.zeros((A_BLK, LANES), F32)
        rest = lse_blk
        for part in range(LSE_PARTS):
            hi = rest.astype(BF16).astype(F32)
            rest = rest - hi
            packed = packed + (hi if part == 0 else pltpu.roll(hi, part * SUBLANES, 1))
        lse_ref[rows] = packed.astype(BF16)


def _attn_group(src, col0, bias_g, g, rows=1024):
    B, d, sd, _ = src.shape
    tq = min(rows, sd)
    nres = min(rows // tq, d)
    cur = lambda c: pl.BlockSpec((None, nres, tq, A_GW), lambda b, r, n: (b, r, n, col0 + c))
    prev = lambda c: pl.BlockSpec(
        (None, nres, A_BLK, A_GW),
        lambda b, r, n: (b, r, jnp.maximum(n * (tq // A_BLK) - 1, 0), col0 + c))
    return pl.pallas_call(
        functools.partial(_attn_kernel, tq=tq),
        grid=(B, d // nres, sd // tq),
        in_specs=[
            cur(0), cur(1), prev(1), cur(2), prev(2),
            pl.BlockSpec((2, A_HG, A_BLK, 2 * A_BLK), lambda b, r, n: (0, 0, 0, 0)),
        ],
        out_specs=[
            pl.BlockSpec((None, nres, tq, A_GW), lambda b, r, n: (b, r, n, 0)),
            pl.BlockSpec((None, nres, tq, LANES), lambda b, r, n: (b, r, n, 0)),
        ],
        out_shape=[
            jax.ShapeDtypeStruct((B, d, sd, A_GW), BF16),
            jax.ShapeDtypeStruct((B, d, sd, LANES), BF16),
        ],
        compiler_params=pltpu.CompilerParams(
            dimension_semantics=("parallel", "parallel", "arbitrary"),
            vmem_limit_bytes=VMEM_LIMIT),
        name=f"attn_g{g}",
    )(src, src, src, src, src, bias_g)


def _merge_kernel(x_ref, gate_ref, ya_ref, *rest):
    o_refs = rest[:N_GROUPS]
    l_refs = rest[N_GROUPS:2 * N_GROUPS]
    unperm_refs = rest[2 * N_GROUPS:2 * N_GROUPS + N_DIL]
    e_ref, wa_ref, wb_ref, wo_ref, out_ref = rest[2 * N_GROUPS + N_DIL:]
    tm = x_ref.shape[0]

    outs, lses = [], []
    for g in range(N_GROUPS):
        o = o_refs[g][...].reshape(tm, A_GW)
        lp = l_refs[g][...].reshape(tm, LANES)
        if g == 0:
            o, lp = o.astype(F32), lp.astype(F32)
        else:
            both = jnp.dot(unperm_refs[g - 1][...], jnp.concatenate([o, lp], axis=1),
                           preferred_element_type=F32)
            o, lp = both[:, :A_GW], both[:, A_GW:]
        outs.append(o)
        lses.append(sum(lp if part == 0 else pltpu.roll(lp, LANES - part * SUBLANES, 1)
                        for part in range(LSE_PARTS)))

    lm = functools.reduce(jnp.maximum, lses)
    es = [jnp.exp(l - lm) for l in lses]
    den = functools.reduce(jnp.add, es)
    yb = jnp.zeros((tm, A_GW), F32)
    for e, o in zip(es, outs):
        wide = jnp.dot((e / den).astype(BF16), e_ref[...], preferred_element_type=F32)
        yb = yb + wide * o
    pa = jnp.dot(ya_ref[...], wa_ref[...], preferred_element_type=F32)
    pb = jnp.dot(yb.astype(BF16), wb_ref[...], preferred_element_type=F32)
    ga = gate_ref[:, :D_MODEL].astype(F32)
    gb = gate_ref[:, D_MODEL:].astype(F32)
    merged = (ga * pa + gb * pb).astype(BF16)
    out_ref[...] = x_ref[...] + jnp.dot(merged, wo_ref[...], preferred_element_type=F32)


def _merge(x2, p_flat, ya, os_, ls_, wa, wb, wo, S, tm=512):
    n = x2.shape[0]
    nt = S // tm
    row = lambda w: pl.BlockSpec((tm, w), lambda i: (i, 0))
    full = lambda a: pl.BlockSpec(a.shape, lambda i: (0, 0))
    grp = lambda a: pl.BlockSpec((None, a.shape[1], tm // a.shape[1], a.shape[3]),
                                 lambda i: (i // nt, 0, i % nt, 0))
    unperms = [jnp.asarray(_deinterleave_matrix(tm, d).T, BF16) for _, d in GROUPS[1:]]
    head_lane = (np.arange(A_HG) % 2) * A_DH + np.arange(A_HG) // 2
    expand = jnp.asarray(
        np.arange(LANES)[:, None] == head_lane[np.arange(A_GW) // A_DH][None, :], BF16)
    consts = unperms + [expand, wa, wb, wo]
    return pl.pallas_call(
        _merge_kernel,
        grid=(n // tm,),
        in_specs=[row(D_MODEL),
                  pl.BlockSpec((tm, 2 * D_MODEL), lambda i: (i, P_GATE // (2 * D_MODEL))),
                  row(M_WIDTH)]
                 + [grp(a) for a in os_] + [grp(a) for a in ls_] + [full(a) for a in consts],
        out_specs=row(D_MODEL),
        out_shape=jax.ShapeDtypeStruct((n, D_MODEL), F32),
        compiler_params=pltpu.CompilerParams(
            dimension_semantics=("parallel",), vmem_limit_bytes=VMEM_LIMIT),
        name="merge",
    )(x2, p_flat, ya, *os_, *ls_, *consts)


FF_CHUNKS = ((0, 1024), (1024, 1024), (2048, 768))
FF_RB = 256


def _rms(x, g):
    return x * lax.rsqrt(jnp.mean(x * x, axis=-1, keepdims=True) + EPS) * g


def _ffn_kernel(x_ref, gf_ref, wg_ref, wu_ref, wd_ref, gl_ref, out_ref):
    for rc in range(x_ref.shape[0] // FF_RB):
        rows = slice(rc * FF_RB, (rc + 1) * FF_RB)
        x = x_ref[rows]
        hf = _rms(x, gf_ref[...]).astype(BF16)
        acc = x
        for start, size in FF_CHUNKS:
            cs = slice(start, start + size)
            gt = jnp.dot(hf, wg_ref[:, cs], preferred_element_type=F32)
            up = jnp.dot(hf, wu_ref[:, cs], preferred_element_type=F32)
            act = (gt * jax.nn.sigmoid(gt) * up).astype(BF16)
            acc = acc + jnp.dot(act, wd_ref[cs, :], preferred_element_type=F32)
        out_ref[rows] = _rms(acc, gl_ref[...])


def _ffn(x1, gf, wg, wu, wd, gl, tm=512):
    n = x1.shape[0]
    row = pl.BlockSpec((tm, D_MODEL), lambda i: (i, 0))
    once = lambda a: pl.BlockSpec(a.shape, lambda i: (0, 0), pipeline_mode=pl.Buffered(1))
    return pl.pallas_call(
        _ffn_kernel,
        grid=(n // tm,),
        in_specs=[row, once(gf), once(wg), once(wu), once(wd), once(gl)],
        out_specs=row,
        out_shape=jax.ShapeDtypeStruct((n, D_MODEL), F32),
        compiler_params=pltpu.CompilerParams(
            dimension_semantics=("parallel",), vmem_limit_bytes=VMEM_LIMIT),
        name="ffn",
    )(x1, gf, wg, wu, wd, gl)


def _layer(x2, B, S, bias, norm_mix_g, w_in, b_gate_if, conv_w, conv_b, mlstm_norm_g,
           w_proj_a, w_proj_b, w_out, norm_ffn_g, w_gate, w_up, w_down, final_g):
    sizes = (M_WIDTH, M_WIDTH, M_WIDTH, 2 * M_HEADS, M_WIDTH, A_WIDTH, A_WIDTH, A_WIDTH,
             2 * D_MODEL)
    offs = np.concatenate([[0], np.cumsum(sizes)])
    piece = lambda i: w_in[:, offs[i]:offs[i + 1]]
    qkv = lambda g: [piece(i)[:, g * A_GW:(g + 1) * A_GW] * sc
                     for i, sc in ((5, QK_SCALE), (6, 1.0), (7, 1.0))]
    w_p = jnp.concatenate([piece(0), piece(1), piece(2), piece(4), piece(8)]
                          + [w for g in range(N_GROUPS) for w in qkv(g)], axis=1).astype(BF16).T
    w_if = jnp.concatenate(_split_bf16(piece(3).T, 2), axis=0)

    p_flat, if_t, *dilated = _inproj(x2, norm_mix_g[None], w_p, w_if, B, S)

    bif = jnp.broadcast_to(b_gate_if[:, None], (SUBLANES, LANES))
    gates = _gates(if_t, bif, B, S)
    ya = _mlstm(p_flat.reshape(B, S, P_WIDTH), gates, conv_w, conv_b[None],
                mlstm_norm_g[None]).reshape(B * S, M_WIDTH)

    os_, ls_ = [], []
    for g in range(N_GROUPS):
        if g == 0:
            o, lse = _attn_group(p_flat.reshape(B, 1, S, P_WIDTH), P_A0 // A_GW, bias[g], g)
        else:
            o, lse = _attn_group(dilated[g - 1], 0, bias[g], g)
        os_.append(o)
        ls_.append(lse)

    x1 = _merge(x2, p_flat, ya, os_, ls_, w_proj_a.astype(BF16), w_proj_b.astype(BF16),
                w_out.astype(BF16), S)
    return _ffn(x1, norm_ffn_g[None], w_gate.astype(BF16), w_up.astype(BF16),
                w_down.astype(BF16), final_g[None])


def kernel(x, norm_mix_g, w_in, b_gate_if, conv_w, conv_b, mlstm_norm_g, w_proj_a, w_proj_b,
           w_out, norm_ffn_g, w_gate, w_up, w_down, rel_bias, norm_final_g):
    B, S, _ = x.shape
    depth = w_in.shape[0]
    assert depth == 1, "the final norm is fused into the (single) layer's channel mixer"
    bias = _bias_tables(rel_bias)
    out = _layer(x.reshape(B * S, D_MODEL), B, S, bias, norm_mix_g[0], w_in[0], b_gate_if[0],
                 conv_w[0], conv_b[0], mlstm_norm_g[0], w_proj_a[0], w_proj_b[0], w_out[0],
                 norm_ffn_g[0], w_gate[0], w_up[0], w_down[0], norm_final_g)
    return out.reshape(B, S, D_MODEL)
```

```python
import functools
import math

import numpy as np
import jax
import jax.numpy as jnp
from jax import lax
from jax.experimental import pallas as pl
from jax.experimental.pallas import tpu as pltpu

F32 = jnp.float32
BF16 = jnp.bfloat16

D_MODEL = 1024
M_HEADS = 4
M_DH = 256
M_WIDTH = M_HEADS * M_DH
CONV_K = 4
CHUNK = 128
GROUPS = ((128, 1), (512, 4), (2048, 16))
N_GROUPS = len(GROUPS)
A_HG = 8
A_DH = 64
A_GW = A_HG * A_DH
A_WIDTH = N_GROUPS * A_GW
A_BLK = 128
N_BUCKETS = 32
MAX_DISTANCE = 2048
D_FF = 2816
EPS = 1e-6
NEG = -1e30
LOG2E = math.log2(math.e)
QK_SCALE = A_DH ** -0.5 * LOG2E
LSE_PARTS = 3

LANES = 128
SUBLANES = 8

P_QM = 0
P_KM = 1024
P_VM = 2048
P_OM = 3072
P_GATE = 4096
P_A0 = 6144
P_WIDTH = P_A0 + 3 * A_GW
IN_TN = 3 * A_GW
N_DIL = N_GROUPS - 1

VMEM_LIMIT = 56 * 1024 * 1024


def _split_bf16(x, parts):
    out = []
    for _ in range(parts):
        hi = x.astype(BF16)
        out.append(hi)
        x = x - hi.astype(F32)
    return out


def _bucket_tables():
    i = np.arange(A_BLK)[:, None]
    j = np.arange(2 * A_BLK)[None, :]
    dist = i + A_BLK - j
    buckets = []
    for window, dil in GROUPS:
        n = np.maximum(dist, 0) * dil
        nf = np.maximum(n, 1).astype(np.float32)
        max_exact = N_BUCKETS // 2
        large = max_exact + (np.log(nf / max_exact) / math.log(MAX_DISTANCE / max_exact)
                             * (N_BUCKETS - max_exact)).astype(np.int32)
        large = np.minimum(large, N_BUCKETS - 1)
        buckets.append(np.where(n < max_exact, n, large).astype(np.int32))
    span = GROUPS[0][0] // GROUPS[0][1]
    assert all(w // d == span for w, d in GROUPS)
    valid = ((dist >= 0) & (dist <= span)).astype(np.int32)
    valid_first = (valid.astype(bool) & (j >= A_BLK)).astype(np.int32)
    return np.stack(buckets), np.stack([valid, valid_first])


def _bias_kernel(tab_ref, bucket_ref, valid_ref, out_ref):
    g = pl.program_id(0)
    bucket = bucket_ref[...]
    for h in range(A_HG):
        acc = jnp.zeros(bucket.shape, F32)
        for b in range(N_BUCKETS):
            acc = jnp.where(bucket == b, tab_ref[b, g * A_HG + h], acc)
        acc = acc * LOG2E
        out_ref[0, h] = jnp.where(valid_ref[0] > 0, acc, NEG)
        out_ref[1, h] = jnp.where(valid_ref[1] > 0, acc, NEG)


def _bias_tables(rel_bias):
    buckets, valid = _bucket_tables()
    return pl.pallas_call(
        _bias_kernel,
        grid=(N_GROUPS,),
        in_specs=[
            pl.BlockSpec(memory_space=pltpu.SMEM),
            pl.BlockSpec((None, A_BLK, 2 * A_BLK), lambda g: (g, 0, 0)),
            pl.BlockSpec((2, A_BLK, 2 * A_BLK), lambda g: (0, 0, 0)),
        ],
        out_specs=pl.BlockSpec((None, 2, A_HG, A_BLK, 2 * A_BLK), lambda g: (g, 0, 0, 0, 0)),
        out_shape=jax.ShapeDtypeStruct((N_GROUPS, 2, A_HG, A_BLK, 2 * A_BLK), F32),
        name="bias_tables",
    )(rel_bias, jnp.asarray(buckets), jnp.asarray(valid))


def _deinterleave_matrix(rows, d):
    out = np.arange(rows)
    src = d * (out % (rows // d)) + out // (rows // d)
    return (src[:, None] == np.arange(rows)[None, :]).astype(np.float32)


PERM_BLK = 256
NORM_RB = 256


def _inproj_kernel(x_ref, g_ref, w_ref, wif_ref, *rest, n_nat, dils):
    perm_refs = rest[:N_DIL]
    p_ref, if_ref = rest[N_DIL:N_DIL + 2]
    a_refs = rest[N_DIL + 2:2 * N_DIL + 2]
    h_ref, hp_ref = rest[-2:]
    tm = x_ref.shape[0]
    j = pl.program_id(1)
    nt_dims = (((1,), (1,)), ((), ()))

    @pl.when(j == 0)
    def _():
        for rc in range(tm // NORM_RB):
            rows = slice(rc * NORM_RB, (rc + 1) * NORM_RB)
            x = x_ref[rows]
            r = lax.rsqrt(jnp.mean(x * x, axis=-1, keepdims=True) + EPS)
            h = x * r * g_ref[...]
            h_hi, h_lo = _split_bf16(h, 2)
            h_ref[rows] = h_hi
            acc = (lax.dot_general(wif_ref[...], h_hi, nt_dims, preferred_element_type=F32)
                   + lax.dot_general(wif_ref[...], h_lo, nt_dims, preferred_element_type=F32))
            if_ref[:, rows] = acc[:SUBLANES] + acc[SUBLANES:]
            acc = lax.dot_general(h_hi, w_ref[...], nt_dims, preferred_element_type=F32)
            p_ref[rows] = acc.astype(BF16)

    is_gate = (j >= P_OM // IN_TN) & (j < P_A0 // IN_TN)

    @pl.when((j > 0) & (j < n_nat) & jnp.logical_not(is_gate))
    def _():
        acc = lax.dot_general(h_ref[...], w_ref[...], nt_dims, preferred_element_type=F32)
        p_ref[...] = acc.astype(BF16)

    @pl.when(is_gate)
    def _():
        acc = lax.dot_general(h_ref[...], w_ref[...], nt_dims, preferred_element_type=F32)
        p_ref[...] = (0.5 * jnp.tanh(0.5 * acc) + 0.5).astype(BF16)

    for idx, d in enumerate(dils):
        @pl.when(j == n_nat + idx)
        def _(perm_ref=perm_refs[idx], a_ref=a_refs[idx], d=d):
            piece = PERM_BLK // d
            for c in range(tm // PERM_BLK):
                hp = jnp.dot(perm_ref[...], h_ref[c * PERM_BLK:(c + 1) * PERM_BLK],
                             preferred_element_type=F32).astype(BF16)
                for r in range(d):
                    dst = r * (tm // d) + c * piece
                    hp_ref[dst:dst + piece] = hp[r * piece:(r + 1) * piece]
            a = lax.dot_general(hp_ref[...], w_ref[...], nt_dims, preferred_element_type=F32)
            a_ref[...] = a.astype(BF16).reshape(a_ref.shape)


def _inproj(x2, g, w_p, w_if, B, S, tm=1024):
    n = x2.shape[0]
    nt = S // tm
    n_nat = P_WIDTH // IN_TN
    dils = tuple(d for _, d in GROUPS[1:])
    perms = [jnp.asarray(_deinterleave_matrix(PERM_BLK, d), BF16) for d in dils]
    return pl.pallas_call(
        functools.partial(_inproj_kernel, n_nat=n_nat, dils=dils),
        grid=(n // tm, n_nat + N_DIL),
        in_specs=[
            pl.BlockSpec((tm, D_MODEL), lambda i, j: (i, 0)),
            pl.BlockSpec((1, D_MODEL), lambda i, j: (0, 0)),
            pl.BlockSpec((IN_TN, D_MODEL), lambda i, j: (j, 0)),
            pl.BlockSpec((2 * SUBLANES, D_MODEL), lambda i, j: (0, 0)),
        ] + [pl.BlockSpec((PERM_BLK, PERM_BLK), lambda i, j: (0, 0)) for _ in dils],
        out_specs=[
            pl.BlockSpec((tm, IN_TN), lambda i, j: (i, jnp.minimum(j, n_nat - 1))),
            pl.BlockSpec((SUBLANES, tm), lambda i, j: (0, i)),
        ] + [pl.BlockSpec((None, d, tm // d, IN_TN), lambda i, j: (i // nt, 0, i % nt, 0))
             for d in dils],
        out_shape=[
            jax.ShapeDtypeStruct((n, P_WIDTH), BF16),
            jax.ShapeDtypeStruct((SUBLANES, n), F32),
        ] + [jax.ShapeDtypeStruct((B, d, S // d, IN_TN), BF16) for d in dils],
        scratch_shapes=[pltpu.VMEM((tm, D_MODEL), BF16), pltpu.VMEM((tm, D_MODEL), BF16)],
        compiler_params=pltpu.CompilerParams(
            dimension_semantics=("parallel", "arbitrary"), vmem_limit_bytes=VMEM_LIMIT),
        name="inproj",
    )(x2, g, w_p, w_if, *perms)


def _conv_shift_matrix(L):
    return np.concatenate([np.eye(L, k=-(CONV_K - 1 - j)) for j in range(CONV_K)], axis=1)


def _conv_silu(x_ref, tail_ref, shift_ref, cw, cb):
    L, C = x_ref.shape
    pack = 2 * SUBLANES
    x3 = x_ref[...].reshape(L // pack, pack, C)
    prods = []
    for j in range(CONV_K):
        wj = jnp.broadcast_to(cw[j:j + 1], (pack, C)).astype(BF16)
        prods.append((x3 * wj[None]).reshape(L, C))
    y = jnp.dot(shift_ref[...], jnp.concatenate(prods, axis=0),
                preferred_element_type=F32) + cb
    tail = tail_ref[...]
    row = lax.broadcasted_iota(jnp.int32, tail.shape, 0)
    fix = jnp.zeros(tail.shape, F32)
    for k in range(1, CONV_K):
        tap = pltpu.roll(tail, k, 0) * cw[CONV_K - 1 - k:CONV_K - k]
        fix = fix + jnp.where(row < k, tap, 0.0)
    y = jnp.concatenate([y[:SUBLANES] + fix, y[SUBLANES:]], axis=0)
    tail_ref[...] = x_ref[L - pack:].astype(F32)[pack - SUBLANES:]
    return y * jax.nn.sigmoid(y)


GATE_ROWS = 32


def _gates_kernel(ift_ref, bif_ref, out_ref):
    nseq, _, S = out_ref.shape
    nc = S // CHUNK
    g = ift_ref[...] + jnp.concatenate([bif_ref[...]] * (nseq * nc), axis=1)
    ig = g[:M_HEADS]
    lf = jax.nn.log_sigmoid(g[M_HEADS:])
    pos = lax.broadcasted_iota(jnp.int32, lf.shape, 1) % CHUNK

    def scan(x, op, fill):
        sh = 1
        while sh < CHUNK:
            x = op(x, jnp.where(pos >= sh, pltpu.roll(x, sh, 1), fill))
            sh *= 2
        return x

    def last(x):
        x = jnp.where(pos == CHUNK - 1, x, NEG)
        sh = 1
        while sh < CHUNK:
            x = jnp.maximum(x, jnp.where(pos < CHUNK - sh, pltpu.roll(x, nseq * S - sh, 1), NEG))
            sh *= 2
        return x

    b = scan(lf, jnp.add, 0.0)
    a = ig - b
    b_last = last(b)
    a_end = b_last + a
    a_max = last(scan(a_end, jnp.maximum, NEG))
    m_prev = []
    for c in range(nseq * nc):
        if c % nc == 0:
            m = jnp.zeros((M_HEADS, CHUNK), F32)
        m_prev.append(m)
        m = jnp.maximum(b_last[:, c * CHUNK:(c + 1) * CHUNK] + m,
                        a_max[:, c * CHUNK:(c + 1) * CHUNK])
    m_prev = jnp.concatenate(m_prev, axis=1)
    m_new = jnp.maximum(b_last + m_prev, a_max)
    mx = jnp.maximum(m_prev, scan(a, jnp.maximum, NEG))
    rows = jnp.concatenate(
        [-mx, jnp.exp(m_prev - mx), jnp.exp(-(b + mx)), jnp.exp(a_end - m_new), a,
         jnp.exp(b_last + m_prev - m_new),
         jnp.zeros((GATE_ROWS - 6 * M_HEADS, nseq * S), F32)], axis=0)
    for s in range(nseq):
        out_ref[s] = rows[:, s * S:(s + 1) * S]


def _gates(ift, bif, B, S, nseq=4):
    nseq = nseq if B % nseq == 0 else 1
    return pl.pallas_call(
        _gates_kernel,
        grid=(B // nseq,),
        in_specs=[pl.BlockSpec((SUBLANES, nseq * S), lambda b: (0, b)),
                  pl.BlockSpec((SUBLANES, LANES), lambda b: (0, 0))],
        out_specs=pl.BlockSpec((nseq, GATE_ROWS, S), lambda b: (b, 0, 0)),
        out_shape=jax.ShapeDtypeStruct((B, GATE_ROWS, S), F32),
        compiler_params=pltpu.CompilerParams(dimension_semantics=("parallel",)),
        name="gates",
    )(ift, bif)


def _mlstm_kernel(q_ref, k_ref, v_ref, o_ref, g_ref, cw_ref, cb_ref, ng_ref, shift_ref,
                  eye_ref, avg_ref, y_ref, c_ref, n_ref, qt_ref, kt_ref):
    L = CHUNK
    nt_dims = (((1,), (1,)), ((), ()))

    @pl.when(pl.program_id(1) == 0)
    def _():
        c_ref[...] = jnp.zeros_like(c_ref)
        n_ref[...] = jnp.zeros_like(n_ref)
        qt_ref[...] = jnp.zeros_like(qt_ref)
        kt_ref[...] = jnp.zeros_like(kt_ref)

    cw = cw_ref[...]
    cb = cb_ref[...]
    tri = (lax.broadcasted_iota(jnp.int32, (L, L), 0)
           >= lax.broadcasted_iota(jnp.int32, (L, L), 1))

    units = []
    for s in range(q_ref.shape[0]):
        q_all = _conv_silu(q_ref.at[s], qt_ref.at[s], shift_ref, cw[:, :M_WIDTH], cb[:, :M_WIDTH])
        q_all = (q_all * (M_DH ** -0.5)).astype(BF16)
        k_all = _conv_silu(k_ref.at[s], kt_ref.at[s], shift_ref, cw[:, M_WIDTH:], cb[:, M_WIDTH:])
        rows = g_ref[s]
        cols = jnp.transpose(rows)
        for h in range(M_HEADS):
            sl = slice(h * M_DH, (h + 1) * M_DH)
            col = lambda i: cols[:, i * M_HEADS + h:i * M_HEADS + h + 1]
            units.append(dict(
                s=s, h=h, sl=sl, qb=q_all[:, sl], k=k_all[:, sl], vb=v_ref[s, :, sl],
                u_col=col(0), wi_col=col(1), en_col=col(2), wk_col=col(3),
                a_row=rows[4 * M_HEADS + h:4 * M_HEADS + h + 1],
                dec_row=rows[5 * M_HEADS + h:5 * M_HEADS + h + 1]))

    for u in units:
        c_old = c_ref[u["s"], u["h"]]
        n_old = n_ref[u["s"], u["h"]:u["h"] + 1]
        kb = u["k"].astype(BF16)
        kn = jnp.concatenate([kb, jnp.broadcast_to(n_old, (L, M_DH)).astype(BF16)], axis=0)
        u["s_aug"] = lax.dot_general(u["qb"], kn, nt_dims, preferred_element_type=F32)
        u["qc"] = jnp.dot(u["qb"], c_old.astype(BF16), preferred_element_type=F32)
        kw = u["k"] * u["wk_col"]
        u["kw_t"] = lax.dot_general(eye_ref[...], kw.astype(BF16), nt_dims,
                                    preferred_element_type=F32).astype(BF16)
        dec = jnp.concatenate([u["dec_row"], u["dec_row"]], axis=1)
        u["c_dec"] = dec * c_old
        n_ref[u["s"], u["h"]:u["h"] + 1] = dec * n_old + jnp.sum(kw, axis=0, keepdims=True)

    for u in units:
        dmat = jnp.where(tri, jnp.exp(u["u_col"] + u["a_row"]), 0.0)
        w_intra = dmat * u["s_aug"][:, :L]
        u["w_sum"] = jnp.sum(w_intra, axis=1, keepdims=True)
        u["w_intra"] = w_intra.astype(BF16)

    for u in units:
        u["pv"] = jnp.dot(u["w_intra"], u["vb"], preferred_element_type=F32)
        c_ref[u["s"], u["h"]] = u["c_dec"] + jnp.dot(u["kw_t"], u["vb"],
                                                     preferred_element_type=F32)

    for u in units:
        num = u["wi_col"] * u["qc"] + u["pv"]
        den = u["wi_col"] * u["s_aug"][:, L:L + 1] + u["w_sum"]
        hout = num / jnp.maximum(jnp.abs(den), u["en_col"])
        mu = jnp.dot(hout.astype(BF16), avg_ref[...], preferred_element_type=F32)
        u["cen"] = hout - jnp.concatenate([mu] * (M_DH // LANES), axis=1)
        var = jnp.dot((u["cen"] * u["cen"]).astype(BF16), avg_ref[...],
                      preferred_element_type=F32)
        u["var"] = jnp.concatenate([var] * (M_DH // LANES), axis=1)

    for u in units:
        hn = u["cen"] * lax.rsqrt(u["var"] + EPS) * ng_ref[:, u["sl"]]
        y_ref[u["s"], :, u["sl"]] = (o_ref[u["s"], :, u["sl"]].astype(F32) * hn).astype(BF16)


def _mlstm(p3, gates, conv_w, conv_b, ng, nseq=4):
    B, S, _ = p3.shape
    nseq = nseq if B % nseq == 0 else 1
    wblk = lambda col: pl.BlockSpec((nseq, CHUNK, M_WIDTH), lambda b, c: (b, c, col // M_WIDTH))
    full = lambda shape: pl.BlockSpec(shape, lambda b, c: (0,) * len(shape))
    return pl.pallas_call(
        _mlstm_kernel,
        grid=(B // nseq, S // CHUNK),
        in_specs=[
            wblk(P_QM), wblk(P_KM), wblk(P_VM), wblk(P_OM),
            pl.BlockSpec((nseq, GATE_ROWS, CHUNK), lambda b, c: (b, 0, c)),
            full((CONV_K, 2 * M_WIDTH)), full((1, 2 * M_WIDTH)), full((1, M_WIDTH)),
            full((CHUNK, CONV_K * CHUNK)), full((M_DH, M_DH)), full((M_DH, LANES)),
        ],
        out_specs=pl.BlockSpec((nseq, CHUNK, M_WIDTH), lambda b, c: (b, c, 0)),
        out_shape=jax.ShapeDtypeStruct((B, S, M_WIDTH), BF16),
        scratch_shapes=[
            pltpu.VMEM((nseq, M_HEADS, M_DH, M_DH), F32),
            pltpu.VMEM((nseq, SUBLANES, M_DH), F32),
            pltpu.VMEM((nseq, SUBLANES, M_WIDTH), F32),
            pltpu.VMEM((nseq, SUBLANES, M_WIDTH), F32),
        ],
        compiler_params=pltpu.CompilerParams(
            dimension_semantics=("parallel", "arbitrary"), vmem_limit_bytes=VMEM_LIMIT),
        name="mlstm",
    )(p3, p3, p3, p3, gates, conv_w, conv_b, ng,
      jnp.asarray(_conv_shift_matrix(CHUNK), BF16), jnp.eye(M_DH, dtype=BF16),
      jnp.full((M_DH, LANES), 1.0 / M_DH, BF16))


def _attn_kernel(q_ref, kc_ref, kp_ref, vc_ref, vp_ref, bias_ref, o_ref, lse_ref, *, tq):
    for r in range(q_ref.shape[0]):
        _attn_rows(q_ref.at[r], kc_ref.at[r], kp_ref.at[r], vc_ref.at[r], vp_ref.at[r], bias_ref,
                   o_ref.at[r], lse_ref.at[r], tq=tq)


def _attn_rows(q_ref, kc_ref, kp_ref, vc_ref, vp_ref, bias_ref, o_ref, lse_ref, *, tq):
    first = (pl.program_id(2) == 0).astype(jnp.int32)
    lane = lax.broadcasted_iota(jnp.int32, (1, LANES), 1)
    low = lane < A_DH
    nt_dims = (((1,), (1,)), ((), ()))
    n_pairs = A_HG // 2
    krow = lax.broadcasted_iota(jnp.int32, (4 * A_BLK, LANES), 0)
    ones_blk = jnp.where((krow < 2 * A_BLK) == low, 1.0, 0.0).astype(BF16)
    for i in range(tq // A_BLK):
        rows = slice(i * A_BLK, (i + 1) * A_BLK)
        if i == 0:
            k2 = jnp.concatenate([kp_ref[...], kc_ref[rows]], axis=0)
            v2 = jnp.concatenate([vp_ref[...], vc_ref[rows]], axis=0)
            var = first
        else:
            k2 = kc_ref[(i - 1) * A_BLK:(i + 1) * A_BLK]
            v2 = vc_ref[(i - 1) * A_BLK:(i + 1) * A_BLK]
            var = 0
        q = q_ref[rows]
        zero = jnp.zeros((A_BLK, LANES), BF16)

        scores = []
        for p in range(n_pairs):
            cs = slice(p * LANES, (p + 1) * LANES)
            qp = q[:, cs]
            q2 = jnp.concatenate([jnp.where(low, qp, zero), jnp.where(low, zero, qp)], axis=0)
            scores.append(lax.dot_general(q2, k2[:, cs], nt_dims, preferred_element_type=F32))

        probs, maxes = [], []
        for p in range(n_pairs):
            for e in range(2):
                s = scores[p][e * A_BLK:(e + 1) * A_BLK] + bias_ref[var, 2 * p + e]
                mx = jnp.max(s, axis=1, keepdims=True)
                probs.append(jnp.exp2(s - mx).astype(BF16))
                maxes.append(mx)

        lse_blk = jnp.zeros((A_BLK, LANES), F32)
        for p in range(n_pairs):
            cs = slice(p * LANES, (p + 1) * LANES)
            vp2 = v2[:, cs]
            zero2 = jnp.zeros_like(vp2)
            v_cat = jnp.concatenate([jnp.where(low, vp2, zero2), jnp.where(low, zero2, vp2)],
                                    axis=0)
            p_cat = jnp.concatenate([probs[2 * p], probs[2 * p + 1]], axis=1)
            o_ext = jnp.dot(p_cat, jnp.concatenate([v_cat, ones_blk], axis=1),
                            preferred_element_type=F32)
            l_pair = o_ext[:, LANES:]
            o_ref[rows, cs] = (o_ext[:, :LANES] * (1.0 / l_pair)).astype(BF16)
            mx_pair = jnp.where(low, maxes[2 * p], maxes[2 * p + 1])
            lse_pair = (mx_pair + jnp.log2(l_pair)) * (1.0 / LOG2E)
            lse_blk = jnp.where(lane % A_DH == p, lse_pair, lse_blk)

        packed = jnp.zeros((A_BLK, LANES), F32)
        rest = lse_blk
        for part in range(LSE_PARTS):
            hi = rest.astype(BF16).astype(F32)
            rest = rest - hi
            packed = packed + (hi if part == 0 else pltpu.roll(hi, part * SUBLANES, 1))
        lse_ref[rows] = packed.astype(BF16)


def _attn_group(src, col0, bias_g, g, rows=1024):
    B, d, sd, _ = src.shape
    tq = min(rows, sd)
    nres = min(rows // tq, d)
    cur = lambda c: pl.BlockSpec((None, nres, tq, A_GW), lambda b, r, n: (b, r, n, col0 + c))
    prev = lambda c: pl.BlockSpec(
        (None, nres, A_BLK, A_GW),
        lambda b, r, n: (b, r, jnp.maximum(n * (tq // A_BLK) - 1, 0), col0 + c))
    return pl.pallas_call(
        functools.partial(_attn_kernel, tq=tq),
        grid=(B, d // nres, sd // tq),
        in_specs=[
            cur(0), cur(1), prev(1), cur(2), prev(2),
            pl.BlockSpec((2, A_HG, A_BLK, 2 * A_BLK), lambda b, r, n: (0, 0, 0, 0)),
        ],
        out_specs=[
            pl.BlockSpec((None, nres, tq, A_GW), lambda b, r, n: (b, r, n, 0)),
            pl.BlockSpec((None, nres, tq, LANES), lambda b, r, n: (b, r, n, 0)),
        ],
        out_shape=[
            jax.ShapeDtypeStruct((B, d, sd, A_GW), BF16),
            jax.ShapeDtypeStruct((B, d, sd, LANES), BF16),
        ],
        compiler_params=pltpu.CompilerParams(
            dimension_semantics=("parallel", "parallel", "arbitrary"),
            vmem_limit_bytes=VMEM_LIMIT),
        name=f"attn_g{g}",
    )(src, src, src, src, src, bias_g)


MERGE_RB = 512


def _merge_kernel(x_ref, gate_ref, ya_ref, *rest):
    o_refs = rest[:N_GROUPS]
    l_refs = rest[N_GROUPS:2 * N_GROUPS]
    unperm_refs = rest[2 * N_GROUPS:2 * N_GROUPS + N_DIL]
    e_ref, wa_ref, wb_ref, wo_ref, out_ref = rest[2 * N_GROUPS + N_DIL:]
    rb = MERGE_RB

    for c in range(x_ref.shape[0] // rb):
        rows = slice(c * rb, (c + 1) * rb)
        outs, lses = [], []
        for g in range(N_GROUPS):
            per = rb // o_refs[g].shape[0]
            o = o_refs[g][:, c * per:(c + 1) * per].reshape(rb, A_GW)
            lp = l_refs[g][:, c * per:(c + 1) * per].reshape(rb, LANES)
            if g == 0:
                o, lp = o.astype(F32), lp.astype(F32)
            else:
                both = jnp.dot(unperm_refs[g - 1][...], jnp.concatenate([o, lp], axis=1),
                               preferred_element_type=F32)
                o, lp = both[:, :A_GW], both[:, A_GW:]
            outs.append(o)
            lses.append(sum(lp if part == 0 else pltpu.roll(lp, LANES - part * SUBLANES, 1)
                            for part in range(LSE_PARTS)))

        lm = functools.reduce(jnp.maximum, lses)
        es = [jnp.exp(l - lm) for l in lses]
        den = functools.reduce(jnp.add, es)
        yb = jnp.zeros((rb, A_GW), F32)
        for e, o in zip(es, outs):
            wide = jnp.dot((e / den).astype(BF16), e_ref[...], preferred_element_type=F32)
            yb = yb + wide * o
        pa = jnp.dot(ya_ref[rows], wa_ref[...], preferred_element_type=F32)
        pb = jnp.dot(yb.astype(BF16), wb_ref[...], preferred_element_type=F32)
        ga = gate_ref[rows, :D_MODEL].astype(F32)
        gb = gate_ref[rows, D_MODEL:].astype(F32)
        merged = (ga * pa + gb * pb).astype(BF16)
        out_ref[rows] = x_ref[rows] + jnp.dot(merged, wo_ref[...], preferred_element_type=F32)


def _merge(x2, p_flat, ya, os_, ls_, wa, wb, wo, S, tm=1024):
    n = x2.shape[0]
    nt = S // tm
    row = lambda w: pl.BlockSpec((tm, w), lambda i: (i, 0))
    full = lambda a: pl.BlockSpec(a.shape, lambda i: (0, 0))
    grp = lambda a: pl.BlockSpec((None, a.shape[1], tm // a.shape[1], a.shape[3]),
                                 lambda i: (i // nt, 0, i % nt, 0))
    unperms = [jnp.asarray(_deinterleave_matrix(MERGE_RB, d).T, BF16) for _, d in GROUPS[1:]]
    head_lane = (np.arange(A_HG) % 2) * A_DH + np.arange(A_HG) // 2
    expand = jnp.asarray(
        np.arange(LANES)[:, None] == head_lane[np.arange(A_GW) // A_DH][None, :], BF16)
    consts = unperms + [expand, wa, wb, wo]
    return pl.pallas_call(
        _merge_kernel,
        grid=(n // tm,),
        in_specs=[row(D_MODEL),
                  pl.BlockSpec((tm, 2 * D_MODEL), lambda i: (i, P_GATE // (2 * D_MODEL))),
                  row(M_WIDTH)]
                 + [grp(a) for a in os_] + [grp(a) for a in ls_] + [full(a) for a in consts],
        out_specs=row(D_MODEL),
        out_shape=jax.ShapeDtypeStruct((n, D_MODEL), F32),
        compiler_params=pltpu.CompilerParams(
            dimension_semantics=("parallel",), vmem_limit_bytes=VMEM_LIMIT),
        name="merge",
    )(x2, p_flat, ya, *os_, *ls_, *consts)


FF_CHUNKS = ((0, 1024), (1024, 1024), (2048, 768))
FF_RB = 256


def _rms(x, g):
    return x * lax.rsqrt(jnp.mean(x * x, axis=-1, keepdims=True) + EPS) * g


def _ffn_kernel(x_ref, gf_ref, wg_ref, wu_ref, wd_ref, gl_ref, out_ref):
    for rc in range(x_ref.shape[0] // FF_RB):
        rows = slice(rc * FF_RB, (rc + 1) * FF_RB)
        x = x_ref[rows]
        hf = _rms(x, gf_ref[...]).astype(BF16)
        acc = x
        for start, size in FF_CHUNKS:
            cs = slice(start, start + size)
            gt = jnp.dot(hf, wg_ref[:, cs], preferred_element_type=F32)
            up = jnp.dot(hf, wu_ref[:, cs], preferred_element_type=F32)
            act = (gt * jax.nn.sigmoid(gt) * up).astype(BF16)
            acc = acc + jnp.dot(act, wd_ref[cs, :], preferred_element_type=F32)
        out_ref[rows] = _rms(acc, gl_ref[...])


def _ffn(x1, gf, wg, wu, wd, gl, tm=512):
    n = x1.shape[0]
    row = pl.BlockSpec((tm, D_MODEL), lambda i: (i, 0))
    once = lambda a: pl.BlockSpec(a.shape, lambda i: (0, 0), pipeline_mode=pl.Buffered(1))
    return pl.pallas_call(
        _ffn_kernel,
        grid=(n // tm,),
        in_specs=[row, once(gf), once(wg), once(wu), once(wd), once(gl)],
        out_specs=row,
        out_shape=jax.ShapeDtypeStruct((n, D_MODEL), F32),
        compiler_params=pltpu.CompilerParams(
            dimension_semantics=("parallel",), vmem_limit_bytes=VMEM_LIMIT),
        name="ffn",
    )(x1, gf, wg, wu, wd, gl)


def _layer(x2, B, S, bias, norm_mix_g, w_in, b_gate_if, conv_w, conv_b, mlstm_norm_g,
           w_proj_a, w_proj_b, w_out, norm_ffn_g, w_gate, w_up, w_down, final_g):
    sizes = (M_WIDTH, M_WIDTH, M_WIDTH, 2 * M_HEADS, M_WIDTH, A_WIDTH, A_WIDTH, A_WIDTH,
             2 * D_MODEL)
    offs = np.concatenate([[0], np.cumsum(sizes)])
    piece = lambda i: w_in[:, offs[i]:offs[i + 1]]
    qkv = lambda g: [piece(i)[:, g * A_GW:(g + 1) * A_GW] * sc
                     for i, sc in ((5, QK_SCALE), (6, 1.0), (7, 1.0))]
    w_p = jnp.concatenate([piece(0), piece(1), piece(2), piece(4), piece(8)]
                          + [w for g in range(N_GROUPS) for w in qkv(g)], axis=1).astype(BF16).T
    w_if = jnp.concatenate(_split_bf16(piece(3).T, 2), axis=0)

    p_flat, if_t, *dilated = _inproj(x2, norm_mix_g[None], w_p, w_if, B, S)

    bif = jnp.broadcast_to(b_gate_if[:, None], (SUBLANES, LANES))
    gates = _gates(if_t, bif, B, S)
    ya = _mlstm(p_flat.reshape(B, S, P_WIDTH), gates, conv_w, conv_b[None],
                mlstm_norm_g[None]).reshape(B * S, M_WIDTH)

    os_, ls_ = [], []
    for g in range(N_GROUPS):
        if g == 0:
            o, lse = _attn_group(p_flat.reshape(B, 1, S, P_WIDTH), P_A0 // A_GW, bias[g], g)
        else:
            o, lse = _attn_group(dilated[g - 1], 0, bias[g], g)
        os_.append(o)
        ls_.append(lse)

    x1 = _merge(x2, p_flat, ya, os_, ls_, w_proj_a.astype(BF16), w_proj_b.astype(BF16),
                w_out.astype(BF16), S)
    return _ffn(x1, norm_ffn_g[None], w_gate.astype(BF16), w_up.astype(BF16),
                w_down.astype(BF16), final_g[None])


def kernel(x, norm_mix_g, w_in, b_gate_if, conv_w, conv_b, mlstm_norm_g, w_proj_a, w_proj_b,
           w_out, norm_ffn_g, w_gate, w_up, w_down, rel_bias, norm_final_g):
    B, S, _ = x.shape
    depth = w_in.shape[0]
    assert depth == 1, "the final norm is fused into the (single) layer's channel mixer"
    bias = _bias_tables(rel_bias)
    out = _layer(x.reshape(B * S, D_MODEL), B, S, bias, norm_mix_g[0], w_in[0], b_gate_if[0],
                 conv_w[0], conv_b[0], mlstm_norm_g[0], w_proj_a[0], w_proj_b[0], w_out[0],
                 norm_ffn_g[0], w_gate[0], w_up[0], w_down[0], norm_final_g)
    return out.reshape(B, S, D_MODEL)
```

```python
import functools
import math

import numpy as np
import jax
import jax.numpy as jnp
from jax import lax
from jax.experimental import pallas as pl
from jax.experimental.pallas import tpu as pltpu

F32 = jnp.float32
BF16 = jnp.bfloat16

D_MODEL = 1024
M_HEADS = 4
M_DH = 256
M_WIDTH = M_HEADS * M_DH
CONV_K = 4
CHUNK = 128
GROUPS = ((128, 1), (512, 4), (2048, 16))
N_GROUPS = len(GROUPS)
A_HG = 8
A_DH = 64
A_GW = A_HG * A_DH
A_WIDTH = N_GROUPS * A_GW
A_BLK = 128
N_BUCKETS = 32
MAX_DISTANCE = 2048
D_FF = 2816
EPS = 1e-6
NEG = -1e30
LOG2E = math.log2(math.e)
QK_SCALE = A_DH ** -0.5 * LOG2E
LSE_PARTS = 3

LANES = 128
SUBLANES = 8

P_QM = 0
P_KM = 1024
P_VM = 2048
P_OM = 3072
P_GATE = 4096
P_A0 = 6144
P_WIDTH = P_A0 + 3 * A_GW
IN_TN = 3 * A_GW
N_DIL = N_GROUPS - 1

VMEM_LIMIT = 56 * 1024 * 1024


def _split_bf16(x, parts):
    out = []
    for _ in range(parts):
        hi = x.astype(BF16)
        out.append(hi)
        x = x - hi.astype(F32)
    return out


def _bucket_tables():
    i = np.arange(A_BLK)[:, None]
    j = np.arange(2 * A_BLK)[None, :]
    dist = i + A_BLK - j
    buckets = []
    for window, dil in GROUPS:
        n = np.maximum(dist, 0) * dil
        nf = np.maximum(n, 1).astype(np.float32)
        max_exact = N_BUCKETS // 2
        large = max_exact + (np.log(nf / max_exact) / math.log(MAX_DISTANCE / max_exact)
                             * (N_BUCKETS - max_exact)).astype(np.int32)
        large = np.minimum(large, N_BUCKETS - 1)
        buckets.append(np.where(n < max_exact, n, large).astype(np.int32))
    span = GROUPS[0][0] // GROUPS[0][1]
    assert all(w // d == span for w, d in GROUPS)
    valid = ((dist >= 0) & (dist <= span)).astype(np.int32)
    valid_first = (valid.astype(bool) & (j >= A_BLK)).astype(np.int32)
    return np.stack(buckets), np.stack([valid, valid_first])


def _bias_kernel(tab_ref, bucket_ref, valid_ref, out_ref):
    g = pl.program_id(0)
    bucket = bucket_ref[...]
    for h in range(A_HG):
        acc = jnp.zeros(bucket.shape, F32)
        for b in range(N_BUCKETS):
            acc = jnp.where(bucket == b, tab_ref[b, g * A_HG + h], acc)
        acc = acc * LOG2E
        out_ref[0, h] = jnp.where(valid_ref[0] > 0, acc, NEG)
        out_ref[1, h] = jnp.where(valid_ref[1] > 0, acc, NEG)


def _bias_tables(rel_bias):
    buckets, valid = _bucket_tables()
    return pl.pallas_call(
        _bias_kernel,
        grid=(N_GROUPS,),
        in_specs=[
            pl.BlockSpec(memory_space=pltpu.SMEM),
            pl.BlockSpec((None, A_BLK, 2 * A_BLK), lambda g: (g, 0, 0)),
            pl.BlockSpec((2, A_BLK, 2 * A_BLK), lambda g: (0, 0, 0)),
        ],
        out_specs=pl.BlockSpec((None, 2, A_HG, A_BLK, 2 * A_BLK), lambda g: (g, 0, 0, 0, 0)),
        out_shape=jax.ShapeDtypeStruct((N_GROUPS, 2, A_HG, A_BLK, 2 * A_BLK), F32),
        name="bias_tables",
    )(rel_bias, jnp.asarray(buckets), jnp.asarray(valid))


def _deinterleave_matrix(rows, d):
    out = np.arange(rows)
    src = d * (out % (rows // d)) + out // (rows // d)
    return (src[:, None] == np.arange(rows)[None, :]).astype(np.float32)


PERM_BLK = 256
NORM_RB = 256


def _inproj_kernel(x_ref, g_ref, w_ref, wif_ref, *rest, n_nat, dils):
    perm_refs = rest[:N_DIL]
    p_ref, if_ref = rest[N_DIL:N_DIL + 2]
    a_refs = rest[N_DIL + 2:2 * N_DIL + 2]
    h_ref, hp_ref = rest[-2:]
    tm = x_ref.shape[0]
    j = pl.program_id(1)
    nt_dims = (((1,), (1,)), ((), ()))

    @pl.when(j == 0)
    def _():
        for rc in range(tm // NORM_RB):
            rows = slice(rc * NORM_RB, (rc + 1) * NORM_RB)
            x = x_ref[rows]
            r = lax.rsqrt(jnp.mean(x * x, axis=-1, keepdims=True) + EPS)
            h = x * r * g_ref[...]
            h_hi, h_lo = _split_bf16(h, 2)
            h_ref[rows] = h_hi
            acc = (lax.dot_general(wif_ref[...], h_hi, nt_dims, preferred_element_type=F32)
                   + lax.dot_general(wif_ref[...], h_lo, nt_dims, preferred_element_type=F32))
            if_ref[:, rows] = acc[:SUBLANES] + acc[SUBLANES:]
            acc = lax.dot_general(h_hi, w_ref[...], nt_dims, preferred_element_type=F32)
            p_ref[rows] = acc.astype(BF16)

    is_gate = (j >= P_OM // IN_TN) & (j < P_A0 // IN_TN)

    @pl.when((j > 0) & (j < n_nat) & jnp.logical_not(is_gate))
    def _():
        acc = lax.dot_general(h_ref[...], w_ref[...], nt_dims, preferred_element_type=F32)
        p_ref[...] = acc.astype(BF16)

    @pl.when(is_gate)
    def _():
        acc = lax.dot_general(h_ref[...], w_ref[...], nt_dims, preferred_element_type=F32)
        p_ref[...] = (0.5 * jnp.tanh(0.5 * acc) + 0.5).astype(BF16)

    for idx, d in enumerate(dils):
        @pl.when(j == n_nat + idx)
        def _(perm_ref=perm_refs[idx], a_ref=a_refs[idx], d=d):
            piece = PERM_BLK // d
            for c in range(tm // PERM_BLK):
                hp = jnp.dot(perm_ref[...], h_ref[c * PERM_BLK:(c + 1) * PERM_BLK],
                             preferred_element_type=F32).astype(BF16)
                for r in range(d):
                    dst = r * (tm // d) + c * piece
                    hp_ref[dst:dst + piece] = hp[r * piece:(r + 1) * piece]
            a = lax.dot_general(hp_ref[...], w_ref[...], nt_dims, preferred_element_type=F32)
            a_ref[...] = a.astype(BF16).reshape(a_ref.shape)


def _inproj(x2, g, w_p, w_if, B, S, tm=1024):
    n = x2.shape[0]
    nt = S // tm
    n_nat = P_WIDTH // IN_TN
    dils = tuple(d for _, d in GROUPS[1:])
    perms = [jnp.asarray(_deinterleave_matrix(PERM_BLK, d), BF16) for d in dils]
    return pl.pallas_call(
        functools.partial(_inproj_kernel, n_nat=n_nat, dils=dils),
        grid=(n // tm, n_nat + N_DIL),
        in_specs=[
            pl.BlockSpec((tm, D_MODEL), lambda i, j: (i, 0)),
            pl.BlockSpec((1, D_MODEL), lambda i, j: (0, 0)),
            pl.BlockSpec((IN_TN, D_MODEL), lambda i, j: (j, 0)),
            pl.BlockSpec((2 * SUBLANES, D_MODEL), lambda i, j: (0, 0)),
        ] + [pl.BlockSpec((PERM_BLK, PERM_BLK), lambda i, j: (0, 0)) for _ in dils],
        out_specs=[
            pl.BlockSpec((tm, IN_TN), lambda i, j: (i, jnp.minimum(j, n_nat - 1))),
            pl.BlockSpec((SUBLANES, tm), lambda i, j: (0, i)),
        ] + [pl.BlockSpec((None, d, tm // d, IN_TN), lambda i, j: (i // nt, 0, i % nt, 0))
             for d in dils],
        out_shape=[
            jax.ShapeDtypeStruct((n, P_WIDTH), BF16),
            jax.ShapeDtypeStruct((SUBLANES, n), F32),
        ] + [jax.ShapeDtypeStruct((B, d, S // d, IN_TN), BF16) for d in dils],
        scratch_shapes=[pltpu.VMEM((tm, D_MODEL), BF16), pltpu.VMEM((tm, D_MODEL), BF16)],
        compiler_params=pltpu.CompilerParams(
            dimension_semantics=("parallel", "arbitrary"), vmem_limit_bytes=VMEM_LIMIT),
        name="inproj",
    )(x2, g, w_p, w_if, *perms)


def _conv_shift_matrix(L):
    return np.concatenate([np.eye(L, k=-(CONV_K - 1 - j)) for j in range(CONV_K)], axis=1)


def _conv_silu(x_ref, tail_ref, shift_ref, cw, cb):
    L, C = x_ref.shape
    pack = 2 * SUBLANES
    x3 = x_ref[...].reshape(L // pack, pack, C)
    prods = []
    for j in range(CONV_K):
        wj = jnp.broadcast_to(cw[j:j + 1], (pack, C)).astype(BF16)
        prods.append((x3 * wj[None]).reshape(L, C))
    y = jnp.dot(shift_ref[...], jnp.concatenate(prods, axis=0),
                preferred_element_type=F32) + cb
    tail = tail_ref[...]
    row = lax.broadcasted_iota(jnp.int32, tail.shape, 0)
    fix = jnp.zeros(tail.shape, F32)
    for k in range(1, CONV_K):
        tap = pltpu.roll(tail, k, 0) * cw[CONV_K - 1 - k:CONV_K - k]
        fix = fix + jnp.where(row < k, tap, 0.0)
    y = jnp.concatenate([y[:SUBLANES] + fix, y[SUBLANES:]], axis=0)
    tail_ref[...] = x_ref[L - pack:].astype(F32)[pack - SUBLANES:]
    return y * jax.nn.sigmoid(y)


GATE_ROWS = 32


def _gates_kernel(ift_ref, bif_ref, out_ref):
    nseq, _, S = out_ref.shape
    nc = S // CHUNK
    g = ift_ref[...] + jnp.concatenate([bif_ref[...]] * (nseq * nc), axis=1)
    ig = g[:M_HEADS]
    lf = jax.nn.log_sigmoid(g[M_HEADS:])
    pos = lax.broadcasted_iota(jnp.int32, lf.shape, 1) % CHUNK

    def scan(x, op, fill):
        sh = 1
        while sh < CHUNK:
            x = op(x, jnp.where(pos >= sh, pltpu.roll(x, sh, 1), fill))
            sh *= 2
        return x

    def last(x):
        x = jnp.where(pos == CHUNK - 1, x, NEG)
        sh = 1
        while sh < CHUNK:
            x = jnp.maximum(x, jnp.where(pos < CHUNK - sh, pltpu.roll(x, nseq * S - sh, 1), NEG))
            sh *= 2
        return x

    b = scan(lf, jnp.add, 0.0)
    a = ig - b
    b_last = last(b)
    a_end = b_last + a
    a_max = last(scan(a_end, jnp.maximum, NEG))
    m_prev = []
    for c in range(nseq * nc):
        if c % nc == 0:
            m = jnp.zeros((M_HEADS, CHUNK), F32)
        m_prev.append(m)
        m = jnp.maximum(b_last[:, c * CHUNK:(c + 1) * CHUNK] + m,
                        a_max[:, c * CHUNK:(c + 1) * CHUNK])
    m_prev = jnp.concatenate(m_prev, axis=1)
    m_new = jnp.maximum(b_last + m_prev, a_max)
    mx = jnp.maximum(m_prev, scan(a, jnp.maximum, NEG))
    rows = jnp.concatenate(
        [-mx, jnp.exp(m_prev - mx), jnp.exp(-(b + mx)), jnp.exp(a_end - m_new), a,
         jnp.exp(b_last + m_prev - m_new),
         jnp.zeros((GATE_ROWS - 6 * M_HEADS, nseq * S), F32)], axis=0)
    for s in range(nseq):
        out_ref[s] = rows[:, s * S:(s + 1) * S]


def _gates(ift, bif, B, S, nseq=4):
    nseq = nseq if B % nseq == 0 else 1
    return pl.pallas_call(
        _gates_kernel,
        grid=(B // nseq,),
        in_specs=[pl.BlockSpec((SUBLANES, nseq * S), lambda b: (0, b)),
                  pl.BlockSpec((SUBLANES, LANES), lambda b: (0, 0))],
        out_specs=pl.BlockSpec((nseq, GATE_ROWS, S), lambda b: (b, 0, 0)),
        out_shape=jax.ShapeDtypeStruct((B, GATE_ROWS, S), F32),
        compiler_params=pltpu.CompilerParams(dimension_semantics=("parallel",)),
        name="gates",
    )(ift, bif)


def _mlstm_kernel(q_ref, k_ref, v_ref, o_ref, g_ref, cw_ref, cb_ref, ng_ref, shift_ref,
                  eye_ref, avg_ref, y_ref, c_ref, n_ref, qt_ref, kt_ref):
    L = CHUNK
    nt_dims = (((1,), (1,)), ((), ()))

    @pl.when(pl.program_id(1) == 0)
    def _():
        c_ref[...] = jnp.zeros_like(c_ref)
        n_ref[...] = jnp.zeros_like(n_ref)
        qt_ref[...] = jnp.zeros_like(qt_ref)
        kt_ref[...] = jnp.zeros_like(kt_ref)

    cw = cw_ref[...]
    cb = cb_ref[...]
    tri = (lax.broadcasted_iota(jnp.int32, (L, L), 0)
           >= lax.broadcasted_iota(jnp.int32, (L, L), 1))

    units = []
    for s in range(q_ref.shape[0]):
        q_all = _conv_silu(q_ref.at[s], qt_ref.at[s], shift_ref, cw[:, :M_WIDTH], cb[:, :M_WIDTH])
        q_all = (q_all * (M_DH ** -0.5)).astype(BF16)
        k_all = _conv_silu(k_ref.at[s], kt_ref.at[s], shift_ref, cw[:, M_WIDTH:], cb[:, M_WIDTH:])
        rows = g_ref[s]
        cols = jnp.transpose(rows)
        for h in range(M_HEADS):
            sl = slice(h * M_DH, (h + 1) * M_DH)
            col = lambda i: cols[:, i * M_HEADS + h:i * M_HEADS + h + 1]
            units.append(dict(
                s=s, h=h, sl=sl, qb=q_all[:, sl], k=k_all[:, sl], vb=v_ref[s, :, sl],
                u_col=col(0), wi_col=col(1), en_col=col(2), wk_col=col(3),
                a_row=rows[4 * M_HEADS + h:4 * M_HEADS + h + 1],
                dec_row=rows[5 * M_HEADS + h:5 * M_HEADS + h + 1]))

    for u in units:
        c_old = c_ref[u["s"], u["h"]]
        n_old = n_ref[u["s"], u["h"]:u["h"] + 1]
        kb = u["k"].astype(BF16)
        kn = jnp.concatenate([kb, jnp.broadcast_to(n_old, (L, M_DH)).astype(BF16)], axis=0)
        u["s_aug"] = lax.dot_general(u["qb"], kn, nt_dims, preferred_element_type=F32)
        u["qc"] = jnp.dot(u["qb"], c_old.astype(BF16), preferred_element_type=F32)
        kw = u["k"] * u["wk_col"]
        u["kw_t"] = lax.dot_general(eye_ref[...], kw.astype(BF16), nt_dims,
                                    preferred_element_type=F32).astype(BF16)
        dec = jnp.concatenate([u["dec_row"], u["dec_row"]], axis=1)
        u["c_dec"] = dec * c_old
        n_ref[u["s"], u["h"]:u["h"] + 1] = dec * n_old + jnp.sum(kw, axis=0, keepdims=True)

    for u in units:
        dmat = jnp.where(tri, jnp.exp(u["u_col"] + u["a_row"]), 0.0)
        w_intra = dmat * u["s_aug"][:, :L]
        u["w_sum"] = jnp.sum(w_intra, axis=1, keepdims=True)
        u["w_intra"] = w_intra.astype(BF16)

    for u in units:
        u["pv"] = jnp.dot(u["w_intra"], u["vb"], preferred_element_type=F32)
        c_ref[u["s"], u["h"]] = u["c_dec"] + jnp.dot(u["kw_t"], u["vb"],
                                                     preferred_element_type=F32)

    for u in units:
        num = u["wi_col"] * u["qc"] + u["pv"]
        den = u["wi_col"] * u["s_aug"][:, L:L + 1] + u["w_sum"]
        hout = num / jnp.maximum(jnp.abs(den), u["en_col"])
        mu = jnp.dot(hout.astype(BF16), avg_ref[...], preferred_element_type=F32)
        u["cen"] = hout - jnp.concatenate([mu] * (M_DH // LANES), axis=1)
        var = jnp.dot((u["cen"] * u["cen"]).astype(BF16), avg_ref[...],
                      preferred_element_type=F32)
        u["var"] = jnp.concatenate([var] * (M_DH // LANES), axis=1)

    for u in units:
        hn = u["cen"] * lax.rsqrt(u["var"] + EPS) * ng_ref[:, u["sl"]]
        y_ref[u["s"], :, u["sl"]] = (o_ref[u["s"], :, u["sl"]].astype(F32) * hn).astype(BF16)


def _mlstm(p3, gates, conv_w, conv_b, ng, nseq=4):
    B, S, _ = p3.shape
    nseq = nseq if B % nseq == 0 else 1
    wblk = lambda col: pl.BlockSpec((nseq, CHUNK, M_WIDTH), lambda b, c: (b, c, col // M_WIDTH))
    full = lambda shape: pl.BlockSpec(shape, lambda b, c: (0,) * len(shape))
    return pl.pallas_call(
        _mlstm_kernel,
        grid=(B // nseq, S // CHUNK),
        in_specs=[
            wblk(P_QM), wblk(P_KM), wblk(P_VM), wblk(P_OM),
            pl.BlockSpec((nseq, GATE_ROWS, CHUNK), lambda b, c: (b, 0, c)),
            full((CONV_K, 2 * M_WIDTH)), full((1, 2 * M_WIDTH)), full((1, M_WIDTH)),
            full((CHUNK, CONV_K * CHUNK)), full((M_DH, M_DH)), full((M_DH, LANES)),
        ],
        out_specs=pl.BlockSpec((nseq, CHUNK, M_WIDTH), lambda b, c: (b, c, 0)),
        out_shape=jax.ShapeDtypeStruct((B, S, M_WIDTH), BF16),
        scratch_shapes=[
            pltpu.VMEM((nseq, M_HEADS, M_DH, M_DH), F32),
            pltpu.VMEM((nseq, SUBLANES, M_DH), F32),
            pltpu.VMEM((nseq, SUBLANES, M_WIDTH), F32),
            pltpu.VMEM((nseq, SUBLANES, M_WIDTH), F32),
        ],
        compiler_params=pltpu.CompilerParams(
            dimension_semantics=("parallel", "arbitrary"), vmem_limit_bytes=VMEM_LIMIT),
        name="mlstm",
    )(p3, p3, p3, p3, gates, conv_w, conv_b, ng,
      jnp.asarray(_conv_shift_matrix(CHUNK), BF16), jnp.eye(M_DH, dtype=BF16),
      jnp.full((M_DH, LANES), 1.0 / M_DH, BF16))


def _attn_kernel(q_ref, kc_ref, kp_ref, vc_ref, vp_ref, bias_ref, o_ref, lse_ref, *, tq):
    for r in range(q_ref.shape[0]):
        _attn_rows(q_ref.at[r], kc_ref.at[r], kp_ref.at[r], vc_ref.at[r], vp_ref.at[r], bias_ref,
                   o_ref.at[r], lse_ref.at[r], tq=tq)


def _attn_rows(q_ref, kc_ref, kp_ref, vc_ref, vp_ref, bias_ref, o_ref, lse_ref, *, tq):
    first = (pl.program_id(2) == 0).astype(jnp.int32)
    lane = lax.broadcasted_iota(jnp.int32, (1, LANES), 1)
    low = lane < A_DH
    nt_dims = (((1,), (1,)), ((), ()))
    n_pairs = A_HG // 2
    krow = lax.broadcasted_iota(jnp.int32, (4 * A_BLK, LANES), 0)
    ones_blk = jnp.where((krow < 2 * A_BLK) == low, 1.0, 0.0).astype(BF16)
    for i in range(tq // A_BLK):
        rows = slice(i * A_BLK, (i + 1) * A_BLK)
        if i == 0:
            k2 = jnp.concatenate([kp_ref[...], kc_ref[rows]], axis=0)
            v2 = jnp.concatenate([vp_ref[...], vc_ref[rows]], axis=0)
            var = first
        else:
            k2 = kc_ref[(i - 1) * A_BLK:(i + 1) * A_BLK]
            v2 = vc_ref[(i - 1) * A_BLK:(i + 1) * A_BLK]
            var = 0
        q = q_ref[rows]
        zero = jnp.zeros((A_BLK, LANES), BF16)

        scores = []
        for p in range(n_pairs):
            cs = slice(p * LANES, (p + 1) * LANES)
            qp = q[:, cs]
            q2 = jnp.concatenate([jnp.where(low, qp, zero), jnp.where(low, zero, qp)], axis=0)
            scores.append(lax.dot_general(q2, k2[:, cs], nt_dims, preferred_element_type=F32))

        probs, maxes = [], []
        for p in range(n_pairs):
            for e in range(2):
                s = scores[p][e * A_BLK:(e + 1) * A_BLK] + bias_ref[var, 2 * p + e]
                mx = jnp.max(s, axis=1, keepdims=True)
                probs.append(jnp.exp2(s - mx).astype(BF16))
                maxes.append(mx)

        lse_blk = jnp.zeros((A_BLK, LANES), F32)
        for p in range(n_pairs):
            cs = slice(p * LANES, (p + 1) * LANES)
            vp2 = v2[:, cs]
            zero2 = jnp.zeros_like(vp2)
            v_cat = jnp.concatenate([jnp.where(low, vp2, zero2), jnp.where(low, zero2, vp2)],
                                    axis=0)
            p_cat = jnp.concatenate([probs[2 * p], probs[2 * p + 1]], axis=1)
            o_ext = jnp.dot(p_cat, jnp.concatenate([v_cat, ones_blk], axis=1),
                            preferred_element_type=F32)
            l_pair = o_ext[:, LANES:]
            o_ref[rows, cs] = (o_ext[:, :LANES] * (1.0 / l_pair)).astype(BF16)
            mx_pair = jnp.where(low, maxes[2 * p], maxes[2 * p + 1])
            lse_pair = mx_pair * (1.0 / LOG2E) + jnp.log(l_pair)
            lse_blk = jnp.where(lane % A_DH == p, lse_pair, lse_blk)

        packed = jnp.zeros((A_BLK, LANES), F32)
        rest = lse_blk
        for part in range(LSE_PARTS):
            hi = rest.astype(BF16).astype(F32)
            rest = rest - hi
            packed = packed + (hi if part == 0 else pltpu.roll(hi, part * SUBLANES, 1))
        lse_ref[rows] = packed.astype(BF16)


def _attn_group(src, col0, bias_g, g, rows=1024):
    B, d, sd, _ = src.shape
    tq = min(rows, sd)
    nres = min(rows // tq, d)
    cur = lambda c: pl.BlockSpec((None, nres, tq, A_GW), lambda b, r, n: (b, r, n, col0 + c))
    prev = lambda c: pl.BlockSpec(
        (None, nres, A_BLK, A_GW),
        lambda b, r, n: (b, r, jnp.maximum(n * (tq // A_BLK) - 1, 0), col0 + c))
    return pl.pallas_call(
        functools.partial(_attn_kernel, tq=tq),
        grid=(B, d // nres, sd // tq),
        in_specs=[
            cur(0), cur(1), prev(1), cur(2), prev(2),
            pl.BlockSpec((2, A_HG, A_BLK, 2 * A_BLK), lambda b, r, n: (0, 0, 0, 0)),
        ],
        out_specs=[
            pl.BlockSpec((None, nres, tq, A_GW), lambda b, r, n: (b, r, n, 0)),
            pl.BlockSpec((None, nres, tq, LANES), lambda b, r, n: (b, r, n, 0)),
        ],
        out_shape=[
            jax.ShapeDtypeStruct((B, d, sd, A_GW), BF16),
            jax.ShapeDtypeStruct((B, d, sd, LANES), BF16),
        ],
        compiler_params=pltpu.CompilerParams(
            dimension_semantics=("parallel", "parallel", "arbitrary"),
            vmem_limit_bytes=VMEM_LIMIT),
        name=f"attn_g{g}",
    )(src, src, src, src, src, bias_g)


MERGE_RB = 512
UNPERM_RB = 256


def _merge_kernel(x_ref, gate_ref, ya_ref, *rest):
    o_refs = rest[:N_GROUPS]
    l_refs = rest[N_GROUPS:2 * N_GROUPS]
    unperm_refs = rest[2 * N_GROUPS:2 * N_GROUPS + N_DIL]
    e_ref, wa_ref, wb_ref, wo_ref, out_ref = rest[2 * N_GROUPS + N_DIL:]
    rb = MERGE_RB

    for c in range(x_ref.shape[0] // rb):
        rows = slice(c * rb, (c + 1) * rb)
        outs, lses = [], []
        for g in range(N_GROUPS):
            d = o_refs[g].shape[0]
            if g == 0:
                o = o_refs[g][:, c * rb:(c + 1) * rb].reshape(rb, A_GW).astype(F32)
                lp = l_refs[g][:, c * rb:(c + 1) * rb].reshape(rb, LANES).astype(F32)
            else:
                per = UNPERM_RB // d
                parts = []
                for cc in range(c * rb // UNPERM_RB, (c + 1) * rb // UNPERM_RB):
                    o = o_refs[g][:, cc * per:(cc + 1) * per].reshape(UNPERM_RB, A_GW)
                    lp = l_refs[g][:, cc * per:(cc + 1) * per].reshape(UNPERM_RB, LANES)
                    parts.append(jnp.dot(unperm_refs[g - 1][...],
                                         jnp.concatenate([o, lp], axis=1),
                                         preferred_element_type=F32))
                both = jnp.concatenate(parts, axis=0)
                o, lp = both[:, :A_GW], both[:, A_GW:]
            outs.append(o)
            lses.append(sum(lp if part == 0 else pltpu.roll(lp, LANES - part * SUBLANES, 1)
                            for part in range(LSE_PARTS)))

        lm = functools.reduce(jnp.maximum, lses)
        es = [jnp.exp(l - lm) for l in lses]
        den = functools.reduce(jnp.add, es)
        yb = jnp.zeros((rb, A_GW), F32)
        for e, o in zip(es, outs):
            wide = jnp.dot((e / den).astype(BF16), e_ref[...], preferred_element_type=F32)
            yb = yb + wide * o
        pa = jnp.dot(ya_ref[rows], wa_ref[...], preferred_element_type=F32)
        pb = jnp.dot(yb.astype(BF16), wb_ref[...], preferred_element_type=F32)
        ga = gate_ref[rows, :D_MODEL].astype(F32)
        gb = gate_ref[rows, D_MODEL:].astype(F32)
        merged = (ga * pa + gb * pb).astype(BF16)
        out_ref[rows] = x_ref[rows] + jnp.dot(merged, wo_ref[...], preferred_element_type=F32)


def _merge(x2, p_flat, ya, os_, ls_, wa, wb, wo, S, tm=1024):
    n = x2.shape[0]
    nt = S // tm
    row = lambda w: pl.BlockSpec((tm, w), lambda i: (i, 0))
    full = lambda a: pl.BlockSpec(a.shape, lambda i: (0, 0))
    grp = lambda a: pl.BlockSpec((None, a.shape[1], tm // a.shape[1], a.shape[3]),
                                 lambda i: (i // nt, 0, i % nt, 0))
    unperms = [jnp.asarray(_deinterleave_matrix(UNPERM_RB, d).T, BF16) for _, d in GROUPS[1:]]
    head_lane = (np.arange(A_HG) % 2) * A_DH + np.arange(A_HG) // 2
    expand = jnp.asarray(
        np.arange(LANES)[:, None] == head_lane[np.arange(A_GW) // A_DH][None, :], BF16)
    consts = unperms + [expand, wa, wb, wo]
    return pl.pallas_call(
        _merge_kernel,
        grid=(n // tm,),
        in_specs=[row(D_MODEL),
                  pl.BlockSpec((tm, 2 * D_MODEL), lambda i: (i, P_GATE // (2 * D_MODEL))),
                  row(M_WIDTH)]
                 + [grp(a) for a in os_] + [grp(a) for a in ls_] + [full(a) for a in consts],
        out_specs=row(D_MODEL),
        out_shape=jax.ShapeDtypeStruct((n, D_MODEL), F32),
        compiler_params=pltpu.CompilerParams(
            dimension_semantics=("parallel",), vmem_limit_bytes=VMEM_LIMIT),
        name="merge",
    )(x2, p_flat, ya, *os_, *ls_, *consts)


FF_CHUNKS = ((0, 1024), (1024, 1024), (2048, 768))
FF_RB = 256


def _rms(x, g):
    return x * lax.rsqrt(jnp.mean(x * x, axis=-1, keepdims=True) + EPS) * g


def _ffn_kernel(x_ref, gf_ref, wg_ref, wu_ref, wd_ref, gl_ref, out_ref):
    for rc in range(x_ref.shape[0] // FF_RB):
        rows = slice(rc * FF_RB, (rc + 1) * FF_RB)
        x = x_ref[rows]
        hf = _rms(x, gf_ref[...]).astype(BF16)
        acc = x
        for start, size in FF_CHUNKS:
            cs = slice(start, start + size)
            gt = jnp.dot(hf, wg_ref[:, cs], preferred_element_type=F32)
            up = jnp.dot(hf, wu_ref[:, cs], preferred_element_type=F32)
            act = (gt * jax.nn.sigmoid(gt) * up).astype(BF16)
            acc = acc + jnp.dot(act, wd_ref[cs, :], preferred_element_type=F32)
        out_ref[rows] = _rms(acc, gl_ref[...])


def _ffn(x1, gf, wg, wu, wd, gl, tm=512):
    n = x1.shape[0]
    row = pl.BlockSpec((tm, D_MODEL), lambda i: (i, 0))
    once = lambda a: pl.BlockSpec(a.shape, lambda i: (0, 0), pipeline_mode=pl.Buffered(1))
    return pl.pallas_call(
        _ffn_kernel,
        grid=(n // tm,),
        in_specs=[row, once(gf), once(wg), once(wu), once(wd), once(gl)],
        out_specs=row,
        out_shape=jax.ShapeDtypeStruct((n, D_MODEL), F32),
        compiler_params=pltpu.CompilerParams(
            dimension_semantics=("parallel",), vmem_limit_bytes=VMEM_LIMIT),
        name="ffn",
    )(x1, gf, wg, wu, wd, gl)


def _layer(x2, B, S, bias, norm_mix_g, w_in, b_gate_if, conv_w, conv_b, mlstm_norm_g,
           w_proj_a, w_proj_b, w_out, norm_ffn_g, w_gate, w_up, w_down, final_g):
    sizes = (M_WIDTH, M_WIDTH, M_WIDTH, 2 * M_HEADS, M_WIDTH, A_WIDTH, A_WIDTH, A_WIDTH,
             2 * D_MODEL)
    offs = np.concatenate([[0], np.cumsum(sizes)])
    piece = lambda i: w_in[:, offs[i]:offs[i + 1]]
    qkv = lambda g: [piece(i)[:, g * A_GW:(g + 1) * A_GW] * sc
                     for i, sc in ((5, QK_SCALE), (6, 1.0), (7, 1.0))]
    w_p = jnp.concatenate([piece(0), piece(1), piece(2), piece(4), piece(8)]
                          + [w for g in range(N_GROUPS) for w in qkv(g)], axis=1).astype(BF16).T
    w_if = jnp.concatenate(_split_bf16(piece(3).T, 2), axis=0)

    p_flat, if_t, *dilated = _inproj(x2, norm_mix_g[None], w_p, w_if, B, S)

    bif = jnp.broadcast_to(b_gate_if[:, None], (SUBLANES, LANES))
    gates = _gates(if_t, bif, B, S)
    ya = _mlstm(p_flat.reshape(B, S, P_WIDTH), gates, conv_w, conv_b[None],
                mlstm_norm_g[None]).reshape(B * S, M_WIDTH)

    os_, ls_ = [], []
    for g in range(N_GROUPS):
        if g == 0:
            o, lse = _attn_group(p_flat.reshape(B, 1, S, P_WIDTH), P_A0 // A_GW, bias[g], g)
        else:
            o, lse = _attn_group(dilated[g - 1], 0, bias[g], g)
        os_.append(o)
        ls_.append(lse)

    x1 = _merge(x2, p_flat, ya, os_, ls_, w_proj_a.astype(BF16), w_proj_b.astype(BF16),
                w_out.astype(BF16), S)
    return _ffn(x1, norm_ffn_g[None], w_gate.astype(BF16), w_up.astype(BF16),
                w_down.astype(BF16), final_g[None])


def kernel(x, norm_mix_g, w_in, b_gate_if, conv_w, conv_b, mlstm_norm_g, w_proj_a, w_proj_b,
           w_out, norm_ffn_g, w_gate, w_up, w_down, rel_bias, norm_final_g):
    B, S, _ = x.shape
    depth = w_in.shape[0]
    assert depth == 1, "the final norm is fused into the (single) layer's channel mixer"
    bias = _bias_tables(rel_bias)
    out = _layer(x.reshape(B * S, D_MODEL), B, S, bias, norm_mix_g[0], w_in[0], b_gate_if[0],
                 conv_w[0], conv_b[0], mlstm_norm_g[0], w_proj_a[0], w_proj_b[0], w_out[0],
                 norm_ffn_g[0], w_gate[0], w_up[0], w_down[0], norm_final_g)
    return out.reshape(B, S, D_MODEL)
```

```python
import functools
import math

import numpy as np
import jax
import jax.numpy as jnp
from jax import lax
from jax.experimental import pallas as pl
from jax.experimental.pallas import tpu as pltpu

F32 = jnp.float32
BF16 = jnp.bfloat16

D_MODEL = 1024
M_HEADS = 4
M_DH = 256
M_WIDTH = M_HEADS * M_DH
CONV_K = 4
CHUNK = 128
GROUPS = ((128, 1), (512, 4), (2048, 16))
N_GROUPS = len(GROUPS)
A_HG = 8
A_DH = 64
A_GW = A_HG * A_DH
A_WIDTH = N_GROUPS * A_GW
A_BLK = 128
N_BUCKETS = 32
MAX_DISTANCE = 2048
D_FF = 2816
EPS = 1e-6
NEG = -1e30
LOG2E = math.log2(math.e)
QK_SCALE = A_DH ** -0.5 * LOG2E
LSE_PARTS = 3

LANES = 128
SUBLANES = 8

P_QM = 0
P_KM = 1024
P_VM = 2048
P_OM = 3072
P_GATE = 4096
P_A0 = 6144
P_WIDTH = P_A0 + 3 * A_GW
IN_TN = 3 * A_GW
N_DIL = N_GROUPS - 1

VMEM_LIMIT = 56 * 1024 * 1024


def _split_bf16(x, parts):
    out = []
    for _ in range(parts):
        hi = x.astype(BF16)
        out.append(hi)
        x = x - hi.astype(F32)
    return out


def _bucket_tables():
    i = np.arange(A_BLK)[:, None]
    j = np.arange(2 * A_BLK)[None, :]
    dist = i + A_BLK - j
    buckets = []
    for window, dil in GROUPS:
        n = np.maximum(dist, 0) * dil
        nf = np.maximum(n, 1).astype(np.float32)
        max_exact = N_BUCKETS // 2
        large = max_exact + (np.log(nf / max_exact) / math.log(MAX_DISTANCE / max_exact)
                             * (N_BUCKETS - max_exact)).astype(np.int32)
        large = np.minimum(large, N_BUCKETS - 1)
        buckets.append(np.where(n < max_exact, n, large).astype(np.int32))
    span = GROUPS[0][0] // GROUPS[0][1]
    assert all(w // d == span for w, d in GROUPS)
    valid = ((dist >= 0) & (dist <= span)).astype(np.int32)
    valid_first = (valid.astype(bool) & (j >= A_BLK)).astype(np.int32)
    return np.stack(buckets), np.stack([valid, valid_first])


def _bias_kernel(tab_ref, bucket_ref, valid_ref, out_ref):
    g = pl.program_id(0)
    bucket = bucket_ref[...]
    for h in range(A_HG):
        acc = jnp.zeros(bucket.shape, F32)
        for b in range(N_BUCKETS):
            acc = jnp.where(bucket == b, tab_ref[b, g * A_HG + h], acc)
        acc = acc * LOG2E
        out_ref[0, h] = jnp.where(valid_ref[0] > 0, acc, NEG)
        out_ref[1, h] = jnp.where(valid_ref[1] > 0, acc, NEG)


def _bias_tables(rel_bias):
    buckets, valid = _bucket_tables()
    return pl.pallas_call(
        _bias_kernel,
        grid=(N_GROUPS,),
        in_specs=[
            pl.BlockSpec(memory_space=pltpu.SMEM),
            pl.BlockSpec((None, A_BLK, 2 * A_BLK), lambda g: (g, 0, 0)),
            pl.BlockSpec((2, A_BLK, 2 * A_BLK), lambda g: (0, 0, 0)),
        ],
        out_specs=pl.BlockSpec((None, 2, A_HG, A_BLK, 2 * A_BLK), lambda g: (g, 0, 0, 0, 0)),
        out_shape=jax.ShapeDtypeStruct((N_GROUPS, 2, A_HG, A_BLK, 2 * A_BLK), F32),
        name="bias_tables",
    )(rel_bias, jnp.asarray(buckets), jnp.asarray(valid))


def _deinterleave_matrix(rows, d):
    out = np.arange(rows)
    src = d * (out % (rows // d)) + out // (rows // d)
    return (src[:, None] == np.arange(rows)[None, :]).astype(np.float32)


PERM_BLK = 256
NORM_RB = 256


def _inproj_kernel(x_ref, g_ref, w_ref, wif_ref, *rest, n_nat, dils):
    perm_refs = rest[:N_DIL]
    p_ref, if_ref = rest[N_DIL:N_DIL + 2]
    a_refs = rest[N_DIL + 2:2 * N_DIL + 2]
    h_ref, hp_ref = rest[-2:]
    tm = x_ref.shape[0]
    j = pl.program_id(1)
    nt_dims = (((1,), (1,)), ((), ()))

    @pl.when(j == 0)
    def _():
        for rc in range(tm // NORM_RB):
            rows = slice(rc * NORM_RB, (rc + 1) * NORM_RB)
            x = x_ref[rows]
            r = lax.rsqrt(jnp.mean(x * x, axis=-1, keepdims=True) + EPS)
            h = x * r * g_ref[...]
            h_hi, h_lo = _split_bf16(h, 2)
            h_ref[rows] = h_hi
            acc = (lax.dot_general(wif_ref[...], h_hi, nt_dims, preferred_element_type=F32)
                   + lax.dot_general(wif_ref[...], h_lo, nt_dims, preferred_element_type=F32))
            if_ref[:, rows] = acc[:SUBLANES] + acc[SUBLANES:]
            acc = lax.dot_general(h_hi, w_ref[...], nt_dims, preferred_element_type=F32)
            p_ref[rows] = acc.astype(BF16)

    is_gate = (j >= P_OM // IN_TN) & (j < P_A0 // IN_TN)

    @pl.when((j > 0) & (j < n_nat) & jnp.logical_not(is_gate))
    def _():
        acc = lax.dot_general(h_ref[...], w_ref[...], nt_dims, preferred_element_type=F32)
        p_ref[...] = acc.astype(BF16)

    @pl.when(is_gate)
    def _():
        acc = lax.dot_general(h_ref[...], w_ref[...], nt_dims, preferred_element_type=F32)
        p_ref[...] = (0.5 * jnp.tanh(0.5 * acc) + 0.5).astype(BF16)

    for idx, d in enumerate(dils):
        @pl.when(j == n_nat + idx)
        def _(perm_ref=perm_refs[idx], a_ref=a_refs[idx], d=d):
            piece = PERM_BLK // d
            for c in range(tm // PERM_BLK):
                hp = jnp.dot(perm_ref[...], h_ref[c * PERM_BLK:(c + 1) * PERM_BLK],
                             preferred_element_type=F32).astype(BF16)
                for r in range(d):
                    dst = r * (tm // d) + c * piece
                    hp_ref[dst:dst + piece] = hp[r * piece:(r + 1) * piece]
            a = lax.dot_general(hp_ref[...], w_ref[...], nt_dims, preferred_element_type=F32)
            a_ref[...] = a.astype(BF16).reshape(a_ref.shape)


def _inproj(x2, g, w_p, w_if, B, S, tm=1024):
    n = x2.shape[0]
    nt = S // tm
    n_nat = P_WIDTH // IN_TN
    dils = tuple(d for _, d in GROUPS[1:])
    perms = [jnp.asarray(_deinterleave_matrix(PERM_BLK, d), BF16) for d in dils]
    return pl.pallas_call(
        functools.partial(_inproj_kernel, n_nat=n_nat, dils=dils),
        grid=(n // tm, n_nat + N_DIL),
        in_specs=[
            pl.BlockSpec((tm, D_MODEL), lambda i, j: (i, 0)),
            pl.BlockSpec((1, D_MODEL), lambda i, j: (0, 0)),
            pl.BlockSpec((IN_TN, D_MODEL), lambda i, j: (j, 0)),
            pl.BlockSpec((2 * SUBLANES, D_MODEL), lambda i, j: (0, 0)),
        ] + [pl.BlockSpec((PERM_BLK, PERM_BLK), lambda i, j: (0, 0)) for _ in dils],
        out_specs=[
            pl.BlockSpec((tm, IN_TN), lambda i, j: (i, jnp.minimum(j, n_nat - 1))),
            pl.BlockSpec((SUBLANES, tm), lambda i, j: (0, i)),
        ] + [pl.BlockSpec((None, d, tm // d, IN_TN), lambda i, j: (i // nt, 0, i % nt, 0))
             for d in dils],
        out_shape=[
            jax.ShapeDtypeStruct((n, P_WIDTH), BF16),
            jax.ShapeDtypeStruct((SUBLANES, n), F32),
        ] + [jax.ShapeDtypeStruct((B, d, S // d, IN_TN), BF16) for d in dils],
        scratch_shapes=[pltpu.VMEM((tm, D_MODEL), BF16), pltpu.VMEM((tm, D_MODEL), BF16)],
        compiler_params=pltpu.CompilerParams(
            dimension_semantics=("parallel", "arbitrary"), vmem_limit_bytes=VMEM_LIMIT),
        name="inproj",
    )(x2, g, w_p, w_if, *perms)


def _conv_shift_matrix(L):
    return np.concatenate([np.eye(L, k=-(CONV_K - 1 - j)) for j in range(CONV_K)], axis=1)


def _conv_silu(x_ref, tail_ref, shift_ref, cw, cb):
    L, C = x_ref.shape
    pack = 2 * SUBLANES
    x3 = x_ref[...].reshape(L // pack, pack, C)
    prods = []
    for j in range(CONV_K):
        wj = jnp.broadcast_to(cw[j:j + 1], (pack, C)).astype(BF16)
        prods.append((x3 * wj[None]).reshape(L, C))
    y = jnp.dot(shift_ref[...], jnp.concatenate(prods, axis=0),
                preferred_element_type=F32) + cb
    tail = tail_ref[...]
    row = lax.broadcasted_iota(jnp.int32, tail.shape, 0)
    fix = jnp.zeros(tail.shape, F32)
    for k in range(1, CONV_K):
        tap = pltpu.roll(tail, k, 0) * cw[CONV_K - 1 - k:CONV_K - k]
        fix = fix + jnp.where(row < k, tap, 0.0)
    y = jnp.concatenate([y[:SUBLANES] + fix, y[SUBLANES:]], axis=0)
    tail_ref[...] = x_ref[L - pack:].astype(F32)[pack - SUBLANES:]
    return y * jax.nn.sigmoid(y)


GATE_ROWS = 32


def _gates_kernel(ift_ref, bif_ref, out_ref):
    nseq, _, S = out_ref.shape
    nc = S // CHUNK
    g = ift_ref[...] + jnp.concatenate([bif_ref[...]] * (nseq * nc), axis=1)
    ig = g[:M_HEADS]
    lf = jax.nn.log_sigmoid(g[M_HEADS:])
    pos = lax.broadcasted_iota(jnp.int32, lf.shape, 1) % CHUNK

    def scan(x, op, fill):
        sh = 1
        while sh < CHUNK:
            x = op(x, jnp.where(pos >= sh, pltpu.roll(x, sh, 1), fill))
            sh *= 2
        return x

    def last(x):
        x = jnp.where(pos == CHUNK - 1, x, NEG)
        sh = 1
        while sh < CHUNK:
            x = jnp.maximum(x, jnp.where(pos < CHUNK - sh, pltpu.roll(x, nseq * S - sh, 1), NEG))
            sh *= 2
        return x

    b = scan(lf, jnp.add, 0.0)
    a = ig - b
    b_last = last(b)
    a_end = b_last + a
    a_max = last(scan(a_end, jnp.maximum, NEG))
    m_prev = []
    for c in range(nseq * nc):
        if c % nc == 0:
            m = jnp.zeros((M_HEADS, CHUNK), F32)
        m_prev.append(m)
        m = jnp.maximum(b_last[:, c * CHUNK:(c + 1) * CHUNK] + m,
                        a_max[:, c * CHUNK:(c + 1) * CHUNK])
    m_prev = jnp.concatenate(m_prev, axis=1)
    m_new = jnp.maximum(b_last + m_prev, a_max)
    mx = jnp.maximum(m_prev, scan(a, jnp.maximum, NEG))
    rows = jnp.concatenate(
        [-mx, jnp.exp(m_prev - mx), jnp.exp(-(b + mx)), jnp.exp(a_end - m_new), a,
         jnp.exp(b_last + m_prev - m_new),
         jnp.zeros((GATE_ROWS - 6 * M_HEADS, nseq * S), F32)], axis=0)
    for s in range(nseq):
        out_ref[s] = rows[:, s * S:(s + 1) * S]


def _gates(ift, bif, B, S, nseq=4):
    nseq = nseq if B % nseq == 0 else 1
    return pl.pallas_call(
        _gates_kernel,
        grid=(B // nseq,),
        in_specs=[pl.BlockSpec((SUBLANES, nseq * S), lambda b: (0, b)),
                  pl.BlockSpec((SUBLANES, LANES), lambda b: (0, 0))],
        out_specs=pl.BlockSpec((nseq, GATE_ROWS, S), lambda b: (b, 0, 0)),
        out_shape=jax.ShapeDtypeStruct((B, GATE_ROWS, S), F32),
        compiler_params=pltpu.CompilerParams(dimension_semantics=("parallel",)),
        name="gates",
    )(ift, bif)


def _mlstm_kernel(q_ref, k_ref, v_ref, o_ref, g_ref, cw_ref, cb_ref, ng_ref, shift_ref,
                  avg_ref, y_ref, c_ref, n_ref, qt_ref, kt_ref):
    L = CHUNK
    nt_dims = (((1,), (1,)), ((), ()))

    @pl.when(pl.program_id(1) == 0)
    def _():
        c_ref[...] = jnp.zeros_like(c_ref)
        n_ref[...] = jnp.zeros_like(n_ref)
        qt_ref[...] = jnp.zeros_like(qt_ref)
        kt_ref[...] = jnp.zeros_like(kt_ref)

    cw = cw_ref[...]
    cb = cb_ref[...]
    tri = (lax.broadcasted_iota(jnp.int32, (L, L), 0)
           >= lax.broadcasted_iota(jnp.int32, (L, L), 1))

    units = []
    for s in range(q_ref.shape[0]):
        q_all = _conv_silu(q_ref.at[s], qt_ref.at[s], shift_ref, cw[:, :M_WIDTH], cb[:, :M_WIDTH])
        q_all = (q_all * (M_DH ** -0.5)).astype(BF16)
        k_all = _conv_silu(k_ref.at[s], kt_ref.at[s], shift_ref, cw[:, M_WIDTH:], cb[:, M_WIDTH:])
        rows = g_ref[s]
        cols = jnp.transpose(rows)
        for h in range(M_HEADS):
            sl = slice(h * M_DH, (h + 1) * M_DH)
            col = lambda i: cols[:, i * M_HEADS + h:i * M_HEADS + h + 1]
            units.append(dict(
                s=s, h=h, sl=sl, qb=q_all[:, sl], k=k_all[:, sl], vb=v_ref[s, :, sl],
                u_col=col(0), wi_col=col(1), en_col=col(2), wk_col=col(3),
                a_row=rows[4 * M_HEADS + h:4 * M_HEADS + h + 1],
                dec_row=rows[5 * M_HEADS + h:5 * M_HEADS + h + 1]))

    for u in units:
        c_old = c_ref[u["s"], u["h"]]
        n_old = n_ref[u["s"], u["h"]:u["h"] + 1]
        kb = u["k"].astype(BF16)
        kn = jnp.concatenate([kb, jnp.broadcast_to(n_old, (L, M_DH)).astype(BF16)], axis=0)
        u["s_aug"] = lax.dot_general(u["qb"], kn, nt_dims, preferred_element_type=F32)
        u["qc"] = jnp.dot(u["qb"], c_old.astype(BF16), preferred_element_type=F32)
        kw = u["k"] * u["wk_col"]
        u["kw"] = kw.astype(BF16)
        dec = jnp.concatenate([u["dec_row"], u["dec_row"]], axis=1)
        u["c_dec"] = dec * c_old
        n_ref[u["s"], u["h"]:u["h"] + 1] = dec * n_old + jnp.sum(kw, axis=0, keepdims=True)

    for u in units:
        dmat = jnp.where(tri, jnp.exp(u["u_col"] + u["a_row"]), 0.0)
        w_intra = dmat * u["s_aug"][:, :L]
        u["w_sum"] = jnp.sum(w_intra, axis=1, keepdims=True)
        u["w_intra"] = w_intra.astype(BF16)

    for u in units:
        u["pv"] = jnp.dot(u["w_intra"], u["vb"], preferred_element_type=F32)
        c_ref[u["s"], u["h"]] = u["c_dec"] + lax.dot_general(
            u["kw"], u["vb"], (((0,), (0,)), ((), ())), preferred_element_type=F32)

    for u in units:
        num = u["wi_col"] * u["qc"] + u["pv"]
        den = u["wi_col"] * u["s_aug"][:, L:L + 1] + u["w_sum"]
        hout = num / jnp.maximum(jnp.abs(den), u["en_col"])
        mu = jnp.dot(hout.astype(BF16), avg_ref[...], preferred_element_type=F32)
        u["cen"] = hout - jnp.concatenate([mu] * (M_DH // LANES), axis=1)
        var = jnp.dot((u["cen"] * u["cen"]).astype(BF16), avg_ref[...],
                      preferred_element_type=F32)
        u["var"] = jnp.concatenate([var] * (M_DH // LANES), axis=1)

    for u in units:
        hn = u["cen"] * lax.rsqrt(u["var"] + EPS) * ng_ref[:, u["sl"]]
        y_ref[u["s"], :, u["sl"]] = (o_ref[u["s"], :, u["sl"]].astype(F32) * hn).astype(BF16)


def _mlstm(p3, gates, conv_w, conv_b, ng, nseq=4):
    B, S, _ = p3.shape
    nseq = nseq if B % nseq == 0 else 1
    wblk = lambda col: pl.BlockSpec((nseq, CHUNK, M_WIDTH), lambda b, c: (b, c, col // M_WIDTH))
    full = lambda shape: pl.BlockSpec(shape, lambda b, c: (0,) * len(shape))
    return pl.pallas_call(
        _mlstm_kernel,
        grid=(B // nseq, S // CHUNK),
        in_specs=[
            wblk(P_QM), wblk(P_KM), wblk(P_VM), wblk(P_OM),
            pl.BlockSpec((nseq, GATE_ROWS, CHUNK), lambda b, c: (b, 0, c)),
            full((CONV_K, 2 * M_WIDTH)), full((1, 2 * M_WIDTH)), full((1, M_WIDTH)),
            full((CHUNK, CONV_K * CHUNK)), full((M_DH, LANES)),
        ],
        out_specs=pl.BlockSpec((nseq, CHUNK, M_WIDTH), lambda b, c: (b, c, 0)),
        out_shape=jax.ShapeDtypeStruct((B, S, M_WIDTH), BF16),
        scratch_shapes=[
            pltpu.VMEM((nseq, M_HEADS, M_DH, M_DH), F32),
            pltpu.VMEM((nseq, SUBLANES, M_DH), F32),
            pltpu.VMEM((nseq, SUBLANES, M_WIDTH), F32),
            pltpu.VMEM((nseq, SUBLANES, M_WIDTH), F32),
        ],
        compiler_params=pltpu.CompilerParams(
            dimension_semantics=("parallel", "arbitrary"), vmem_limit_bytes=VMEM_LIMIT),
        name="mlstm",
    )(p3, p3, p3, p3, gates, conv_w, conv_b, ng,
      jnp.asarray(_conv_shift_matrix(CHUNK), BF16),
      jnp.full((M_DH, LANES), 1.0 / M_DH, BF16))


def _attn_kernel(q_ref, kc_ref, kp_ref, vc_ref, vp_ref, bias_ref, o_ref, lse_ref, *, tq):
    for r in range(q_ref.shape[0]):
        _attn_rows(q_ref.at[r], kc_ref.at[r], kp_ref.at[r], vc_ref.at[r], vp_ref.at[r], bias_ref,
                   o_ref.at[r], lse_ref.at[r], tq=tq)


def _attn_rows(q_ref, kc_ref, kp_ref, vc_ref, vp_ref, bias_ref, o_ref, lse_ref, *, tq):
    first = (pl.program_id(2) == 0).astype(jnp.int32)
    lane = lax.broadcasted_iota(jnp.int32, (1, LANES), 1)
    low = lane < A_DH
    nt_dims = (((1,), (1,)), ((), ()))
    n_pairs = A_HG // 2
    krow = lax.broadcasted_iota(jnp.int32, (4 * A_BLK, LANES), 0)
    ones_blk = jnp.where((krow < 2 * A_BLK) == low, 1.0, 0.0).astype(BF16)
    for i in range(tq // A_BLK):
        rows = slice(i * A_BLK, (i + 1) * A_BLK)
        if i == 0:
            k2 = jnp.concatenate([kp_ref[...], kc_ref[rows]], axis=0)
            v2 = jnp.concatenate([vp_ref[...], vc_ref[rows]], axis=0)
            var = first
        else:
            k2 = kc_ref[(i - 1) * A_BLK:(i + 1) * A_BLK]
            v2 = vc_ref[(i - 1) * A_BLK:(i + 1) * A_BLK]
            var = 0
        q = q_ref[rows]
        zero = jnp.zeros((A_BLK, LANES), BF16)

        scores = []
        for p in range(n_pairs):
            cs = slice(p * LANES, (p + 1) * LANES)
            qp = q[:, cs]
            q2 = jnp.concatenate([jnp.where(low, qp, zero), jnp.where(low, zero, qp)], axis=0)
            scores.append(lax.dot_general(q2, k2[:, cs], nt_dims, preferred_element_type=F32))

        probs, maxes = [], []
        for p in range(n_pairs):
            for e in range(2):
                s = scores[p][e * A_BLK:(e + 1) * A_BLK] + bias_ref[var, 2 * p + e]
                mx = jnp.max(s, axis=1, keepdims=True)
                probs.append(jnp.exp2(s - mx).astype(BF16))
                maxes.append(mx)

        lse_blk = jnp.zeros((A_BLK, LANES), F32)
        for p in range(n_pairs):
            cs = slice(p * LANES, (p + 1) * LANES)
            vp2 = v2[:, cs]
            zero2 = jnp.zeros_like(vp2)
            v_cat = jnp.concatenate([jnp.where(low, vp2, zero2), jnp.where(low, zero2, vp2)],
                                    axis=0)
            p_cat = jnp.concatenate([probs[2 * p], probs[2 * p + 1]], axis=1)
            o_ext = jnp.dot(p_cat, jnp.concatenate([v_cat, ones_blk], axis=1),
                            preferred_element_type=F32)
            l_pair = o_ext[:, LANES:]
            o_ref[rows, cs] = (o_ext[:, :LANES] * (1.0 / l_pair)).astype(BF16)
            mx_pair = jnp.where(low, maxes[2 * p], maxes[2 * p + 1])
            lse_pair = mx_pair * (1.0 / LOG2E) + jnp.log(l_pair)
            lse_blk = jnp.where(lane % A_DH == p, lse_pair, lse_blk)

        packed = jnp.zeros((A_BLK, LANES), F32)
        rest = lse_blk
        for part in range(LSE_PARTS):
            hi = rest.astype(BF16).astype(F32)
            rest = rest - hi
            packed = packed + (hi if part == 0 else pltpu.roll(hi, part * SUBLANES, 1))
        lse_ref[rows] = packed.astype(BF16)


def _attn_group(src, col0, bias, g, rows=1024):
    B, d, sd, _ = src.shape
    tq = min(rows, sd)
    nres = min(rows // tq, d)
    cur = lambda c: pl.BlockSpec((None, nres, tq, A_GW), lambda b, r, n: (b, r, n, col0 + c))
    prev = lambda c: pl.BlockSpec(
        (None, nres, A_BLK, A_GW),
        lambda b, r, n: (b, r, jnp.maximum(n * (tq // A_BLK) - 1, 0), col0 + c))
    return pl.pallas_call(
        functools.partial(_attn_kernel, tq=tq),
        grid=(B, d // nres, sd // tq),
        in_specs=[
            cur(0), cur(1), prev(1), cur(2), prev(2),
            pl.BlockSpec((None, 2, A_HG, A_BLK, 2 * A_BLK), lambda b, r, n: (g, 0, 0, 0, 0)),
        ],
        out_specs=[
            pl.BlockSpec((None, nres, tq, A_GW), lambda b, r, n: (b, r, n, 0)),
            pl.BlockSpec((None, nres, tq, LANES), lambda b, r, n: (b, r, n, 0)),
        ],
        out_shape=[
            jax.ShapeDtypeStruct((B, d, sd, A_GW), BF16),
            jax.ShapeDtypeStruct((B, d, sd, LANES), BF16),
        ],
        compiler_params=pltpu.CompilerParams(
            dimension_semantics=("parallel", "parallel", "arbitrary"),
            vmem_limit_bytes=VMEM_LIMIT),
        name=f"attn_g{g}",
    )(src, src, src, src, src, bias)


MERGE_RB = 512
UNPERM_RB = 256


def _merge_kernel(x_ref, gate_ref, ya_ref, *rest):
    o_refs = rest[:N_GROUPS]
    l_refs = rest[N_GROUPS:2 * N_GROUPS]
    unperm_refs = rest[2 * N_GROUPS:2 * N_GROUPS + N_DIL]
    e_ref, wa_ref, wb_ref, wo_ref, out_ref = rest[2 * N_GROUPS + N_DIL:]
    rb = MERGE_RB

    for c in range(x_ref.shape[0] // rb):
        rows = slice(c * rb, (c + 1) * rb)
        outs, lses = [], []
        for g in range(N_GROUPS):
            d = o_refs[g].shape[0]
            if g == 0:
                o = o_refs[g][:, c * rb:(c + 1) * rb].reshape(rb, A_GW).astype(F32)
                lp = l_refs[g][:, c * rb:(c + 1) * rb].reshape(rb, LANES).astype(F32)
            else:
                per = UNPERM_RB // d
                parts = []
                for cc in range(c * rb // UNPERM_RB, (c + 1) * rb // UNPERM_RB):
                    o = o_refs[g][:, cc * per:(cc + 1) * per].reshape(UNPERM_RB, A_GW)
                    lp = l_refs[g][:, cc * per:(cc + 1) * per].reshape(UNPERM_RB, LANES)
                    parts.append(jnp.dot(unperm_refs[g - 1][...],
                                         jnp.concatenate([o, lp], axis=1),
                                         preferred_element_type=F32))
                both = jnp.concatenate(parts, axis=0)
                o, lp = both[:, :A_GW], both[:, A_GW:]
            outs.append(o)
            lses.append(sum(lp if part == 0 else pltpu.roll(lp, LANES - part * SUBLANES, 1)
                            for part in range(LSE_PARTS)))

        lm = functools.reduce(jnp.maximum, lses)
        es = [jnp.exp(l - lm) for l in lses]
        den = functools.reduce(jnp.add, es)
        yb = jnp.zeros((rb, A_GW), F32)
        for e, o in zip(es, outs):
            wide = jnp.dot((e / den).astype(BF16), e_ref[...], preferred_element_type=F32)
            yb = yb + wide * o
        pa = jnp.dot(ya_ref[rows], wa_ref[...], preferred_element_type=F32)
        pb = jnp.dot(yb.astype(BF16), wb_ref[...], preferred_element_type=F32)
        ga = gate_ref[rows, :D_MODEL].astype(F32)
        gb = gate_ref[rows, D_MODEL:].astype(F32)
        merged = (ga * pa + gb * pb).astype(BF16)
        out_ref[rows] = x_ref[rows] + jnp.dot(merged, wo_ref[...], preferred_element_type=F32)


def _merge(x2, p_flat, ya, os_, ls_, wa, wb, wo, S, tm=1024):
    n = x2.shape[0]
    nt = S // tm
    row = lambda w: pl.BlockSpec((tm, w), lambda i: (i, 0))
    full = lambda a: pl.BlockSpec(a.shape, lambda i: (0, 0))
    grp = lambda a: pl.BlockSpec((None, a.shape[1], tm // a.shape[1], a.shape[3]),
                                 lambda i: (i // nt, 0, i % nt, 0))
    unperms = [jnp.asarray(_deinterleave_matrix(UNPERM_RB, d).T, BF16) for _, d in GROUPS[1:]]
    head_lane = (np.arange(A_HG) % 2) * A_DH + np.arange(A_HG) // 2
    expand = jnp.asarray(
        np.arange(LANES)[:, None] == head_lane[np.arange(A_GW) // A_DH][None, :], BF16)
    consts = unperms + [expand, wa, wb, wo]
    return pl.pallas_call(
        _merge_kernel,
        grid=(n // tm,),
        in_specs=[row(D_MODEL),
                  pl.BlockSpec((tm, 2 * D_MODEL), lambda i: (i, P_GATE // (2 * D_MODEL))),
                  row(M_WIDTH)]
                 + [grp(a) for a in os_] + [grp(a) for a in ls_] + [full(a) for a in consts],
        out_specs=row(D_MODEL),
        out_shape=jax.ShapeDtypeStruct((n, D_MODEL), F32),
        compiler_params=pltpu.CompilerParams(
            dimension_semantics=("parallel",), vmem_limit_bytes=VMEM_LIMIT),
        name="merge",
    )(x2, p_flat, ya, *os_, *ls_, *consts)


FF_CHUNKS = ((0, 1024), (1024, 1024), (2048, 768))
FF_RB = 256


def _rms(x, g):
    return x * lax.rsqrt(jnp.mean(x * x, axis=-1, keepdims=True) + EPS) * g


def _ffn_kernel(x_ref, gf_ref, wg_ref, wu_ref, wd_ref, gl_ref, out_ref):
    for rc in range(x_ref.shape[0] // FF_RB):
        rows = slice(rc * FF_RB, (rc + 1) * FF_RB)
        x = x_ref[rows]
        hf = _rms(x, gf_ref[...]).astype(BF16)
        acc = x
        for start, size in FF_CHUNKS:
            cs = slice(start, start + size)
            gt = jnp.dot(hf, wg_ref[:, cs], preferred_element_type=F32)
            up = jnp.dot(hf, wu_ref[:, cs], preferred_element_type=F32)
            act = (gt * jax.nn.sigmoid(gt) * up).astype(BF16)
            acc = acc + jnp.dot(act, wd_ref[cs, :], preferred_element_type=F32)
        out_ref[rows] = _rms(acc, gl_ref[...])


def _ffn(x1, gf, wg, wu, wd, gl, tm=512):
    n = x1.shape[0]
    row = pl.BlockSpec((tm, D_MODEL), lambda i: (i, 0))
    once = lambda a: pl.BlockSpec(a.shape, lambda i: (0, 0), pipeline_mode=pl.Buffered(1))
    return pl.pallas_call(
        _ffn_kernel,
        grid=(n // tm,),
        in_specs=[row, once(gf), once(wg), once(wu), once(wd), once(gl)],
        out_specs=row,
        out_shape=jax.ShapeDtypeStruct((n, D_MODEL), F32),
        compiler_params=pltpu.CompilerParams(
            dimension_semantics=("parallel",), vmem_limit_bytes=VMEM_LIMIT),
        name="ffn",
    )(x1, gf, wg, wu, wd, gl)


def _layer(x2, B, S, bias, norm_mix_g, w_in, b_gate_if, conv_w, conv_b, mlstm_norm_g,
           w_proj_a, w_proj_b, w_out, norm_ffn_g, w_gate, w_up, w_down, final_g):
    sizes = (M_WIDTH, M_WIDTH, M_WIDTH, 2 * M_HEADS, M_WIDTH, A_WIDTH, A_WIDTH, A_WIDTH,
             2 * D_MODEL)
    offs = np.concatenate([[0], np.cumsum(sizes)])
    piece = lambda i: w_in[:, offs[i]:offs[i + 1]]
    qkv = lambda g: [piece(i)[:, g * A_GW:(g + 1) * A_GW] * sc
                     for i, sc in ((5, QK_SCALE), (6, 1.0), (7, 1.0))]
    w_p = jnp.concatenate([piece(0), piece(1), piece(2), piece(4), piece(8)]
                          + [w for g in range(N_GROUPS) for w in qkv(g)], axis=1).astype(BF16).T
    w_if = jnp.concatenate(_split_bf16(piece(3).T, 2), axis=0)

    p_flat, if_t, *dilated = _inproj(x2, norm_mix_g[None], w_p, w_if, B, S)

    bif = jnp.broadcast_to(b_gate_if[:, None], (SUBLANES, LANES))
    gates = _gates(if_t, bif, B, S)
    ya = _mlstm(p_flat.reshape(B, S, P_WIDTH), gates, conv_w, conv_b[None],
                mlstm_norm_g[None]).reshape(B * S, M_WIDTH)

    os_, ls_ = [], []
    for g in range(N_GROUPS):
        if g == 0:
            o, lse = _attn_group(p_flat.reshape(B, 1, S, P_WIDTH), P_A0 // A_GW, bias, g)
        else:
            o, lse = _attn_group(dilated[g - 1], 0, bias, g)
        os_.append(o)
        ls_.append(lse)

    x1 = _merge(x2, p_flat, ya, os_, ls_, w_proj_a.astype(BF16), w_proj_b.astype(BF16),
                w_out.astype(BF16), S)
    return _ffn(x1, norm_ffn_g[None], w_gate.astype(BF16), w_up.astype(BF16),
                w_down.astype(BF16), final_g[None])


def kernel(x, norm_mix_g, w_in, b_gate_if, conv_w, conv_b, mlstm_norm_g, w_proj_a, w_proj_b,
           w_out, norm_ffn_g, w_gate, w_up, w_down, rel_bias, norm_final_g):
    B, S, _ = x.shape
    depth = w_in.shape[0]
    assert depth == 1, "the final norm is fused into the (single) layer's channel mixer"
    bias = _bias_tables(rel_bias)
    out = _layer(x.reshape(B * S, D_MODEL), B, S, bias, norm_mix_g[0], w_in[0], b_gate_if[0],
                 conv_w[0], conv_b[0], mlstm_norm_g[0], w_proj_a[0], w_proj_b[0], w_out[0],
                 norm_ffn_g[0], w_gate[0], w_up[0], w_down[0], norm_final_g)
    return out.reshape(B, S, D_MODEL)
```

```python
import functools
import math

import numpy as np
import jax
import jax.numpy as jnp
from jax import lax
from jax.experimental import pallas as pl
from jax.experimental.pallas import tpu as pltpu

F32 = jnp.float32
BF16 = jnp.bfloat16

D_MODEL = 1024
M_HEADS = 4
M_DH = 256
M_WIDTH = M_HEADS * M_DH
CONV_K = 4
CHUNK = 128
GROUPS = ((128, 1), (512, 4), (2048, 16))
N_GROUPS = len(GROUPS)
A_HG = 8
A_DH = 64
A_GW = A_HG * A_DH
A_WIDTH = N_GROUPS * A_GW
A_BLK = 128
N_BUCKETS = 32
MAX_DISTANCE = 2048
D_FF = 2816
EPS = 1e-6
NEG = -1e30
LOG2E = math.log2(math.e)
QK_SCALE = A_DH ** -0.5 * LOG2E
LSE_PARTS = 3

LANES = 128
SUBLANES = 8

P_QM = 0
P_KM = 1024
P_VM = 2048
P_OM = 3072
P_GATE = 4096
P_A0 = 6144
P_WIDTH = P_A0 + 3 * A_GW
IN_TN = 3 * A_GW
N_DIL = N_GROUPS - 1

VMEM_LIMIT = 56 * 1024 * 1024


def _split_bf16(x, parts):
    out = []
    for _ in range(parts):
        hi = x.astype(BF16)
        out.append(hi)
        x = x - hi.astype(F32)
    return out


def _bucket_tables():
    i = np.arange(A_BLK)[:, None]
    j = np.arange(2 * A_BLK)[None, :]
    dist = i + A_BLK - j
    buckets = []
    for window, dil in GROUPS:
        n = np.maximum(dist, 0) * dil
        nf = np.maximum(n, 1).astype(np.float32)
        max_exact = N_BUCKETS // 2
        large = max_exact + (np.log(nf / max_exact) / math.log(MAX_DISTANCE / max_exact)
                             * (N_BUCKETS - max_exact)).astype(np.int32)
        large = np.minimum(large, N_BUCKETS - 1)
        buckets.append(np.where(n < max_exact, n, large).astype(np.int32))
    span = GROUPS[0][0] // GROUPS[0][1]
    assert all(w // d == span for w, d in GROUPS)
    valid = ((dist >= 0) & (dist <= span)).astype(np.int32)
    valid_first = (valid.astype(bool) & (j >= A_BLK)).astype(np.int32)
    return np.stack(buckets), np.stack([valid, valid_first])


def _bias_kernel(tab_ref, bucket_ref, valid_ref, out_ref):
    g = pl.program_id(0)
    bucket = bucket_ref[...]
    for h in range(A_HG):
        acc = jnp.zeros(bucket.shape, F32)
        for b in range(N_BUCKETS):
            acc = jnp.where(bucket == b, tab_ref[b, g * A_HG + h], acc)
        acc = acc * LOG2E
        out_ref[0, h] = jnp.where(valid_ref[0] > 0, acc, NEG)
        out_ref[1, h] = jnp.where(valid_ref[1] > 0, acc, NEG)


def _bias_tables(rel_bias):
    buckets, valid = _bucket_tables()
    return pl.pallas_call(
        _bias_kernel,
        grid=(N_GROUPS,),
        in_specs=[
            pl.BlockSpec(memory_space=pltpu.SMEM),
            pl.BlockSpec((None, A_BLK, 2 * A_BLK), lambda g: (g, 0, 0)),
            pl.BlockSpec((2, A_BLK, 2 * A_BLK), lambda g: (0, 0, 0)),
        ],
        out_specs=pl.BlockSpec((None, 2, A_HG, A_BLK, 2 * A_BLK), lambda g: (g, 0, 0, 0, 0)),
        out_shape=jax.ShapeDtypeStruct((N_GROUPS, 2, A_HG, A_BLK, 2 * A_BLK), F32),
        name="bias_tables",
    )(rel_bias, jnp.asarray(buckets), jnp.asarray(valid))


def _deinterleave_matrix(rows, d):
    out = np.arange(rows)
    src = d * (out % (rows // d)) + out // (rows // d)
    return (src[:, None] == np.arange(rows)[None, :]).astype(np.float32)


PERM_BLK = 256
NORM_RB = 256


def _inproj_kernel(x_ref, g_ref, w_ref, wif_ref, *rest, n_nat, dils):
    perm_refs = rest[:N_DIL]
    p_ref, if_ref = rest[N_DIL:N_DIL + 2]
    a_refs = rest[N_DIL + 2:2 * N_DIL + 2]
    h_ref, hp_ref = rest[-2:]
    tm = x_ref.shape[0]
    j = pl.program_id(1)
    nt_dims = (((1,), (1,)), ((), ()))

    @pl.when(j == 0)
    def _():
        for rc in range(tm // NORM_RB):
            rows = slice(rc * NORM_RB, (rc + 1) * NORM_RB)
            x = x_ref[rows]
            r = lax.rsqrt(jnp.mean(x * x, axis=-1, keepdims=True) + EPS)
            h = x * r * g_ref[...]
            h_hi, h_lo = _split_bf16(h, 2)
            h_ref[rows] = h_hi
            acc = (lax.dot_general(wif_ref[...], h_hi, nt_dims, preferred_element_type=F32)
                   + lax.dot_general(wif_ref[...], h_lo, nt_dims, preferred_element_type=F32))
            if_ref[:, rows] = acc[:SUBLANES] + acc[SUBLANES:]
            acc = lax.dot_general(h_hi, w_ref[...], nt_dims, preferred_element_type=F32)
            p_ref[rows] = acc.astype(BF16)

    is_gate = (j >= P_OM // IN_TN) & (j < P_A0 // IN_TN)

    @pl.when((j > 0) & (j < n_nat) & jnp.logical_not(is_gate))
    def _():
        acc = lax.dot_general(h_ref[...], w_ref[...], nt_dims, preferred_element_type=F32)
        p_ref[...] = acc.astype(BF16)

    @pl.when(is_gate)
    def _():
        acc = lax.dot_general(h_ref[...], w_ref[...], nt_dims, preferred_element_type=F32)
        p_ref[...] = (0.5 * jnp.tanh(0.5 * acc) + 0.5).astype(BF16)

    for idx, d in enumerate(dils):
        @pl.when(j == n_nat + idx)
        def _(perm_ref=perm_refs[idx], a_ref=a_refs[idx], d=d):
            piece = PERM_BLK // d
            for c in range(tm // PERM_BLK):
                hp = jnp.dot(perm_ref[...], h_ref[c * PERM_BLK:(c + 1) * PERM_BLK],
                             preferred_element_type=F32).astype(BF16)
                for r in range(d):
                    dst = r * (tm // d) + c * piece
                    hp_ref[dst:dst + piece] = hp[r * piece:(r + 1) * piece]
            a = lax.dot_general(hp_ref[...], w_ref[...], nt_dims, preferred_element_type=F32)
            a_ref[...] = a.astype(BF16).reshape(a_ref.shape)


def _inproj(x2, g, w_p, w_if, B, S, tm=1024):
    n = x2.shape[0]
    nt = S // tm
    n_nat = P_WIDTH // IN_TN
    dils = tuple(d for _, d in GROUPS[1:])
    perms = [jnp.asarray(_deinterleave_matrix(PERM_BLK, d), BF16) for d in dils]
    return pl.pallas_call(
        functools.partial(_inproj_kernel, n_nat=n_nat, dils=dils),
        grid=(n // tm, n_nat + N_DIL),
        in_specs=[
            pl.BlockSpec((tm, D_MODEL), lambda i, j: (i, 0)),
            pl.BlockSpec((1, D_MODEL), lambda i, j: (0, 0)),
            pl.BlockSpec((IN_TN, D_MODEL), lambda i, j: (j, 0)),
            pl.BlockSpec((2 * SUBLANES, D_MODEL), lambda i, j: (0, 0)),
        ] + [pl.BlockSpec((PERM_BLK, PERM_BLK), lambda i, j: (0, 0)) for _ in dils],
        out_specs=[
            pl.BlockSpec((tm, IN_TN), lambda i, j: (i, jnp.minimum(j, n_nat - 1))),
            pl.BlockSpec((SUBLANES, tm), lambda i, j: (0, i)),
        ] + [pl.BlockSpec((None, d, tm // d, IN_TN), lambda i, j: (i // nt, 0, i % nt, 0))
             for d in dils],
        out_shape=[
            jax.ShapeDtypeStruct((n, P_WIDTH), BF16),
            jax.ShapeDtypeStruct((SUBLANES, n), F32),
        ] + [jax.ShapeDtypeStruct((B, d, S // d, IN_TN), BF16) for d in dils],
        scratch_shapes=[pltpu.VMEM((tm, D_MODEL), BF16), pltpu.VMEM((tm, D_MODEL), BF16)],
        compiler_params=pltpu.CompilerParams(
            dimension_semantics=("parallel", "arbitrary"), vmem_limit_bytes=VMEM_LIMIT),
        name="inproj",
    )(x2, g, w_p, w_if, *perms)


def _conv_shift_matrix(L):
    return np.concatenate([np.eye(L, k=-(CONV_K - 1 - j)) for j in range(CONV_K)], axis=1)


def _conv_silu(x_ref, tail_ref, shift_ref, cw, cb):
    L, C = x_ref.shape
    pack = 2 * SUBLANES
    x3 = x_ref[...].reshape(L // pack, pack, C)
    prods = []
    for j in range(CONV_K):
        wj = jnp.broadcast_to(cw[j:j + 1], (pack, C)).astype(BF16)
        prods.append((x3 * wj[None]).reshape(L, C))
    y = jnp.dot(shift_ref[...], jnp.concatenate(prods, axis=0),
                preferred_element_type=F32) + cb
    tail = tail_ref[...]
    row = lax.broadcasted_iota(jnp.int32, tail.shape, 0)
    fix = jnp.zeros(tail.shape, F32)
    for k in range(1, CONV_K):
        tap = pltpu.roll(tail, k, 0) * cw[CONV_K - 1 - k:CONV_K - k]
        fix = fix + jnp.where(row < k, tap, 0.0)
    y = jnp.concatenate([y[:SUBLANES] + fix, y[SUBLANES:]], axis=0)
    tail_ref[...] = x_ref[L - pack:].astype(F32)[pack - SUBLANES:]
    return y * jax.nn.sigmoid(y)


GATE_ROWS = 32


def _gates_kernel(ift_ref, bif_ref, out_ref):
    nseq, _, S = out_ref.shape
    nc = S // CHUNK
    g = ift_ref[...] + jnp.concatenate([bif_ref[...]] * (nseq * nc), axis=1)
    ig = g[:M_HEADS]
    lf = jax.nn.log_sigmoid(g[M_HEADS:])
    pos = lax.broadcasted_iota(jnp.int32, lf.shape, 1) % CHUNK

    def scan(x, op, fill):
        sh = 1
        while sh < CHUNK:
            x = op(x, jnp.where(pos >= sh, pltpu.roll(x, sh, 1), fill))
            sh *= 2
        return x

    def last(x):
        x = jnp.where(pos == CHUNK - 1, x, NEG)
        sh = 1
        while sh < CHUNK:
            x = jnp.maximum(x, jnp.where(pos < CHUNK - sh, pltpu.roll(x, nseq * S - sh, 1), NEG))
            sh *= 2
        return x

    b = scan(lf, jnp.add, 0.0)
    a = ig - b
    b_last = last(b)
    a_end = b_last + a
    a_max = last(scan(a_end, jnp.maximum, NEG))
    m_prev = []
    for c in range(nseq * nc):
        if c % nc == 0:
            m = jnp.zeros((M_HEADS, CHUNK), F32)
        m_prev.append(m)
        m = jnp.maximum(b_last[:, c * CHUNK:(c + 1) * CHUNK] + m,
                        a_max[:, c * CHUNK:(c + 1) * CHUNK])
    m_prev = jnp.concatenate(m_prev, axis=1)
    m_new = jnp.maximum(b_last + m_prev, a_max)
    mx = jnp.maximum(m_prev, scan(a, jnp.maximum, NEG))
    rows = jnp.concatenate(
        [-mx, jnp.exp(m_prev - mx), jnp.exp(-(b + mx)), jnp.exp(a_end - m_new), a,
         jnp.exp(b_last + m_prev - m_new),
         jnp.zeros((GATE_ROWS - 6 * M_HEADS, nseq * S), F32)], axis=0)
    for s in range(nseq):
        out_ref[s] = rows[:, s * S:(s + 1) * S]


def _gates(ift, bif, B, S, nseq=4):
    nseq = nseq if B % nseq == 0 else 1
    return pl.pallas_call(
        _gates_kernel,
        grid=(B // nseq,),
        in_specs=[pl.BlockSpec((SUBLANES, nseq * S), lambda b: (0, b)),
                  pl.BlockSpec((SUBLANES, LANES), lambda b: (0, 0))],
        out_specs=pl.BlockSpec((nseq, GATE_ROWS, S), lambda b: (b, 0, 0)),
        out_shape=jax.ShapeDtypeStruct((B, GATE_ROWS, S), F32),
        compiler_params=pltpu.CompilerParams(dimension_semantics=("parallel",)),
        name="gates",
    )(ift, bif)


def _mlstm_kernel(q_ref, k_ref, v_ref, o_ref, g_ref, cw_ref, cb_ref, ng_ref, shift_ref,
                  avg_ref, y_ref, c_ref, n_ref, qt_ref, kt_ref):
    L = CHUNK
    nt_dims = (((1,), (1,)), ((), ()))

    @pl.when(pl.program_id(1) == 0)
    def _():
        c_ref[...] = jnp.zeros_like(c_ref)
        n_ref[...] = jnp.zeros_like(n_ref)
        qt_ref[...] = jnp.zeros_like(qt_ref)
        kt_ref[...] = jnp.zeros_like(kt_ref)

    cw = cw_ref[...]
    cb = cb_ref[...]
    tri = (lax.broadcasted_iota(jnp.int32, (L, L), 0)
           >= lax.broadcasted_iota(jnp.int32, (L, L), 1))

    units = []
    for s in range(q_ref.shape[0]):
        q_all = _conv_silu(q_ref.at[s], qt_ref.at[s], shift_ref, cw[:, :M_WIDTH], cb[:, :M_WIDTH])
        q_all = (q_all * (M_DH ** -0.5)).astype(BF16)
        k_all = _conv_silu(k_ref.at[s], kt_ref.at[s], shift_ref, cw[:, M_WIDTH:], cb[:, M_WIDTH:])
        rows = g_ref[s]
        cols = jnp.transpose(rows)
        for h in range(M_HEADS):
            sl = slice(h * M_DH, (h + 1) * M_DH)
            col = lambda i: cols[:, i * M_HEADS + h:i * M_HEADS + h + 1]
            units.append(dict(
                s=s, h=h, sl=sl, qb=q_all[:, sl], k=k_all[:, sl], vb=v_ref[s, :, sl],
                u_col=col(0), wi_col=col(1), en_col=col(2), wk_col=col(3),
                a_row=rows[4 * M_HEADS + h:4 * M_HEADS + h + 1],
                dec_row=rows[5 * M_HEADS + h:5 * M_HEADS + h + 1]))

    for u in units:
        c_old = c_ref[u["s"], u["h"]]
        n_old = n_ref[u["s"], u["h"]:u["h"] + 1]
        kb = u["k"].astype(BF16)
        kn = jnp.concatenate([kb, jnp.broadcast_to(n_old, (L, M_DH)).astype(BF16)], axis=0)
        u["s_aug"] = lax.dot_general(u["qb"], kn, nt_dims, preferred_element_type=F32)
        u["qc"] = jnp.dot(u["qb"], c_old.astype(BF16), preferred_element_type=F32)
        kw = u["k"] * u["wk_col"]
        u["kw"] = kw.astype(BF16)
        dec = jnp.concatenate([u["dec_row"], u["dec_row"]], axis=1)
        u["c_dec"] = dec * c_old
        n_ref[u["s"], u["h"]:u["h"] + 1] = dec * n_old + jnp.sum(kw, axis=0, keepdims=True)

    for u in units:
        dmat = jnp.where(tri, jnp.exp(u["u_col"] + u["a_row"]), 0.0)
        w_intra = dmat * u["s_aug"][:, :L]
        u["w_sum"] = jnp.sum(w_intra, axis=1, keepdims=True)
        u["w_intra"] = w_intra.astype(BF16)

    for u in units:
        u["pv"] = jnp.dot(u["w_intra"], u["vb"], preferred_element_type=F32)
        c_ref[u["s"], u["h"]] = u["c_dec"] + lax.dot_general(
            u["kw"], u["vb"], (((0,), (0,)), ((), ())), preferred_element_type=F32)

    for u in units:
        num = u["wi_col"] * u["qc"] + u["pv"]
        den = u["wi_col"] * u["s_aug"][:, L:L + 1] + u["w_sum"]
        hout = num / jnp.maximum(jnp.abs(den), u["en_col"])
        mu = jnp.dot(hout.astype(BF16), avg_ref[...], preferred_element_type=F32)
        u["cen"] = hout - jnp.concatenate([mu] * (M_DH // LANES), axis=1)
        var = jnp.dot((u["cen"] * u["cen"]).astype(BF16), avg_ref[...],
                      preferred_element_type=F32)
        u["var"] = jnp.concatenate([var] * (M_DH // LANES), axis=1)

    for u in units:
        hn = u["cen"] * lax.rsqrt(u["var"] + EPS) * ng_ref[:, u["sl"]]
        y_ref[u["s"], :, u["sl"]] = (o_ref[u["s"], :, u["sl"]].astype(F32) * hn).astype(BF16)


def _mlstm(p3, gates, conv_w, conv_b, ng, nseq=4):
    B, S, _ = p3.shape
    nseq = nseq if B % nseq == 0 else 1
    wblk = lambda col: pl.BlockSpec((nseq, CHUNK, M_WIDTH), lambda b, c: (b, c, col // M_WIDTH))
    full = lambda shape: pl.BlockSpec(shape, lambda b, c: (0,) * len(shape))
    return pl.pallas_call(
        _mlstm_kernel,
        grid=(B // nseq, S // CHUNK),
        in_specs=[
            wblk(P_QM), wblk(P_KM), wblk(P_VM), wblk(P_OM),
            pl.BlockSpec((nseq, GATE_ROWS, CHUNK), lambda b, c: (b, 0, c)),
            full((CONV_K, 2 * M_WIDTH)), full((1, 2 * M_WIDTH)), full((1, M_WIDTH)),
            full((CHUNK, CONV_K * CHUNK)), full((M_DH, LANES)),
        ],
        out_specs=pl.BlockSpec((nseq, CHUNK, M_WIDTH), lambda b, c: (b, c, 0)),
        out_shape=jax.ShapeDtypeStruct((B, S, M_WIDTH), BF16),
        scratch_shapes=[
            pltpu.VMEM((nseq, M_HEADS, M_DH, M_DH), F32),
            pltpu.VMEM((nseq, SUBLANES, M_DH), F32),
            pltpu.VMEM((nseq, SUBLANES, M_WIDTH), F32),
            pltpu.VMEM((nseq, SUBLANES, M_WIDTH), F32),
        ],
        compiler_params=pltpu.CompilerParams(
            dimension_semantics=("parallel", "arbitrary"), vmem_limit_bytes=VMEM_LIMIT),
        name="mlstm",
    )(p3, p3, p3, p3, gates, conv_w, conv_b, ng,
      jnp.asarray(_conv_shift_matrix(CHUNK), BF16),
      jnp.full((M_DH, LANES), 1.0 / M_DH, BF16))


def _attn_kernel(q_ref, kc_ref, kp_ref, vc_ref, vp_ref, bias_ref, o_ref, lse_ref, *, tq):
    for r in range(q_ref.shape[0]):
        _attn_rows(q_ref.at[r], kc_ref.at[r], kp_ref.at[r], vc_ref.at[r], vp_ref.at[r], bias_ref,
                   o_ref.at[r], lse_ref.at[r], tq=tq)


def _attn_rows(q_ref, kc_ref, kp_ref, vc_ref, vp_ref, bias_ref, o_ref, lse_ref, *, tq):
    first = (pl.program_id(2) == 0).astype(jnp.int32)
    lane = lax.broadcasted_iota(jnp.int32, (1, LANES), 1)
    low = lane < A_DH
    nt_dims = (((1,), (1,)), ((), ()))
    n_pairs = A_HG // 2
    krow = lax.broadcasted_iota(jnp.int32, (4 * A_BLK, LANES), 0)
    ones_blk = jnp.where((krow < 2 * A_BLK) == low, 1.0, 0.0).astype(BF16)
    for i in range(tq // A_BLK):
        rows = slice(i * A_BLK, (i + 1) * A_BLK)
        if i == 0:
            k2 = jnp.concatenate([kp_ref[...], kc_ref[rows]], axis=0)
            v2 = jnp.concatenate([vp_ref[...], vc_ref[rows]], axis=0)
            var = first
        else:
            k2 = kc_ref[(i - 1) * A_BLK:(i + 1) * A_BLK]
            v2 = vc_ref[(i - 1) * A_BLK:(i + 1) * A_BLK]
            var = 0
        q = q_ref[rows]
        zero = jnp.zeros((A_BLK, LANES), BF16)

        scores = []
        for p in range(n_pairs):
            cs = slice(p * LANES, (p + 1) * LANES)
            qp = q[:, cs]
            q2 = jnp.concatenate([jnp.where(low, qp, zero), jnp.where(low, zero, qp)], axis=0)
            scores.append(lax.dot_general(q2, k2[:, cs], nt_dims, preferred_element_type=F32))

        probs, maxes = [], []
        for p in range(n_pairs):
            for e in range(2):
                s = scores[p][e * A_BLK:(e + 1) * A_BLK] + bias_ref[var, 2 * p + e]
                mx = jnp.max(s, axis=1, keepdims=True)
                probs.append(jnp.exp2(s - mx).astype(BF16))
                maxes.append(mx)

        lse_blk = jnp.zeros((A_BLK, LANES), F32)
        for p in range(n_pairs):
            cs = slice(p * LANES, (p + 1) * LANES)
            vp2 = v2[:, cs]
            zero2 = jnp.zeros_like(vp2)
            v_cat = jnp.concatenate([jnp.where(low, vp2, zero2), jnp.where(low, zero2, vp2)],
                                    axis=0)
            p_cat = jnp.concatenate([probs[2 * p], probs[2 * p + 1]], axis=1)
            o_ext = jnp.dot(p_cat, jnp.concatenate([v_cat, ones_blk], axis=1),
                            preferred_element_type=F32)
            l_pair = o_ext[:, LANES:]
            o_ref[rows, cs] = (o_ext[:, :LANES] * (1.0 / l_pair)).astype(BF16)
            mx_pair = jnp.where(low, maxes[2 * p], maxes[2 * p + 1])
            lse_pair = mx_pair * (1.0 / LOG2E) + jnp.log(l_pair)
            lse_blk = jnp.where(lane % A_DH == p, lse_pair, lse_blk)

        packed = jnp.zeros((A_BLK, LANES), F32)
        rest = lse_blk
        for part in range(LSE_PARTS):
            hi = rest.astype(BF16).astype(F32)
            rest = rest - hi
            packed = packed + (hi if part == 0 else pltpu.roll(hi, part * SUBLANES, 1))
        lse_ref[rows] = packed.astype(BF16)


def _attn_group(src, col0, bias, g, rows=1024):
    B, d, sd, _ = src.shape
    tq = min(rows, sd)
    nres = min(rows // tq, d)
    cur = lambda c: pl.BlockSpec((None, nres, tq, A_GW), lambda b, r, n: (b, r, n, col0 + c))
    prev = lambda c: pl.BlockSpec(
        (None, nres, A_BLK, A_GW),
        lambda b, r, n: (b, r, jnp.maximum(n * (tq // A_BLK) - 1, 0), col0 + c))
    return pl.pallas_call(
        functools.partial(_attn_kernel, tq=tq),
        grid=(B, d // nres, sd // tq),
        in_specs=[
            cur(0), cur(1), prev(1), cur(2), prev(2),
            pl.BlockSpec((None, 2, A_HG, A_BLK, 2 * A_BLK), lambda b, r, n: (g, 0, 0, 0, 0)),
        ],
        out_specs=[
            pl.BlockSpec((None, nres, tq, A_GW), lambda b, r, n: (b, r, n, 0)),
            pl.BlockSpec((None, nres, tq, LANES), lambda b, r, n: (b, r, n, 0)),
        ],
        out_shape=[
            jax.ShapeDtypeStruct((B, d, sd, A_GW), BF16),
            jax.ShapeDtypeStruct((B, d, sd, LANES), BF16),
        ],
        compiler_params=pltpu.CompilerParams(
            dimension_semantics=("parallel", "parallel", "arbitrary"),
            vmem_limit_bytes=VMEM_LIMIT),
        name=f"attn_g{g}",
    )(src, src, src, src, src, bias)


MERGE_RB = 512
UNPERM_RB = 256


def _merge_kernel(x_ref, gate_ref, ya_ref, *rest):
    o_refs = rest[:N_GROUPS]
    l_refs = rest[N_GROUPS:2 * N_GROUPS]
    unperm_refs = rest[2 * N_GROUPS:2 * N_GROUPS + N_DIL]
    e_ref, wa_ref, wb_ref, wo_ref, out_ref = rest[2 * N_GROUPS + N_DIL:]
    rb = MERGE_RB

    for c in range(x_ref.shape[0] // rb):
        rows = slice(c * rb, (c + 1) * rb)
        outs, lses = [], []
        for g in range(N_GROUPS):
            d = o_refs[g].shape[0]
            if g == 0:
                o = o_refs[g][:, c * rb:(c + 1) * rb].reshape(rb, A_GW).astype(F32)
                lp = l_refs[g][:, c * rb:(c + 1) * rb].reshape(rb, LANES).astype(F32)
            else:
                per = UNPERM_RB // d
                parts = []
                for cc in range(c * rb // UNPERM_RB, (c + 1) * rb // UNPERM_RB):
                    o = o_refs[g][:, cc * per:(cc + 1) * per].reshape(UNPERM_RB, A_GW)
                    lp = l_refs[g][:, cc * per:(cc + 1) * per].reshape(UNPERM_RB, LANES)
                    parts.append(jnp.dot(unperm_refs[g - 1][...],
                                         jnp.concatenate([o, lp], axis=1),
                                         preferred_element_type=F32))
                both = jnp.concatenate(parts, axis=0)
                o, lp = both[:, :A_GW], both[:, A_GW:]
            outs.append(o)
            lses.append(sum(lp if part == 0 else pltpu.roll(lp, LANES - part * SUBLANES, 1)
                            for part in range(LSE_PARTS)))

        lm = functools.reduce(jnp.maximum, lses)
        es = [jnp.exp(l - lm) for l in lses]
        den = functools.reduce(jnp.add, es)
        yb = jnp.zeros((rb, A_GW), F32)
        for e, o in zip(es, outs):
            wide = jnp.dot((e / den).astype(BF16), e_ref[...], preferred_element_type=F32)
            yb = yb + wide * o
        pa = jnp.dot(ya_ref[rows], wa_ref[...], preferred_element_type=F32)
        pb = jnp.dot(yb.astype(BF16), wb_ref[...], preferred_element_type=F32)
        ga = gate_ref[rows, :D_MODEL].astype(F32)
        gb = gate_ref[rows, D_MODEL:].astype(F32)
        merged = (ga * pa + gb * pb).astype(BF16)
        out_ref[rows] = x_ref[rows] + jnp.dot(merged, wo_ref[...], preferred_element_type=F32)


def _merge(x2, p_flat, ya, os_, ls_, wa, wb, wo, S, tm=1024):
    n = x2.shape[0]
    nt = S // tm
    row = lambda w: pl.BlockSpec((tm, w), lambda i: (i, 0))
    full = lambda a: pl.BlockSpec(a.shape, lambda i: (0, 0))
    grp = lambda a: pl.BlockSpec((None, a.shape[1], tm // a.shape[1], a.shape[3]),
                                 lambda i: (i // nt, 0, i % nt, 0))
    unperms = [jnp.asarray(_deinterleave_matrix(UNPERM_RB, d).T, BF16) for _, d in GROUPS[1:]]
    head_lane = (np.arange(A_HG) % 2) * A_DH + np.arange(A_HG) // 2
    expand = jnp.asarray(
        np.arange(LANES)[:, None] == head_lane[np.arange(A_GW) // A_DH][None, :], BF16)
    consts = unperms + [expand, wa, wb, wo]
    return pl.pallas_call(
        _merge_kernel,
        grid=(n // tm,),
        in_specs=[row(D_MODEL),
                  pl.BlockSpec((tm, 2 * D_MODEL), lambda i: (i, P_GATE // (2 * D_MODEL))),
                  row(M_WIDTH)]
                 + [grp(a) for a in os_] + [grp(a) for a in ls_] + [full(a) for a in consts],
        out_specs=row(D_MODEL),
        out_shape=jax.ShapeDtypeStruct((n, D_MODEL), F32),
        compiler_params=pltpu.CompilerParams(
            dimension_semantics=("parallel",), vmem_limit_bytes=VMEM_LIMIT),
        name="merge",
    )(x2, p_flat, ya, *os_, *ls_, *consts)


FF_CHUNKS = ((0, 1024), (1024, 1024), (2048, 768))
FF_RB = 256


def _rms(x, g):
    return x * lax.rsqrt(jnp.mean(x * x, axis=-1, keepdims=True) + EPS) * g


def _ffn_kernel(x_ref, gf_ref, wg_ref, wu_ref, wd_ref, gl_ref, out_ref):
    for rc in range(x_ref.shape[0] // FF_RB):
        rows = slice(rc * FF_RB, (rc + 1) * FF_RB)
        x = x_ref[rows]
        hf = _rms(x, gf_ref[...]).astype(BF16)
        acc = x
        for start, size in FF_CHUNKS:
            cs = slice(start, start + size)
            gt = jnp.dot(hf, wg_ref[:, cs], preferred_element_type=F32)
            up = jnp.dot(hf, wu_ref[:, cs], preferred_element_type=F32)
            act = (gt * jax.nn.sigmoid(gt) * up).astype(BF16)
            acc = acc + jnp.dot(act, wd_ref[cs, :], preferred_element_type=F32)
        out_ref[rows] = _rms(acc, gl_ref[...])


def _ffn(x1, gf, wg, wu, wd, gl, tm=512):
    n = x1.shape[0]
    row = pl.BlockSpec((tm, D_MODEL), lambda i: (i, 0))
    once = lambda a: pl.BlockSpec(a.shape, lambda i: (0, 0), pipeline_mode=pl.Buffered(1))
    return pl.pallas_call(
        _ffn_kernel,
        grid=(n // tm,),
        in_specs=[row, once(gf), once(wg), once(wu), once(wd), once(gl)],
        out_specs=row,
        out_shape=jax.ShapeDtypeStruct((n, D_MODEL), F32),
        compiler_params=pltpu.CompilerParams(
            dimension_semantics=("parallel",), vmem_limit_bytes=VMEM_LIMIT),
        name="ffn",
    )(x1, gf, wg, wu, wd, gl)


def _layer(x2, B, S, bias, norm_mix_g, w_in, b_gate_if, conv_w, conv_b, mlstm_norm_g,
           w_proj_a, w_proj_b, w_out, norm_ffn_g, w_gate, w_up, w_down, final_g):
    sizes = (M_WIDTH, M_WIDTH, M_WIDTH, 2 * M_HEADS, M_WIDTH, A_WIDTH, A_WIDTH, A_WIDTH,
             2 * D_MODEL)
    offs = np.concatenate([[0], np.cumsum(sizes)])
    piece = lambda i: w_in[:, offs[i]:offs[i + 1]]
    w_attn = w_in[:, offs[5]:offs[8]].reshape(D_MODEL, 3, N_GROUPS, A_GW)
    w_attn = w_attn * jnp.asarray([QK_SCALE, 1.0, 1.0], F32)[:, None, None]
    w_attn = w_attn.transpose(0, 2, 1, 3).reshape(D_MODEL, N_GROUPS * IN_TN)
    w_p = jnp.concatenate([w_in[:, :offs[3]], piece(4), piece(8), w_attn],
                          axis=1).astype(BF16).T
    w_if = jnp.concatenate(_split_bf16(piece(3).T, 2), axis=0)

    p_flat, if_t, *dilated = _inproj(x2, norm_mix_g[None], w_p, w_if, B, S)

    bif = jnp.broadcast_to(b_gate_if[:, None], (SUBLANES, LANES))
    gates = _gates(if_t, bif, B, S)
    ya = _mlstm(p_flat.reshape(B, S, P_WIDTH), gates, conv_w, conv_b[None],
                mlstm_norm_g[None]).reshape(B * S, M_WIDTH)

    os_, ls_ = [], []
    for g in range(N_GROUPS):
        if g == 0:
            o, lse = _attn_group(p_flat.reshape(B, 1, S, P_WIDTH), P_A0 // A_GW, bias, g)
        else:
            o, lse = _attn_group(dilated[g - 1], 0, bias, g)
        os_.append(o)
        ls_.append(lse)

    x1 = _merge(x2, p_flat, ya, os_, ls_, w_proj_a.astype(BF16), w_proj_b.astype(BF16),
                w_out.astype(BF16), S)
    return _ffn(x1, norm_ffn_g[None], w_gate.astype(BF16), w_up.astype(BF16),
                w_down.astype(BF16), final_g[None])


def kernel(x, norm_mix_g, w_in, b_gate_if, conv_w, conv_b, mlstm_norm_g, w_proj_a, w_proj_b,
           w_out, norm_ffn_g, w_gate, w_up, w_down, rel_bias, norm_final_g):
    B, S, _ = x.shape
    depth = w_in.shape[0]
    assert depth == 1, "the final norm is fused into the (single) layer's channel mixer"
    bias = _bias_tables(rel_bias)
    out = _layer(x.reshape(B * S, D_MODEL), B, S, bias, norm_mix_g[0], w_in[0], b_gate_if[0],
                 conv_w[0], conv_b[0], mlstm_norm_g[0], w_proj_a[0], w_proj_b[0], w_out[0],
                 norm_ffn_g[0], w_gate[0], w_up[0], w_down[0], norm_final_g)
    return out.reshape(B, S, D_MODEL)
```

```python
import functools
import math

import numpy as np
import jax
import jax.numpy as jnp
from jax import lax
from jax.experimental import pallas as pl
from jax.experimental.pallas import tpu as pltpu

F32 = jnp.float32
BF16 = jnp.bfloat16

D_MODEL = 1024
M_HEADS = 4
M_DH = 256
M_WIDTH = M_HEADS * M_DH
CONV_K = 4
CHUNK = 128
GROUPS = ((128, 1), (512, 4), (2048, 16))
N_GROUPS = len(GROUPS)
A_HG = 8
A_DH = 64
A_GW = A_HG * A_DH
A_WIDTH = N_GROUPS * A_GW
A_BLK = 128
N_BUCKETS = 32
MAX_DISTANCE = 2048
D_FF = 2816
EPS = 1e-6
NEG = -1e30
LOG2E = math.log2(math.e)
QK_SCALE = A_DH ** -0.5 * LOG2E
LSE_PARTS = 3

LANES = 128
SUBLANES = 8

P_QM = 0
P_KM = 1024
P_VM = 2048
P_OM = 3072
P_GATE = 4096
P_A0 = 6144
P_WIDTH = P_A0 + 3 * A_GW
IN_TN = 3 * A_GW
N_DIL = N_GROUPS - 1

VMEM_LIMIT = 56 * 1024 * 1024


def _split_bf16(x, parts):
    out = []
    for _ in range(parts):
        hi = x.astype(BF16)
        out.append(hi)
        x = x - hi.astype(F32)
    return out


def _bucket_tables():
    i = np.arange(A_BLK)[:, None]
    j = np.arange(2 * A_BLK)[None, :]
    dist = i + A_BLK - j
    buckets = []
    for window, dil in GROUPS:
        n = np.maximum(dist, 0) * dil
        nf = np.maximum(n, 1).astype(np.float32)
        max_exact = N_BUCKETS // 2
        large = max_exact + (np.log(nf / max_exact) / math.log(MAX_DISTANCE / max_exact)
                             * (N_BUCKETS - max_exact)).astype(np.int32)
        large = np.minimum(large, N_BUCKETS - 1)
        buckets.append(np.where(n < max_exact, n, large).astype(np.int32))
    span = GROUPS[0][0] // GROUPS[0][1]
    assert all(w // d == span for w, d in GROUPS)
    valid = ((dist >= 0) & (dist <= span)).astype(np.int32)
    valid_first = (valid.astype(bool) & (j >= A_BLK)).astype(np.int32)
    return np.stack(buckets), np.stack([valid, valid_first])


def _bias_kernel(tab_ref, bucket_ref, valid_ref, out_ref):
    g = pl.program_id(0)
    bucket = bucket_ref[...]
    for h in range(A_HG):
        acc = jnp.zeros(bucket.shape, F32)
        for b in range(N_BUCKETS):
            acc = jnp.where(bucket == b, tab_ref[b, g * A_HG + h], acc)
        acc = acc * LOG2E
        out_ref[0, h] = jnp.where(valid_ref[0] > 0, acc, NEG)
        out_ref[1, h] = jnp.where(valid_ref[1] > 0, acc, NEG)


def _bias_tables(rel_bias):
    buckets, valid = _bucket_tables()
    return pl.pallas_call(
        _bias_kernel,
        grid=(N_GROUPS,),
        in_specs=[
            pl.BlockSpec(memory_space=pltpu.SMEM),
            pl.BlockSpec((None, A_BLK, 2 * A_BLK), lambda g: (g, 0, 0)),
            pl.BlockSpec((2, A_BLK, 2 * A_BLK), lambda g: (0, 0, 0)),
        ],
        out_specs=pl.BlockSpec((None, 2, A_HG, A_BLK, 2 * A_BLK), lambda g: (g, 0, 0, 0, 0)),
        out_shape=jax.ShapeDtypeStruct((N_GROUPS, 2, A_HG, A_BLK, 2 * A_BLK), F32),
        name="bias_tables",
    )(rel_bias, jnp.asarray(buckets), jnp.asarray(valid))


def _deinterleave_matrix(rows, d):
    out = np.arange(rows)
    src = d * (out % (rows // d)) + out // (rows // d)
    return (src[:, None] == np.arange(rows)[None, :]).astype(np.float32)


PERM_BLK = 256
NORM_RB = 256


def _inproj_kernel(x_ref, g_ref, w_ref, wif_ref, *rest, n_nat, dils):
    perm_refs = rest[:N_DIL]
    p_ref, if_ref = rest[N_DIL:N_DIL + 2]
    a_refs = rest[N_DIL + 2:2 * N_DIL + 2]
    h_ref, hp_ref = rest[-2:]
    tm = x_ref.shape[0]
    j = pl.program_id(1)
    nt_dims = (((1,), (1,)), ((), ()))

    @pl.when(j == 0)
    def _():
        for rc in range(tm // NORM_RB):
            rows = slice(rc * NORM_RB, (rc + 1) * NORM_RB)
            x = x_ref[rows]
            r = lax.rsqrt(jnp.mean(x * x, axis=-1, keepdims=True) + EPS)
            h = x * r * g_ref[...]
            h_hi, h_lo = _split_bf16(h, 2)
            h_ref[rows] = h_hi
            acc = (lax.dot_general(wif_ref[...], h_hi, nt_dims, preferred_element_type=F32)
                   + lax.dot_general(wif_ref[...], h_lo, nt_dims, preferred_element_type=F32))
            if_ref[:, rows] = acc[:SUBLANES] + acc[SUBLANES:]
            acc = lax.dot_general(h_hi, w_ref[...], nt_dims, preferred_element_type=F32)
            p_ref[rows] = acc.astype(BF16)

    is_gate = (j >= P_OM // IN_TN) & (j < P_A0 // IN_TN)

    @pl.when((j > 0) & (j < n_nat) & jnp.logical_not(is_gate))
    def _():
        acc = lax.dot_general(h_ref[...], w_ref[...], nt_dims, preferred_element_type=F32)
        p_ref[...] = acc.astype(BF16)

    @pl.when(is_gate)
    def _():
        acc = lax.dot_general(h_ref[...], w_ref[...], nt_dims, preferred_element_type=F32)
        p_ref[...] = (0.5 * jnp.tanh(0.5 * acc) + 0.5).astype(BF16)

    for idx, d in enumerate(dils):
        @pl.when(j == n_nat + idx)
        def _(perm_ref=perm_refs[idx], a_ref=a_refs[idx], d=d):
            piece = PERM_BLK // d
            for c in range(tm // PERM_BLK):
                hp = jnp.dot(perm_ref[...], h_ref[c * PERM_BLK:(c + 1) * PERM_BLK],
                             preferred_element_type=F32).astype(BF16)
                for r in range(d):
                    dst = r * (tm // d) + c * piece
                    hp_ref[dst:dst + piece] = hp[r * piece:(r + 1) * piece]
            a = lax.dot_general(hp_ref[...], w_ref[...], nt_dims, preferred_element_type=F32)
            a_ref[...] = a.astype(BF16).reshape(a_ref.shape)


def _inproj(x2, g, w_p, w_if, B, S, tm=1024):
    n = x2.shape[0]
    nt = S // tm
    n_nat = P_WIDTH // IN_TN
    dils = tuple(d for _, d in GROUPS[1:])
    perms = [jnp.asarray(_deinterleave_matrix(PERM_BLK, d), BF16) for d in dils]
    return pl.pallas_call(
        functools.partial(_inproj_kernel, n_nat=n_nat, dils=dils),
        grid=(n // tm, n_nat + N_DIL),
        in_specs=[
            pl.BlockSpec((tm, D_MODEL), lambda i, j: (i, 0)),
            pl.BlockSpec((1, D_MODEL), lambda i, j: (0, 0)),
            pl.BlockSpec((IN_TN, D_MODEL), lambda i, j: (j, 0)),
            pl.BlockSpec((2 * SUBLANES, D_MODEL), lambda i, j: (0, 0)),
        ] + [pl.BlockSpec((PERM_BLK, PERM_BLK), lambda i, j: (0, 0)) for _ in dils],
        out_specs=[
            pl.BlockSpec((tm, IN_TN), lambda i, j: (i, jnp.minimum(j, n_nat - 1))),
            pl.BlockSpec((SUBLANES, tm), lambda i, j: (0, i)),
        ] + [pl.BlockSpec((None, d, tm // d, IN_TN), lambda i, j: (i // nt, 0, i % nt, 0))
             for d in dils],
        out_shape=[
            jax.ShapeDtypeStruct((n, P_WIDTH), BF16),
            jax.ShapeDtypeStruct((SUBLANES, n), F32),
        ] + [jax.ShapeDtypeStruct((B, d, S // d, IN_TN), BF16) for d in dils],
        scratch_shapes=[pltpu.VMEM((tm, D_MODEL), BF16), pltpu.VMEM((tm, D_MODEL), BF16)],
        compiler_params=pltpu.CompilerParams(
            dimension_semantics=("parallel", "arbitrary"), vmem_limit_bytes=VMEM_LIMIT),
        name="inproj",
    )(x2, g, w_p, w_if, *perms)


def _conv_shift_matrix(L):
    return np.concatenate([np.eye(L, k=-(CONV_K - 1 - j)) for j in range(CONV_K)], axis=1)


def _conv_silu(x_ref, tail_ref, shift_ref, cw, cb):
    L, C = x_ref.shape
    pack = 2 * SUBLANES
    x3 = x_ref[...].reshape(L // pack, pack, C)
    prods = []
    for j in range(CONV_K):
        wj = jnp.broadcast_to(cw[j:j + 1], (pack, C)).astype(BF16)
        prods.append((x3 * wj[None]).reshape(L, C))
    y = jnp.dot(shift_ref[...], jnp.concatenate(prods, axis=0),
                preferred_element_type=F32) + cb
    tail = tail_ref[...]
    row = lax.broadcasted_iota(jnp.int32, tail.shape, 0)
    fix = jnp.zeros(tail.shape, F32)
    for k in range(1, CONV_K):
        tap = pltpu.roll(tail, k, 0) * cw[CONV_K - 1 - k:CONV_K - k]
        fix = fix + jnp.where(row < k, tap, 0.0)
    y = jnp.concatenate([y[:SUBLANES] + fix, y[SUBLANES:]], axis=0)
    tail_ref[...] = x_ref[L - pack:].astype(F32)[pack - SUBLANES:]
    return y * jax.nn.sigmoid(y)


GATE_ROWS = 32


def _gates_kernel(ift_ref, bif_ref, out_ref):
    nseq, _, S = out_ref.shape
    nc = S // CHUNK
    g = ift_ref[...] + jnp.concatenate([bif_ref[...]] * (nseq * nc), axis=1)
    ig = g[:M_HEADS]
    lf = jax.nn.log_sigmoid(g[M_HEADS:])
    pos = lax.broadcasted_iota(jnp.int32, lf.shape, 1) % CHUNK

    def scan(x, op, fill):
        sh = 1
        while sh < CHUNK:
            x = op(x, jnp.where(pos >= sh, pltpu.roll(x, sh, 1), fill))
            sh *= 2
        return x

    def last(x):
        x = jnp.where(pos == CHUNK - 1, x, NEG)
        sh = 1
        while sh < CHUNK:
            x = jnp.maximum(x, jnp.where(pos < CHUNK - sh, pltpu.roll(x, nseq * S - sh, 1), NEG))
            sh *= 2
        return x

    b = scan(lf, jnp.add, 0.0)
    a = ig - b
    b_last = last(b)
    a_end = b_last + a
    a_max = last(scan(a_end, jnp.maximum, NEG))
    m_prev = []
    for c in range(nseq * nc):
        if c % nc == 0:
            m = jnp.zeros((M_HEADS, CHUNK), F32)
        m_prev.append(m)
        m = jnp.maximum(b_last[:, c * CHUNK:(c + 1) * CHUNK] + m,
                        a_max[:, c * CHUNK:(c + 1) * CHUNK])
    m_prev = jnp.concatenate(m_prev, axis=1)
    m_new = jnp.maximum(b_last + m_prev, a_max)
    mx = jnp.maximum(m_prev, scan(a, jnp.maximum, NEG))
    rows = jnp.concatenate(
        [-mx, jnp.exp(m_prev - mx), jnp.exp(-(b + mx)), jnp.exp(a_end - m_new), a,
         jnp.exp(b_last + m_prev - m_new),
         jnp.zeros((GATE_ROWS - 6 * M_HEADS, nseq * S), F32)], axis=0)
    for s in range(nseq):
        out_ref[s] = rows[:, s * S:(s + 1) * S]


def _gates(ift, bif, B, S, nseq=8):
    nseq = nseq if B % nseq == 0 else 1
    return pl.pallas_call(
        _gates_kernel,
        grid=(B // nseq,),
        in_specs=[pl.BlockSpec((SUBLANES, nseq * S), lambda b: (0, b)),
                  pl.BlockSpec((SUBLANES, LANES), lambda b: (0, 0))],
        out_specs=pl.BlockSpec((nseq, GATE_ROWS, S), lambda b: (b, 0, 0)),
        out_shape=jax.ShapeDtypeStruct((B, GATE_ROWS, S), F32),
        compiler_params=pltpu.CompilerParams(dimension_semantics=("parallel",)),
        name="gates",
    )(ift, bif)


def _mlstm_kernel(q_ref, k_ref, v_ref, o_ref, g_ref, cw_ref, cb_ref, ng_ref, shift_ref,
                  avg_ref, y_ref, c_ref, n_ref, qt_ref, kt_ref):
    L = CHUNK
    nt_dims = (((1,), (1,)), ((), ()))

    @pl.when(pl.program_id(1) == 0)
    def _():
        c_ref[...] = jnp.zeros_like(c_ref)
        n_ref[...] = jnp.zeros_like(n_ref)
        qt_ref[...] = jnp.zeros_like(qt_ref)
        kt_ref[...] = jnp.zeros_like(kt_ref)

    cw = cw_ref[...]
    cb = cb_ref[...]
    tri = (lax.broadcasted_iota(jnp.int32, (L, L), 0)
           >= lax.broadcasted_iota(jnp.int32, (L, L), 1))

    units = []
    for s in range(q_ref.shape[0]):
        q_all = _conv_silu(q_ref.at[s], qt_ref.at[s], shift_ref, cw[:, :M_WIDTH], cb[:, :M_WIDTH])
        q_all = (q_all * (M_DH ** -0.5)).astype(BF16)
        k_all = _conv_silu(k_ref.at[s], kt_ref.at[s], shift_ref, cw[:, M_WIDTH:], cb[:, M_WIDTH:])
        rows = g_ref[s]
        cols = jnp.transpose(rows)
        for h in range(M_HEADS):
            sl = slice(h * M_DH, (h + 1) * M_DH)
            col = lambda i: cols[:, i * M_HEADS + h:i * M_HEADS + h + 1]
            units.append(dict(
                s=s, h=h, sl=sl, qb=q_all[:, sl], k=k_all[:, sl], vb=v_ref[s, :, sl],
                u_col=col(0), wi_col=col(1), en_col=col(2), wk_col=col(3),
                a_row=rows[4 * M_HEADS + h:4 * M_HEADS + h + 1],
                dec_row=rows[5 * M_HEADS + h:5 * M_HEADS + h + 1]))

    for u in units:
        c_old = c_ref[u["s"], u["h"]]
        n_old = n_ref[u["s"], u["h"]:u["h"] + 1]
        kb = u["k"].astype(BF16)
        kn = jnp.concatenate([kb, jnp.broadcast_to(n_old, (L, M_DH)).astype(BF16)], axis=0)
        u["s_aug"] = lax.dot_general(u["qb"], kn, nt_dims, preferred_element_type=F32)
        u["qc"] = jnp.dot(u["qb"], c_old.astype(BF16), preferred_element_type=F32)
        kw = u["k"] * u["wk_col"]
        u["kw"] = kw.astype(BF16)
        dec = jnp.concatenate([u["dec_row"], u["dec_row"]], axis=1)
        u["c_dec"] = dec * c_old
        n_ref[u["s"], u["h"]:u["h"] + 1] = dec * n_old + jnp.sum(kw, axis=0, keepdims=True)

    for u in units:
        dmat = jnp.where(tri, jnp.exp(u["u_col"] + u["a_row"]), 0.0)
        w_intra = dmat * u["s_aug"][:, :L]
        u["w_sum"] = jnp.sum(w_intra, axis=1, keepdims=True)
        u["w_intra"] = w_intra.astype(BF16)

    for u in units:
        u["pv"] = jnp.dot(u["w_intra"], u["vb"], preferred_element_type=F32)
        c_ref[u["s"], u["h"]] = u["c_dec"] + lax.dot_general(
            u["kw"], u["vb"], (((0,), (0,)), ((), ())), preferred_element_type=F32)

    for u in units:
        num = u["wi_col"] * u["qc"] + u["pv"]
        den = u["wi_col"] * u["s_aug"][:, L:L + 1] + u["w_sum"]
        hout = num / jnp.maximum(jnp.abs(den), u["en_col"])
        mu = jnp.dot(hout.astype(BF16), avg_ref[...], preferred_element_type=F32)
        u["cen"] = hout - jnp.concatenate([mu] * (M_DH // LANES), axis=1)
        var = jnp.dot((u["cen"] * u["cen"]).astype(BF16), avg_ref[...],
                      preferred_element_type=F32)
        u["var"] = jnp.concatenate([var] * (M_DH // LANES), axis=1)

    for u in units:
        hn = u["cen"] * lax.rsqrt(u["var"] + EPS) * ng_ref[:, u["sl"]]
        y_ref[u["s"], :, u["sl"]] = (o_ref[u["s"], :, u["sl"]].astype(F32) * hn).astype(BF16)


def _mlstm(p3, gates, conv_w, conv_b, ng, nseq=8):
    B, S, _ = p3.shape
    nseq = nseq if B % nseq == 0 else 1
    wblk = lambda col: pl.BlockSpec((nseq, CHUNK, M_WIDTH), lambda b, c: (b, c, col // M_WIDTH))
    full = lambda shape: pl.BlockSpec(shape, lambda b, c: (0,) * len(shape))
    return pl.pallas_call(
        _mlstm_kernel,
        grid=(B // nseq, S // CHUNK),
        in_specs=[
            wblk(P_QM), wblk(P_KM), wblk(P_VM), wblk(P_OM),
            pl.BlockSpec((nseq, GATE_ROWS, CHUNK), lambda b, c: (b, 0, c)),
            full((CONV_K, 2 * M_WIDTH)), full((1, 2 * M_WIDTH)), full((1, M_WIDTH)),
            full((CHUNK, CONV_K * CHUNK)), full((M_DH, LANES)),
        ],
        out_specs=pl.BlockSpec((nseq, CHUNK, M_WIDTH), lambda b, c: (b, c, 0)),
        out_shape=jax.ShapeDtypeStruct((B, S, M_WIDTH), BF16),
        scratch_shapes=[
            pltpu.VMEM((nseq, M_HEADS, M_DH, M_DH), F32),
            pltpu.VMEM((nseq, SUBLANES, M_DH), F32),
            pltpu.VMEM((nseq, SUBLANES, M_WIDTH), F32),
            pltpu.VMEM((nseq, SUBLANES, M_WIDTH), F32),
        ],
        compiler_params=pltpu.CompilerParams(
            dimension_semantics=("parallel", "arbitrary"), vmem_limit_bytes=VMEM_LIMIT),
        name="mlstm",
    )(p3, p3, p3, p3, gates, conv_w, conv_b, ng,
      jnp.asarray(_conv_shift_matrix(CHUNK), BF16),
      jnp.full((M_DH, LANES), 1.0 / M_DH, BF16))


def _attn_kernel(q_ref, kc_ref, kp_ref, vc_ref, vp_ref, bias_ref, o_ref, lse_ref, *, tq):
    for r in range(q_ref.shape[0]):
        _attn_rows(q_ref.at[r], kc_ref.at[r], kp_ref.at[r], vc_ref.at[r], vp_ref.at[r], bias_ref,
                   o_ref.at[r], lse_ref.at[r], tq=tq)


def _attn_rows(q_ref, kc_ref, kp_ref, vc_ref, vp_ref, bias_ref, o_ref, lse_ref, *, tq):
    first = (pl.program_id(2) == 0).astype(jnp.int32)
    lane = lax.broadcasted_iota(jnp.int32, (1, LANES), 1)
    low = lane < A_DH
    nt_dims = (((1,), (1,)), ((), ()))
    n_pairs = A_HG // 2
    krow = lax.broadcasted_iota(jnp.int32, (4 * A_BLK, LANES), 0)
    ones_blk = jnp.where((krow < 2 * A_BLK) == low, 1.0, 0.0).astype(BF16)
    for i in range(tq // A_BLK):
        rows = slice(i * A_BLK, (i + 1) * A_BLK)
        if i == 0:
            k2 = jnp.concatenate([kp_ref[...], kc_ref[rows]], axis=0)
            v2 = jnp.concatenate([vp_ref[...], vc_ref[rows]], axis=0)
            var = first
        else:
            k2 = kc_ref[(i - 1) * A_BLK:(i + 1) * A_BLK]
            v2 = vc_ref[(i - 1) * A_BLK:(i + 1) * A_BLK]
            var = 0
        q = q_ref[rows]
        zero = jnp.zeros((A_BLK, LANES), BF16)

        scores = []
        for p in range(n_pairs):
            cs = slice(p * LANES, (p + 1) * LANES)
            qp = q[:, cs]
            q2 = jnp.concatenate([jnp.where(low, qp, zero), jnp.where(low, zero, qp)], axis=0)
            scores.append(lax.dot_general(q2, k2[:, cs], nt_dims, preferred_element_type=F32))

        probs, maxes = [], []
        for p in range(n_pairs):
            for e in range(2):
                s = scores[p][e * A_BLK:(e + 1) * A_BLK] + bias_ref[var, 2 * p + e]
                mx = jnp.max(s, axis=1, keepdims=True)
                probs.append(jnp.exp2(s - mx).astype(BF16))
                maxes.append(mx)

        lse_blk = jnp.zeros((A_BLK, LANES), F32)
        for p in range(n_pairs):
            cs = slice(p * LANES, (p + 1) * LANES)
            vp2 = v2[:, cs]
            zero2 = jnp.zeros_like(vp2)
            v_cat = jnp.concatenate([jnp.where(low, vp2, zero2), jnp.where(low, zero2, vp2)],
                                    axis=0)
            p_cat = jnp.concatenate([probs[2 * p], probs[2 * p + 1]], axis=1)
            o_ext = jnp.dot(p_cat, jnp.concatenate([v_cat, ones_blk], axis=1),
                            preferred_element_type=F32)
            l_pair = o_ext[:, LANES:]
            o_ref[rows, cs] = (o_ext[:, :LANES] * (1.0 / l_pair)).astype(BF16)
            mx_pair = jnp.where(low, maxes[2 * p], maxes[2 * p + 1])
            lse_pair = mx_pair * (1.0 / LOG2E) + jnp.log(l_pair)
            lse_blk = jnp.where(lane % A_DH == p, lse_pair, lse_blk)

        packed = jnp.zeros((A_BLK, LANES), F32)
        rest = lse_blk
        for part in range(LSE_PARTS):
            hi = rest.astype(BF16).astype(F32)
            rest = rest - hi
            packed = packed + (hi if part == 0 else pltpu.roll(hi, part * SUBLANES, 1))
        lse_ref[rows] = packed.astype(BF16)


def _attn_group(src, col0, bias, g, rows=2048):
    B, d, sd, _ = src.shape
    tq = min(rows, sd)
    nres = min(rows // tq, d)
    cur = lambda c: pl.BlockSpec((None, nres, tq, A_GW), lambda b, r, n: (b, r, n, col0 + c))
    prev = lambda c: pl.BlockSpec(
        (None, nres, A_BLK, A_GW),
        lambda b, r, n: (b, r, jnp.maximum(n * (tq // A_BLK) - 1, 0), col0 + c))
    return pl.pallas_call(
        functools.partial(_attn_kernel, tq=tq),
        grid=(B, d // nres, sd // tq),
        in_specs=[
            cur(0), cur(1), prev(1), cur(2), prev(2),
            pl.BlockSpec((None, 2, A_HG, A_BLK, 2 * A_BLK), lambda b, r, n: (g, 0, 0, 0, 0)),
        ],
        out_specs=[
            pl.BlockSpec((None, nres, tq, A_GW), lambda b, r, n: (b, r, n, 0)),
            pl.BlockSpec((None, nres, tq, LANES), lambda b, r, n: (b, r, n, 0)),
        ],
        out_shape=[
            jax.ShapeDtypeStruct((B, d, sd, A_GW), BF16),
            jax.ShapeDtypeStruct((B, d, sd, LANES), BF16),
        ],
        compiler_params=pltpu.CompilerParams(
            dimension_semantics=("parallel", "parallel", "arbitrary"),
            vmem_limit_bytes=VMEM_LIMIT),
        name=f"attn_g{g}",
    )(src, src, src, src, src, bias)


MERGE_RB = 512
UNPERM_RB = 256


def _merge_kernel(x_ref, gate_ref, ya_ref, *rest):
    o_refs = rest[:N_GROUPS]
    l_refs = rest[N_GROUPS:2 * N_GROUPS]
    unperm_refs = rest[2 * N_GROUPS:2 * N_GROUPS + N_DIL]
    e_ref, wa_ref, wb_ref, wo_ref, out_ref = rest[2 * N_GROUPS + N_DIL:]
    rb = MERGE_RB

    for c in range(x_ref.shape[0] // rb):
        rows = slice(c * rb, (c + 1) * rb)
        outs, lses = [], []
        for g in range(N_GROUPS):
            d = o_refs[g].shape[0]
            if g == 0:
                o = o_refs[g][:, c * rb:(c + 1) * rb].reshape(rb, A_GW).astype(F32)
                lp = l_refs[g][:, c * rb:(c + 1) * rb].reshape(rb, LANES).astype(F32)
            else:
                per = UNPERM_RB // d
                parts = []
                for cc in range(c * rb // UNPERM_RB, (c + 1) * rb // UNPERM_RB):
                    o = o_refs[g][:, cc * per:(cc + 1) * per].reshape(UNPERM_RB, A_GW)
                    lp = l_refs[g][:, cc * per:(cc + 1) * per].reshape(UNPERM_RB, LANES)
                    parts.append(jnp.dot(unperm_refs[g - 1][...],
                                         jnp.concatenate([o, lp], axis=1),
                                         preferred_element_type=F32))
                both = jnp.concatenate(parts, axis=0)
                o, lp = both[:, :A_GW], both[:, A_GW:]
            outs.append(o)
            lses.append(sum(lp if part == 0 else pltpu.roll(lp, LANES - part * SUBLANES, 1)
                            for part in range(LSE_PARTS)))

        lm = functools.reduce(jnp.maximum, lses)
        es = [jnp.exp(l - lm) for l in lses]
        den = functools.reduce(jnp.add, es)
        yb = jnp.zeros((rb, A_GW), F32)
        for e, o in zip(es, outs):
            wide = jnp.dot((e / den).astype(BF16), e_ref[...], preferred_element_type=F32)
            yb = yb + wide * o
        pa = jnp.dot(ya_ref[rows], wa_ref[...], preferred_element_type=F32)
        pb = jnp.dot(yb.astype(BF16), wb_ref[...], preferred_element_type=F32)
        ga = gate_ref[rows, :D_MODEL].astype(F32)
        gb = gate_ref[rows, D_MODEL:].astype(F32)
        merged = (ga * pa + gb * pb).astype(BF16)
        out_ref[rows] = x_ref[rows] + jnp.dot(merged, wo_ref[...], preferred_element_type=F32)


def _merge(x2, p_flat, ya, os_, ls_, wa, wb, wo, S, tm=1024):
    n = x2.shape[0]
    nt = S // tm
    row = lambda w: pl.BlockSpec((tm, w), lambda i: (i, 0))
    full = lambda a: pl.BlockSpec(a.shape, lambda i: (0, 0))
    grp = lambda a: pl.BlockSpec((None, a.shape[1], tm // a.shape[1], a.shape[3]),
                                 lambda i: (i // nt, 0, i % nt, 0))
    unperms = [jnp.asarray(_deinterleave_matrix(UNPERM_RB, d).T, BF16) for _, d in GROUPS[1:]]
    head_lane = (np.arange(A_HG) % 2) * A_DH + np.arange(A_HG) // 2
    expand = jnp.asarray(
        np.arange(LANES)[:, None] == head_lane[np.arange(A_GW) // A_DH][None, :], BF16)
    consts = unperms + [expand, wa, wb, wo]
    return pl.pallas_call(
        _merge_kernel,
        grid=(n // tm,),
        in_specs=[row(D_MODEL),
                  pl.BlockSpec((tm, 2 * D_MODEL), lambda i: (i, P_GATE // (2 * D_MODEL))),
                  row(M_WIDTH)]
                 + [grp(a) for a in os_] + [grp(a) for a in ls_] + [full(a) for a in consts],
        out_specs=row(D_MODEL),
        out_shape=jax.ShapeDtypeStruct((n, D_MODEL), F32),
        compiler_params=pltpu.CompilerParams(
            dimension_semantics=("parallel",), vmem_limit_bytes=VMEM_LIMIT),
        name="merge",
    )(x2, p_flat, ya, *os_, *ls_, *consts)


FF_CHUNKS = ((0, 1024), (1024, 1024), (2048, 768))
FF_RB = 256


def _rms(x, g):
    return x * lax.rsqrt(jnp.mean(x * x, axis=-1, keepdims=True) + EPS) * g


def _ffn_kernel(x_ref, gf_ref, wg_ref, wu_ref, wd_ref, gl_ref, out_ref):
    for rc in range(x_ref.shape[0] // FF_RB):
        rows = slice(rc * FF_RB, (rc + 1) * FF_RB)
        x = x_ref[rows]
        hf = _rms(x, gf_ref[...]).astype(BF16)
        acc = x
        for start, size in FF_CHUNKS:
            cs = slice(start, start + size)
            gt = jnp.dot(hf, wg_ref[:, cs], preferred_element_type=F32)
            up = jnp.dot(hf, wu_ref[:, cs], preferred_element_type=F32)
            act = (gt * jax.nn.sigmoid(gt) * up).astype(BF16)
            acc = acc + jnp.dot(act, wd_ref[cs, :], preferred_element_type=F32)
        out_ref[rows] = _rms(acc, gl_ref[...])


def _ffn(x1, gf, wg, wu, wd, gl, tm=1024):
    n = x1.shape[0]
    row = pl.BlockSpec((tm, D_MODEL), lambda i: (i, 0))
    once = lambda a: pl.BlockSpec(a.shape, lambda i: (0, 0), pipeline_mode=pl.Buffered(1))
    return pl.pallas_call(
        _ffn_kernel,
        grid=(n // tm,),
        in_specs=[row, once(gf), once(wg), once(wu), once(wd), once(gl)],
        out_specs=row,
        out_shape=jax.ShapeDtypeStruct((n, D_MODEL), F32),
        compiler_params=pltpu.CompilerParams(
            dimension_semantics=("parallel",), vmem_limit_bytes=VMEM_LIMIT),
        name="ffn",
    )(x1, gf, wg, wu, wd, gl)


def _layer(x2, B, S, bias, norm_mix_g, w_in, b_gate_if, conv_w, conv_b, mlstm_norm_g,
           w_proj_a, w_proj_b, w_out, norm_ffn_g, w_gate, w_up, w_down, final_g):
    sizes = (M_WIDTH, M_WIDTH, M_WIDTH, 2 * M_HEADS, M_WIDTH, A_WIDTH, A_WIDTH, A_WIDTH,
             2 * D_MODEL)
    offs = np.concatenate([[0], np.cumsum(sizes)])
    piece = lambda i: w_in[:, offs[i]:offs[i + 1]]
    qkv = lambda g: [piece(i)[:, g * A_GW:(g + 1) * A_GW] * sc
                     for i, sc in ((5, QK_SCALE), (6, 1.0), (7, 1.0))]
    w_p = jnp.concatenate([piece(0), piece(1), piece(2), piece(4), piece(8)]
                          + [w for g in range(N_GROUPS) for w in qkv(g)], axis=1).astype(BF16).T
    w_if = jnp.concatenate(_split_bf16(piece(3).T, 2), axis=0)

    p_flat, if_t, *dilated = _inproj(x2, norm_mix_g[None], w_p, w_if, B, S)

    bif = jnp.broadcast_to(b_gate_if[:, None], (SUBLANES, LANES))
    gates = _gates(if_t, bif, B, S)
    ya = _mlstm(p_flat.reshape(B, S, P_WIDTH), gates, conv_w, conv_b[None],
                mlstm_norm_g[None]).reshape(B * S, M_WIDTH)

    os_, ls_ = [], []
    for g in range(N_GROUPS):
        if g == 0:
            o, lse = _attn_group(p_flat.reshape(B, 1, S, P_WIDTH), P_A0 // A_GW, bias, g)
        else:
            o, lse = _attn_group(dilated[g - 1], 0, bias, g)
        os_.append(o)
        ls_.append(lse)

    x1 = _merge(x2, p_flat, ya, os_, ls_, w_proj_a.astype(BF16), w_proj_b.astype(BF16),
                w_out.astype(BF16), S)
    return _ffn(x1, norm_ffn_g[None], w_gate.astype(BF16), w_up.astype(BF16),
                w_down.astype(BF16), final_g[None])


def kernel(x, norm_mix_g, w_in, b_gate_if, conv_w, conv_b, mlstm_norm_g, w_proj_a, w_proj_b,
           w_out, norm_ffn_g, w_gate, w_up, w_down, rel_bias, norm_final_g):
    B, S, _ = x.shape
    depth = w_in.shape[0]
    assert depth == 1, "the final norm is fused into the (single) layer's channel mixer"
    bias = _bias_tables(rel_bias)
    out = _layer(x.reshape(B * S, D_MODEL), B, S, bias, norm_mix_g[0], w_in[0], b_gate_if[0],
                 conv_w[0], conv_b[0], mlstm_norm_g[0], w_proj_a[0], w_proj_b[0], w_out[0],
                 norm_ffn_g[0], w_gate[0], w_up[0], w_down[0], norm_final_g)
    return out.reshape(B, S, D_MODEL)
```

```python
import functools
import math

import numpy as np
import jax
import jax.numpy as jnp
from jax import lax
from jax.experimental import pallas as pl
from jax.experimental.pallas import tpu as pltpu

F32 = jnp.float32
BF16 = jnp.bfloat16

D_MODEL = 1024
M_HEADS = 4
M_DH = 256
M_WIDTH = M_HEADS * M_DH
CONV_K = 4
CHUNK = 128
GROUPS = ((128, 1), (512, 4), (2048, 16))
N_GROUPS = len(GROUPS)
A_HG = 8
A_DH = 64
A_GW = A_HG * A_DH
A_WIDTH = N_GROUPS * A_GW
A_BLK = 128
N_BUCKETS = 32
MAX_DISTANCE = 2048
D_FF = 2816
EPS = 1e-6
NEG = -1e30
LOG2E = math.log2(math.e)
QK_SCALE = A_DH ** -0.5 * LOG2E
LSE_PARTS = 3

LANES = 128
SUBLANES = 8

P_QM = 0
P_KM = 1024
P_VM = 2048
P_OM = 3072
P_GATE = 4096
P_A0 = 6144
P_WIDTH = P_A0 + 3 * A_GW
IN_TN = 3 * A_GW
N_DIL = N_GROUPS - 1

VMEM_LIMIT = 56 * 1024 * 1024


def _split_bf16(x, parts):
    out = []
    for _ in range(parts):
        hi = x.astype(BF16)
        out.append(hi)
        x = x - hi.astype(F32)
    return out


def _bucket_tables():
    i = np.arange(A_BLK)[:, None]
    j = np.arange(2 * A_BLK)[None, :]
    dist = i + A_BLK - j
    buckets = []
    for window, dil in GROUPS:
        n = np.maximum(dist, 0) * dil
        nf = np.maximum(n, 1).astype(np.float32)
        max_exact = N_BUCKETS // 2
        large = max_exact + (np.log(nf / max_exact) / math.log(MAX_DISTANCE / max_exact)
                             * (N_BUCKETS - max_exact)).astype(np.int32)
        large = np.minimum(large, N_BUCKETS - 1)
        buckets.append(np.where(n < max_exact, n, large).astype(np.int32))
    span = GROUPS[0][0] // GROUPS[0][1]
    assert all(w // d == span for w, d in GROUPS)
    valid = ((dist >= 0) & (dist <= span)).astype(np.int32)
    valid_first = (valid.astype(bool) & (j >= A_BLK)).astype(np.int32)
    return np.stack(buckets), np.stack([valid, valid_first])


def _bias_kernel(tab_ref, bucket_ref, valid_ref, out_ref):
    g = pl.program_id(0)
    bucket = bucket_ref[...]
    for h in range(A_HG):
        acc = jnp.zeros(bucket.shape, F32)
        for b in range(N_BUCKETS):
            acc = jnp.where(bucket == b, tab_ref[b, g * A_HG + h], acc)
        acc = acc * LOG2E
        out_ref[0, h] = jnp.where(valid_ref[0] > 0, acc, NEG)
        out_ref[1, h] = jnp.where(valid_ref[1] > 0, acc, NEG)


def _bias_tables(rel_bias):
    buckets, valid = _bucket_tables()
    return pl.pallas_call(
        _bias_kernel,
        grid=(N_GROUPS,),
        in_specs=[
            pl.BlockSpec(memory_space=pltpu.SMEM),
            pl.BlockSpec((None, A_BLK, 2 * A_BLK), lambda g: (g, 0, 0)),
            pl.BlockSpec((2, A_BLK, 2 * A_BLK), lambda g: (0, 0, 0)),
        ],
        out_specs=pl.BlockSpec((None, 2, A_HG, A_BLK, 2 * A_BLK), lambda g: (g, 0, 0, 0, 0)),
        out_shape=jax.ShapeDtypeStruct((N_GROUPS, 2, A_HG, A_BLK, 2 * A_BLK), F32),
        name="bias_tables",
    )(rel_bias, jnp.asarray(buckets), jnp.asarray(valid))


def _deinterleave_matrix(rows, d):
    out = np.arange(rows)
    src = d * (out % (rows // d)) + out // (rows // d)
    return (src[:, None] == np.arange(rows)[None, :]).astype(np.float32)


PERM_BLK = 256
NORM_RB = 256


def _inproj_kernel(x_ref, g_ref, w_ref, wif_ref, *rest, n_nat, dils):
    perm_refs = rest[:N_DIL]
    p_ref, if_ref = rest[N_DIL:N_DIL + 2]
    a_refs = rest[N_DIL + 2:2 * N_DIL + 2]
    h_ref, hp_ref = rest[-2:]
    tm = x_ref.shape[0]
    j = pl.program_id(1)
    nt_dims = (((1,), (1,)), ((), ()))

    @pl.when(j == 0)
    def _():
        for rc in range(tm // NORM_RB):
            rows = slice(rc * NORM_RB, (rc + 1) * NORM_RB)
            x = x_ref[rows]
            r = lax.rsqrt(jnp.mean(x * x, axis=-1, keepdims=True) + EPS)
            h = x * r * g_ref[...]
            h_hi, h_lo = _split_bf16(h, 2)
            h_ref[rows] = h_hi
            acc = (lax.dot_general(wif_ref[...], h_hi, nt_dims, preferred_element_type=F32)
                   + lax.dot_general(wif_ref[...], h_lo, nt_dims, preferred_element_type=F32))
            if_ref[:, rows] = acc[:SUBLANES] + acc[SUBLANES:]
            acc = lax.dot_general(h_hi, w_ref[...], nt_dims, preferred_element_type=F32)
            p_ref[rows] = acc.astype(BF16)

    is_gate = (j >= P_OM // IN_TN) & (j < P_A0 // IN_TN)

    @pl.when((j > 0) & (j < n_nat) & jnp.logical_not(is_gate))
    def _():
        acc = lax.dot_general(h_ref[...], w_ref[...], nt_dims, preferred_element_type=F32)
        p_ref[...] = acc.astype(BF16)

    @pl.when(is_gate)
    def _():
        acc = lax.dot_general(h_ref[...], w_ref[...], nt_dims, preferred_element_type=F32)
        p_ref[...] = (0.5 * jnp.tanh(0.5 * acc) + 0.5).astype(BF16)

    for idx, d in enumerate(dils):
        @pl.when(j == n_nat + idx)
        def _(perm_ref=perm_refs[idx], a_ref=a_refs[idx], d=d):
            piece = PERM_BLK // d
            for c in range(tm // PERM_BLK):
                hp = jnp.dot(perm_ref[...], h_ref[c * PERM_BLK:(c + 1) * PERM_BLK],
                             preferred_element_type=F32).astype(BF16)
                for r in range(d):
                    dst = r * (tm // d) + c * piece
                    hp_ref[dst:dst + piece] = hp[r * piece:(r + 1) * piece]
            a = lax.dot_general(hp_ref[...], w_ref[...], nt_dims, preferred_element_type=F32)
            a_ref[...] = a.astype(BF16).reshape(a_ref.shape)


def _inproj(x2, g, w_p, w_if, B, S, tm=1024):
    n = x2.shape[0]
    nt = S // tm
    n_nat = P_WIDTH // IN_TN
    dils = tuple(d for _, d in GROUPS[1:])
    perms = [jnp.asarray(_deinterleave_matrix(PERM_BLK, d), BF16) for d in dils]
    return pl.pallas_call(
        functools.partial(_inproj_kernel, n_nat=n_nat, dils=dils),
        grid=(n // tm, n_nat + N_DIL),
        in_specs=[
            pl.BlockSpec((tm, D_MODEL), lambda i, j: (i, 0)),
            pl.BlockSpec((1, D_MODEL), lambda i, j: (0, 0)),
            pl.BlockSpec((IN_TN, D_MODEL), lambda i, j: (j, 0)),
            pl.BlockSpec((2 * SUBLANES, D_MODEL), lambda i, j: (0, 0)),
        ] + [pl.BlockSpec((PERM_BLK, PERM_BLK), lambda i, j: (0, 0)) for _ in dils],
        out_specs=[
            pl.BlockSpec((tm, IN_TN), lambda i, j: (i, jnp.minimum(j, n_nat - 1))),
            pl.BlockSpec((SUBLANES, tm), lambda i, j: (0, i)),
        ] + [pl.BlockSpec((None, d, tm // d, IN_TN), lambda i, j: (i // nt, 0, i % nt, 0))
             for d in dils],
        out_shape=[
            jax.ShapeDtypeStruct((n, P_WIDTH), BF16),
            jax.ShapeDtypeStruct((SUBLANES, n), F32),
        ] + [jax.ShapeDtypeStruct((B, d, S // d, IN_TN), BF16) for d in dils],
        scratch_shapes=[pltpu.VMEM((tm, D_MODEL), BF16), pltpu.VMEM((tm, D_MODEL), BF16)],
        compiler_params=pltpu.CompilerParams(
            dimension_semantics=("parallel", "arbitrary"), vmem_limit_bytes=VMEM_LIMIT),
        name="inproj",
    )(x2, g, w_p, w_if, *perms)


def _conv_shift_matrix(L):
    return np.concatenate([np.eye(L, k=-(CONV_K - 1 - j)) for j in range(CONV_K)], axis=1)


def _conv_silu(x_ref, tail_ref, shift_ref, cw, cb):
    L, C = x_ref.shape
    pack = 2 * SUBLANES
    x3 = x_ref[...].reshape(L // pack, pack, C)
    prods = []
    for j in range(CONV_K):
        wj = jnp.broadcast_to(cw[j:j + 1], (pack, C)).astype(BF16)
        prods.append((x3 * wj[None]).reshape(L, C))
    y = jnp.dot(shift_ref[...], jnp.concatenate(prods, axis=0),
                preferred_element_type=F32) + cb
    tail = tail_ref[...]
    row = lax.broadcasted_iota(jnp.int32, tail.shape, 0)
    fix = jnp.zeros(tail.shape, F32)
    for k in range(1, CONV_K):
        tap = pltpu.roll(tail, k, 0) * cw[CONV_K - 1 - k:CONV_K - k]
        fix = fix + jnp.where(row < k, tap, 0.0)
    y = jnp.concatenate([y[:SUBLANES] + fix, y[SUBLANES:]], axis=0)
    tail_ref[...] = x_ref[L - pack:].astype(F32)[pack - SUBLANES:]
    return y * jax.nn.sigmoid(y)


GATE_ROWS = 32


def _gates_kernel(ift_ref, bif_ref, out_ref):
    nseq, _, S = out_ref.shape
    nc = S // CHUNK
    g = ift_ref[...] + jnp.concatenate([bif_ref[...]] * (nseq * nc), axis=1)
    ig = g[:M_HEADS]
    lf = jax.nn.log_sigmoid(g[M_HEADS:])
    pos = lax.broadcasted_iota(jnp.int32, lf.shape, 1) % CHUNK

    def scan(x, op, fill):
        sh = 1
        while sh < CHUNK:
            x = op(x, jnp.where(pos >= sh, pltpu.roll(x, sh, 1), fill))
            sh *= 2
        return x

    def last(x):
        x = jnp.where(pos == CHUNK - 1, x, NEG)
        sh = 1
        while sh < CHUNK:
            x = jnp.maximum(x, jnp.where(pos < CHUNK - sh, pltpu.roll(x, nseq * S - sh, 1), NEG))
            sh *= 2
        return x

    b = scan(lf, jnp.add, 0.0)
    a = ig - b
    b_last = last(b)
    a_end = b_last + a
    a_max = last(scan(a_end, jnp.maximum, NEG))
    m_prev = []
    for c in range(nseq * nc):
        if c % nc == 0:
            m = jnp.zeros((M_HEADS, CHUNK), F32)
        m_prev.append(m)
        m = jnp.maximum(b_last[:, c * CHUNK:(c + 1) * CHUNK] + m,
                        a_max[:, c * CHUNK:(c + 1) * CHUNK])
    m_prev = jnp.concatenate(m_prev, axis=1)
    m_new = jnp.maximum(b_last + m_prev, a_max)
    mx = jnp.maximum(m_prev, scan(a, jnp.maximum, NEG))
    rows = jnp.concatenate(
        [-mx, jnp.exp(m_prev - mx), jnp.exp(-(b + mx)), jnp.exp(a_end - m_new), a,
         jnp.exp(b_last + m_prev - m_new),
         jnp.zeros((GATE_ROWS - 6 * M_HEADS, nseq * S), F32)], axis=0)
    for s in range(nseq):
        out_ref[s] = rows[:, s * S:(s + 1) * S]


def _gates(ift, bif, B, S, nseq=8):
    nseq = nseq if B % nseq == 0 else 1
    return pl.pallas_call(
        _gates_kernel,
        grid=(B // nseq,),
        in_specs=[pl.BlockSpec((SUBLANES, nseq * S), lambda b: (0, b)),
                  pl.BlockSpec((SUBLANES, LANES), lambda b: (0, 0))],
        out_specs=pl.BlockSpec((nseq, GATE_ROWS, S), lambda b: (b, 0, 0)),
        out_shape=jax.ShapeDtypeStruct((B, GATE_ROWS, S), F32),
        compiler_params=pltpu.CompilerParams(dimension_semantics=("parallel",)),
        name="gates",
    )(ift, bif)


def _mlstm_kernel(q_ref, k_ref, v_ref, o_ref, g_ref, cw_ref, cb_ref, ng_ref, shift_ref,
                  avg_ref, y_ref, c_ref, n_ref, qt_ref, kt_ref):
    L = CHUNK
    nt_dims = (((1,), (1,)), ((), ()))

    @pl.when(pl.program_id(1) == 0)
    def _():
        c_ref[...] = jnp.zeros_like(c_ref)
        n_ref[...] = jnp.zeros_like(n_ref)
        qt_ref[...] = jnp.zeros_like(qt_ref)
        kt_ref[...] = jnp.zeros_like(kt_ref)

    cw = cw_ref[...]
    cb = cb_ref[...]
    tri = (lax.broadcasted_iota(jnp.int32, (L, L), 0)
           >= lax.broadcasted_iota(jnp.int32, (L, L), 1))

    units = []
    for s in range(q_ref.shape[0]):
        q_all = _conv_silu(q_ref.at[s], qt_ref.at[s], shift_ref, cw[:, :M_WIDTH], cb[:, :M_WIDTH])
        q_all = (q_all * (M_DH ** -0.5)).astype(BF16)
        k_all = _conv_silu(k_ref.at[s], kt_ref.at[s], shift_ref, cw[:, M_WIDTH:], cb[:, M_WIDTH:])
        rows = g_ref[s]
        cols = jnp.transpose(rows)
        for h in range(M_HEADS):
            sl = slice(h * M_DH, (h + 1) * M_DH)
            col = lambda i: cols[:, i * M_HEADS + h:i * M_HEADS + h + 1]
            units.append(dict(
                s=s, h=h, sl=sl, qb=q_all[:, sl], k=k_all[:, sl], vb=v_ref[s, :, sl],
                u_col=col(0), wi_col=col(1), en_col=col(2), wk_col=col(3),
                a_row=rows[4 * M_HEADS + h:4 * M_HEADS + h + 1],
                dec_row=rows[5 * M_HEADS + h:5 * M_HEADS + h + 1]))

    for u in units:
        c_old = c_ref[u["s"], u["h"]]
        n_old = n_ref[u["s"], u["h"]:u["h"] + 1]
        kb = u["k"].astype(BF16)
        kn = jnp.concatenate([kb, jnp.broadcast_to(n_old, (L, M_DH)).astype(BF16)], axis=0)
        u["s_aug"] = lax.dot_general(u["qb"], kn, nt_dims, preferred_element_type=F32)
        u["qc"] = jnp.dot(u["qb"], c_old.astype(BF16), preferred_element_type=F32)
        kw = u["k"] * u["wk_col"]
        u["kw"] = kw.astype(BF16)
        dec = jnp.concatenate([u["dec_row"], u["dec_row"]], axis=1)
        u["c_dec"] = dec * c_old
        n_ref[u["s"], u["h"]:u["h"] + 1] = dec * n_old + jnp.sum(kw, axis=0, keepdims=True)

    for u in units:
        dmat = jnp.where(tri, jnp.exp(u["u_col"] + u["a_row"]), 0.0)
        w_intra = dmat * u["s_aug"][:, :L]
        u["w_sum"] = jnp.sum(w_intra, axis=1, keepdims=True)
        u["w_intra"] = w_intra.astype(BF16)

    for u in units:
        u["pv"] = jnp.dot(u["w_intra"], u["vb"], preferred_element_type=F32)
        c_ref[u["s"], u["h"]] = u["c_dec"] + lax.dot_general(
            u["kw"], u["vb"], (((0,), (0,)), ((), ())), preferred_element_type=F32)

    for u in units:
        num = u["wi_col"] * u["qc"] + u["pv"]
        den = u["wi_col"] * u["s_aug"][:, L:L + 1] + u["w_sum"]
        hout = num / jnp.maximum(jnp.abs(den), u["en_col"])
        mu = jnp.dot(hout.astype(BF16), avg_ref[...], preferred_element_type=F32)
        u["cen"] = hout - jnp.concatenate([mu] * (M_DH // LANES), axis=1)
        var = jnp.dot((u["cen"] * u["cen"]).astype(BF16), avg_ref[...],
                      preferred_element_type=F32)
        u["var"] = jnp.concatenate([var] * (M_DH // LANES), axis=1)

    for u in units:
        hn = u["cen"] * lax.rsqrt(u["var"] + EPS) * ng_ref[:, u["sl"]]
        y_ref[u["s"], :, u["sl"]] = (o_ref[u["s"], :, u["sl"]].astype(F32) * hn).astype(BF16)


def _mlstm(p3, gates, conv_w, conv_b, ng, nseq=8):
    B, S, _ = p3.shape
    nseq = nseq if B % nseq == 0 else 1
    wblk = lambda col: pl.BlockSpec((nseq, CHUNK, M_WIDTH), lambda b, c: (b, c, col // M_WIDTH))
    full = lambda shape: pl.BlockSpec(shape, lambda b, c: (0,) * len(shape))
    return pl.pallas_call(
        _mlstm_kernel,
        grid=(B // nseq, S // CHUNK),
        in_specs=[
            wblk(P_QM), wblk(P_KM), wblk(P_VM), wblk(P_OM),
            pl.BlockSpec((nseq, GATE_ROWS, CHUNK), lambda b, c: (b, 0, c)),
            full((CONV_K, 2 * M_WIDTH)), full((1, 2 * M_WIDTH)), full((1, M_WIDTH)),
            full((CHUNK, CONV_K * CHUNK)), full((M_DH, LANES)),
        ],
        out_specs=pl.BlockSpec((nseq, CHUNK, M_WIDTH), lambda b, c: (b, c, 0)),
        out_shape=jax.ShapeDtypeStruct((B, S, M_WIDTH), BF16),
        scratch_shapes=[
            pltpu.VMEM((nseq, M_HEADS, M_DH, M_DH), F32),
            pltpu.VMEM((nseq, SUBLANES, M_DH), F32),
            pltpu.VMEM((nseq, SUBLANES, M_WIDTH), F32),
            pltpu.VMEM((nseq, SUBLANES, M_WIDTH), F32),
        ],
        compiler_params=pltpu.CompilerParams(
            dimension_semantics=("parallel", "arbitrary"), vmem_limit_bytes=VMEM_LIMIT),
        name="mlstm",
    )(p3, p3, p3, p3, gates, conv_w, conv_b, ng,
      jnp.asarray(_conv_shift_matrix(CHUNK), BF16),
      jnp.full((M_DH, LANES), 1.0 / M_DH, BF16))


def _attn_kernel(q_ref, kc_ref, kp_ref, vc_ref, vp_ref, bias_ref, o_ref, lse_ref, *, tq):
    for r in range(q_ref.shape[0]):
        _attn_rows(q_ref.at[r], kc_ref.at[r], kp_ref.at[r], vc_ref.at[r], vp_ref.at[r], bias_ref,
                   o_ref.at[r], lse_ref.at[r], tq=tq)


def _attn_rows(q_ref, kc_ref, kp_ref, vc_ref, vp_ref, bias_ref, o_ref, lse_ref, *, tq):
    first = (pl.program_id(2) == 0).astype(jnp.int32)
    lane = lax.broadcasted_iota(jnp.int32, (1, LANES), 1)
    low = lane < A_DH
    nt_dims = (((1,), (1,)), ((), ()))
    n_pairs = A_HG // 2
    krow = lax.broadcasted_iota(jnp.int32, (4 * A_BLK, LANES), 0)
    ones_blk = jnp.where((krow < 2 * A_BLK) == low, 1.0, 0.0).astype(BF16)
    for i in range(tq // A_BLK):
        rows = slice(i * A_BLK, (i + 1) * A_BLK)
        if i == 0:
            k2 = jnp.concatenate([kp_ref[...], kc_ref[rows]], axis=0)
            v2 = jnp.concatenate([vp_ref[...], vc_ref[rows]], axis=0)
            var = first
        else:
            k2 = kc_ref[(i - 1) * A_BLK:(i + 1) * A_BLK]
            v2 = vc_ref[(i - 1) * A_BLK:(i + 1) * A_BLK]
            var = 0
        q = q_ref[rows]
        zero = jnp.zeros((A_BLK, LANES), BF16)

        scores = []
        for p in range(n_pairs):
            cs = slice(p * LANES, (p + 1) * LANES)
            qp = q[:, cs]
            q2 = jnp.concatenate([jnp.where(low, qp, zero), jnp.where(low, zero, qp)], axis=0)
            scores.append(lax.dot_general(q2, k2[:, cs], nt_dims, preferred_element_type=F32))

        probs, maxes = [], []
        for p in range(n_pairs):
            for e in range(2):
                s = scores[p][e * A_BLK:(e + 1) * A_BLK] + bias_ref[var, 2 * p + e]
                mx = jnp.max(s, axis=1, keepdims=True)
                probs.append(jnp.exp2(s - mx).astype(BF16))
                maxes.append(mx)

        lse_blk = jnp.zeros((A_BLK, LANES), F32)
        for p in range(n_pairs):
            cs = slice(p * LANES, (p + 1) * LANES)
            vp2 = v2[:, cs]
            zero2 = jnp.zeros_like(vp2)
            v_cat = jnp.concatenate([jnp.where(low, vp2, zero2), jnp.where(low, zero2, vp2)],
                                    axis=0)
            p_cat = jnp.concatenate([probs[2 * p], probs[2 * p + 1]], axis=1)
            o_ext = jnp.dot(p_cat, jnp.concatenate([v_cat, ones_blk], axis=1),
                            preferred_element_type=F32)
            l_pair = o_ext[:, LANES:]
            o_ref[rows, cs] = (o_ext[:, :LANES] * (1.0 / l_pair)).astype(BF16)
            mx_pair = jnp.where(low, maxes[2 * p], maxes[2 * p + 1])
            lse_pair = mx_pair * (1.0 / LOG2E) + jnp.log(l_pair)
            lse_blk = jnp.where(lane % A_DH == p, lse_pair, lse_blk)

        packed = jnp.zeros((A_BLK, LANES), F32)
        rest = lse_blk
        for part in range(LSE_PARTS):
            hi = rest.astype(BF16).astype(F32)
            rest = rest - hi
            packed = packed + (hi if part == 0 else pltpu.roll(hi, part * SUBLANES, 1))
        lse_ref[rows] = packed.astype(BF16)


def _attn_group(src, col0, bias, g, rows=2048):
    B, d, sd, _ = src.shape
    tq = min(rows, sd)
    nres = min(rows // tq, d)
    cur = lambda c: pl.BlockSpec((None, nres, tq, A_GW), lambda b, r, n: (b, r, n, col0 + c))
    prev = lambda c: pl.BlockSpec(
        (None, nres, A_BLK, A_GW),
        lambda b, r, n: (b, r, jnp.maximum(n * (tq // A_BLK) - 1, 0), col0 + c))
    return pl.pallas_call(
        functools.partial(_attn_kernel, tq=tq),
        grid=(B, d // nres, sd // tq),
        in_specs=[
            cur(0), cur(1), prev(1), cur(2), prev(2),
            pl.BlockSpec((None, 2, A_HG, A_BLK, 2 * A_BLK), lambda b, r, n: (g, 0, 0, 0, 0)),
        ],
        out_specs=[
            pl.BlockSpec((None, nres, tq, A_GW), lambda b, r, n: (b, r, n, 0)),
            pl.BlockSpec((None, nres, tq, LANES), lambda b, r, n: (b, r, n, 0)),
        ],
        out_shape=[
            jax.ShapeDtypeStruct((B, d, sd, A_GW), BF16),
            jax.ShapeDtypeStruct((B, d, sd, LANES), BF16),
        ],
        compiler_params=pltpu.CompilerParams(
            dimension_semantics=("parallel", "parallel", "arbitrary"),
            vmem_limit_bytes=VMEM_LIMIT),
        name=f"attn_g{g}",
    )(src, src, src, src, src, bias)


MERGE_RB = 1024
UNPERM_RB = 256


def _merge_kernel(x_ref, gate_ref, ya_ref, *rest):
    o_refs = rest[:N_GROUPS]
    l_refs = rest[N_GROUPS:2 * N_GROUPS]
    unperm_refs = rest[2 * N_GROUPS:2 * N_GROUPS + N_DIL]
    e_ref, wa_ref, wb_ref, wo_ref, out_ref = rest[2 * N_GROUPS + N_DIL:]
    rb = MERGE_RB

    for c in range(x_ref.shape[0] // rb):
        rows = slice(c * rb, (c + 1) * rb)
        outs, lses = [], []
        for g in range(N_GROUPS):
            d = o_refs[g].shape[0]
            if g == 0:
                o = o_refs[g][:, c * rb:(c + 1) * rb].reshape(rb, A_GW).astype(F32)
                lp = l_refs[g][:, c * rb:(c + 1) * rb].reshape(rb, LANES).astype(F32)
            else:
                per = UNPERM_RB // d
                parts = []
                for cc in range(c * rb // UNPERM_RB, (c + 1) * rb // UNPERM_RB):
                    o = o_refs[g][:, cc * per:(cc + 1) * per].reshape(UNPERM_RB, A_GW)
                    lp = l_refs[g][:, cc * per:(cc + 1) * per].reshape(UNPERM_RB, LANES)
                    parts.append(jnp.dot(unperm_refs[g - 1][...],
                                         jnp.concatenate([o, lp], axis=1),
                                         preferred_element_type=F32))
                both = jnp.concatenate(parts, axis=0)
                o, lp = both[:, :A_GW], both[:, A_GW:]
            outs.append(o)
            lses.append(sum(lp if part == 0 else pltpu.roll(lp, LANES - part * SUBLANES, 1)
                            for part in range(LSE_PARTS)))

        lm = functools.reduce(jnp.maximum, lses)
        es = [jnp.exp(l - lm) for l in lses]
        den = functools.reduce(jnp.add, es)
        yb = jnp.zeros((rb, A_GW), F32)
        for e, o in zip(es, outs):
            wide = jnp.dot((e / den).astype(BF16), e_ref[...], preferred_element_type=F32)
            yb = yb + wide * o
        pa = jnp.dot(ya_ref[rows], wa_ref[...], preferred_element_type=F32)
        pb = jnp.dot(yb.astype(BF16), wb_ref[...], preferred_element_type=F32)
        ga = gate_ref[rows, :D_MODEL].astype(F32)
        gb = gate_ref[rows, D_MODEL:].astype(F32)
        merged = (ga * pa + gb * pb).astype(BF16)
        out_ref[rows] = x_ref[rows] + jnp.dot(merged, wo_ref[...], preferred_element_type=F32)


def _merge(x2, p_flat, ya, os_, ls_, wa, wb, wo, S, tm=1024):
    n = x2.shape[0]
    nt = S // tm
    row = lambda w: pl.BlockSpec((tm, w), lambda i: (i, 0))
    full = lambda a: pl.BlockSpec(a.shape, lambda i: (0, 0))
    grp = lambda a: pl.BlockSpec((None, a.shape[1], tm // a.shape[1], a.shape[3]),
                                 lambda i: (i // nt, 0, i % nt, 0))
    unperms = [jnp.asarray(_deinterleave_matrix(UNPERM_RB, d).T, BF16) for _, d in GROUPS[1:]]
    head_lane = (np.arange(A_HG) % 2) * A_DH + np.arange(A_HG) // 2
    expand = jnp.asarray(
        np.arange(LANES)[:, None] == head_lane[np.arange(A_GW) // A_DH][None, :], BF16)
    consts = unperms + [expand, wa, wb, wo]
    return pl.pallas_call(
        _merge_kernel,
        grid=(n // tm,),
        in_specs=[row(D_MODEL),
                  pl.BlockSpec((tm, 2 * D_MODEL), lambda i: (i, P_GATE // (2 * D_MODEL))),
                  row(M_WIDTH)]
                 + [grp(a) for a in os_] + [grp(a) for a in ls_] + [full(a) for a in consts],
        out_specs=row(D_MODEL),
        out_shape=jax.ShapeDtypeStruct((n, D_MODEL), F32),
        compiler_params=pltpu.CompilerParams(
            dimension_semantics=("parallel",), vmem_limit_bytes=VMEM_LIMIT),
        name="merge",
    )(x2, p_flat, ya, *os_, *ls_, *consts)


FF_CHUNKS = ((0, 1024), (1024, 1024), (2048, 768))
FF_RB = 256


def _rms(x, g):
    return x * lax.rsqrt(jnp.mean(x * x, axis=-1, keepdims=True) + EPS) * g


def _ffn_kernel(x_ref, gf_ref, wg_ref, wu_ref, wd_ref, gl_ref, out_ref):
    for rc in range(x_ref.shape[0] // FF_RB):
        rows = slice(rc * FF_RB, (rc + 1) * FF_RB)
        x = x_ref[rows]
        hf = _rms(x, gf_ref[...]).astype(BF16)
        acc = x
        for start, size in FF_CHUNKS:
            cs = slice(start, start + size)
            gt = jnp.dot(hf, wg_ref[:, cs], preferred_element_type=F32)
            up = jnp.dot(hf, wu_ref[:, cs], preferred_element_type=F32)
            act = (gt * jax.nn.sigmoid(gt) * up).astype(BF16)
            acc = acc + jnp.dot(act, wd_ref[cs, :], preferred_element_type=F32)
        out_ref[rows] = _rms(acc, gl_ref[...])


def _ffn(x1, gf, wg, wu, wd, gl, tm=1024):
    n = x1.shape[0]
    row = pl.BlockSpec((tm, D_MODEL), lambda i: (i, 0))
    once = lambda a: pl.BlockSpec(a.shape, lambda i: (0, 0), pipeline_mode=pl.Buffered(1))
    return pl.pallas_call(
        _ffn_kernel,
        grid=(n // tm,),
        in_specs=[row, once(gf), once(wg), once(wu), once(wd), once(gl)],
        out_specs=row,
        out_shape=jax.ShapeDtypeStruct((n, D_MODEL), F32),
        compiler_params=pltpu.CompilerParams(
            dimension_semantics=("parallel",), vmem_limit_bytes=VMEM_LIMIT),
        name="ffn",
    )(x1, gf, wg, wu, wd, gl)


def _layer(x2, B, S, bias, norm_mix_g, w_in, b_gate_if, conv_w, conv_b, mlstm_norm_g,
           w_proj_a, w_proj_b, w_out, norm_ffn_g, w_gate, w_up, w_down, final_g):
    sizes = (M_WIDTH, M_WIDTH, M_WIDTH, 2 * M_HEADS, M_WIDTH, A_WIDTH, A_WIDTH, A_WIDTH,
             2 * D_MODEL)
    offs = np.concatenate([[0], np.cumsum(sizes)])
    piece = lambda i: w_in[:, offs[i]:offs[i + 1]]
    qkv = lambda g: [piece(i)[:, g * A_GW:(g + 1) * A_GW] * sc
                     for i, sc in ((5, QK_SCALE), (6, 1.0), (7, 1.0))]
    w_p = jnp.concatenate([piece(0), piece(1), piece(2), piece(4), piece(8)]
                          + [w for g in range(N_GROUPS) for w in qkv(g)], axis=1).astype(BF16).T
    w_if = jnp.concatenate(_split_bf16(piece(3).T, 2), axis=0)

    p_flat, if_t, *dilated = _inproj(x2, norm_mix_g[None], w_p, w_if, B, S)

    bif = jnp.broadcast_to(b_gate_if[:, None], (SUBLANES, LANES))
    gates = _gates(if_t, bif, B, S)
    ya = _mlstm(p_flat.reshape(B, S, P_WIDTH), gates, conv_w, conv_b[None],
                mlstm_norm_g[None]).reshape(B * S, M_WIDTH)

    os_, ls_ = [], []
    for g in range(N_GROUPS):
        if g == 0:
            o, lse = _attn_group(p_flat.reshape(B, 1, S, P_WIDTH), P_A0 // A_GW, bias, g)
        else:
            o, lse = _attn_group(dilated[g - 1], 0, bias, g)
        os_.append(o)
        ls_.append(lse)

    x1 = _merge(x2, p_flat, ya, os_, ls_, w_proj_a.astype(BF16), w_proj_b.astype(BF16),
                w_out.astype(BF16), S)
    return _ffn(x1, norm_ffn_g[None], w_gate.astype(BF16), w_up.astype(BF16),
                w_down.astype(BF16), final_g[None])


def kernel(x, norm_mix_g, w_in, b_gate_if, conv_w, conv_b, mlstm_norm_g, w_proj_a, w_proj_b,
           w_out, norm_ffn_g, w_gate, w_up, w_down, rel_bias, norm_final_g):
    B, S, _ = x.shape
    depth = w_in.shape[0]
    assert depth == 1, "the final norm is fused into the (single) layer's channel mixer"
    bias = _bias_tables(rel_bias)
    out = _layer(x.reshape(B * S, D_MODEL), B, S, bias, norm_mix_g[0], w_in[0], b_gate_if[0],
                 conv_w[0], conv_b[0], mlstm_norm_g[0], w_proj_a[0], w_proj_b[0], w_out[0],
                 norm_ffn_g[0], w_gate[0], w_up[0], w_down[0], norm_final_g)
    return out.reshape(B, S, D_MODEL)
```

```python
import functools
import math

import numpy as np
import jax
import jax.numpy as jnp
from jax import lax
from jax.experimental import pallas as pl
from jax.experimental.pallas import tpu as pltpu

F32 = jnp.float32
BF16 = jnp.bfloat16

D_MODEL = 1024
M_HEADS = 4
M_DH = 256
M_WIDTH = M_HEADS * M_DH
CONV_K = 4
CHUNK = 128
GROUPS = ((128, 1), (512, 4), (2048, 16))
N_GROUPS = len(GROUPS)
A_HG = 8
A_DH = 64
A_GW = A_HG * A_DH
A_WIDTH = N_GROUPS * A_GW
A_BLK = 128
N_BUCKETS = 32
MAX_DISTANCE = 2048
D_FF = 2816
EPS = 1e-6
NEG = -1e30
LOG2E = math.log2(math.e)
QK_SCALE = A_DH ** -0.5 * LOG2E
LSE_PARTS = 3

LANES = 128
SUBLANES = 8

P_QM = 0
P_KM = 1024
P_VM = 2048
P_OM = 3072
P_GATE = 4096
P_A0 = 6144
P_WIDTH = P_A0 + 3 * A_GW
IN_TN = 3 * A_GW
N_DIL = N_GROUPS - 1

VMEM_LIMIT = 56 * 1024 * 1024


def _split_bf16(x, parts):
    out = []
    for _ in range(parts):
        hi = x.astype(BF16)
        out.append(hi)
        x = x - hi.astype(F32)
    return out


def _bucket_tables():
    i = np.arange(A_BLK)[:, None]
    j = np.arange(2 * A_BLK)[None, :]
    dist = i + A_BLK - j
    buckets = []
    for window, dil in GROUPS:
        n = np.maximum(dist, 0) * dil
        nf = np.maximum(n, 1).astype(np.float32)
        max_exact = N_BUCKETS // 2
        large = max_exact + (np.log(nf / max_exact) / math.log(MAX_DISTANCE / max_exact)
                             * (N_BUCKETS - max_exact)).astype(np.int32)
        large = np.minimum(large, N_BUCKETS - 1)
        buckets.append(np.where(n < max_exact, n, large).astype(np.int32))
    span = GROUPS[0][0] // GROUPS[0][1]
    assert all(w // d == span for w, d in GROUPS)
    valid = ((dist >= 0) & (dist <= span)).astype(np.int32)
    valid_first = (valid.astype(bool) & (j >= A_BLK)).astype(np.int32)
    return np.stack(buckets), np.stack([valid, valid_first])


def _bias_kernel(tab_ref, bucket_ref, valid_ref, out_ref):
    g = pl.program_id(0)
    bucket = bucket_ref[...]
    for h in range(A_HG):
        acc = jnp.zeros(bucket.shape, F32)
        for b in range(N_BUCKETS):
            acc = jnp.where(bucket == b, tab_ref[b, g * A_HG + h], acc)
        acc = acc * LOG2E
        out_ref[0, h] = jnp.where(valid_ref[0] > 0, acc, NEG)
        out_ref[1, h] = jnp.where(valid_ref[1] > 0, acc, NEG)


def _bias_tables(rel_bias):
    buckets, valid = _bucket_tables()
    return pl.pallas_call(
        _bias_kernel,
        grid=(N_GROUPS,),
        in_specs=[
            pl.BlockSpec(memory_space=pltpu.SMEM),
            pl.BlockSpec((None, A_BLK, 2 * A_BLK), lambda g: (g, 0, 0)),
            pl.BlockSpec((2, A_BLK, 2 * A_BLK), lambda g: (0, 0, 0)),
        ],
        out_specs=pl.BlockSpec((None, 2, A_HG, A_BLK, 2 * A_BLK), lambda g: (g, 0, 0, 0, 0)),
        out_shape=jax.ShapeDtypeStruct((N_GROUPS, 2, A_HG, A_BLK, 2 * A_BLK), F32),
        name="bias_tables",
    )(rel_bias, jnp.asarray(buckets), jnp.asarray(valid))


def _deinterleave_matrix(rows, d):
    out = np.arange(rows)
    src = d * (out % (rows // d)) + out // (rows // d)
    return (src[:, None] == np.arange(rows)[None, :]).astype(np.float32)


PERM_BLK = 256
NORM_RB = 256


def _inproj_kernel(x_ref, g_ref, w_ref, wif_ref, *rest, n_nat, dils):
    perm_refs = rest[:N_DIL]
    p_ref, if_ref = rest[N_DIL:N_DIL + 2]
    a_refs = rest[N_DIL + 2:2 * N_DIL + 2]
    h_ref, hp_ref = rest[-2:]
    tm = x_ref.shape[0]
    j = pl.program_id(1)
    nt_dims = (((1,), (1,)), ((), ()))

    @pl.when(j == 0)
    def _():
        for rc in range(tm // NORM_RB):
            rows = slice(rc * NORM_RB, (rc + 1) * NORM_RB)
            x = x_ref[rows]
            r = lax.rsqrt(jnp.mean(x * x, axis=-1, keepdims=True) + EPS)
            h = x * r * g_ref[...]
            h_hi, h_lo = _split_bf16(h, 2)
            h_ref[rows] = h_hi
            acc = (lax.dot_general(wif_ref[...], h_hi, nt_dims, preferred_element_type=F32)
                   + lax.dot_general(wif_ref[...], h_lo, nt_dims, preferred_element_type=F32))
            if_ref[:, rows] = acc[:SUBLANES] + acc[SUBLANES:]
            acc = lax.dot_general(h_hi, w_ref[...], nt_dims, preferred_element_type=F32)
            p_ref[rows] = acc.astype(BF16)

    is_gate = (j >= P_OM // IN_TN) & (j < P_A0 // IN_TN)

    @pl.when((j > 0) & (j < n_nat) & jnp.logical_not(is_gate))
    def _():
        acc = lax.dot_general(h_ref[...], w_ref[...], nt_dims, preferred_element_type=F32)
        p_ref[...] = acc.astype(BF16)

    @pl.when(is_gate)
    def _():
        acc = lax.dot_general(h_ref[...], w_ref[...], nt_dims, preferred_element_type=F32)
        p_ref[...] = (0.5 * jnp.tanh(0.5 * acc) + 0.5).astype(BF16)

    for idx, d in enumerate(dils):
        @pl.when(j == n_nat + idx)
        def _(perm_ref=perm_refs[idx], a_ref=a_refs[idx], d=d):
            piece = PERM_BLK // d
            for c in range(tm // PERM_BLK):
                hp = jnp.dot(perm_ref[...], h_ref[c * PERM_BLK:(c + 1) * PERM_BLK],
                             preferred_element_type=F32).astype(BF16)
                for r in range(d):
                    dst = r * (tm // d) + c * piece
                    hp_ref[dst:dst + piece] = hp[r * piece:(r + 1) * piece]
            a = lax.dot_general(hp_ref[...], w_ref[...], nt_dims, preferred_element_type=F32)
            a_ref[...] = a.astype(BF16).reshape(a_ref.shape)


def _inproj(x2, g, w_p, w_if, B, S, tm=1024):
    n = x2.shape[0]
    nt = S // tm
    n_nat = P_WIDTH // IN_TN
    dils = tuple(d for _, d in GROUPS[1:])
    perms = [jnp.asarray(_deinterleave_matrix(PERM_BLK, d), BF16) for d in dils]
    return pl.pallas_call(
        functools.partial(_inproj_kernel, n_nat=n_nat, dils=dils),
        grid=(n // tm, n_nat + N_DIL),
        in_specs=[
            pl.BlockSpec((tm, D_MODEL), lambda i, j: (i, 0)),
            pl.BlockSpec((1, D_MODEL), lambda i, j: (0, 0)),
            pl.BlockSpec((IN_TN, D_MODEL), lambda i, j: (j, 0)),
            pl.BlockSpec((2 * SUBLANES, D_MODEL), lambda i, j: (0, 0)),
        ] + [pl.BlockSpec((PERM_BLK, PERM_BLK), lambda i, j: (0, 0)) for _ in dils],
        out_specs=[
            pl.BlockSpec((tm, IN_TN), lambda i, j: (i, jnp.minimum(j, n_nat - 1))),
            pl.BlockSpec((SUBLANES, tm), lambda i, j: (0, i)),
        ] + [pl.BlockSpec((None, d, tm // d, IN_TN), lambda i, j: (i // nt, 0, i % nt, 0))
             for d in dils],
        out_shape=[
            jax.ShapeDtypeStruct((n, P_WIDTH), BF16),
            jax.ShapeDtypeStruct((SUBLANES, n), F32),
        ] + [jax.ShapeDtypeStruct((B, d, S // d, IN_TN), BF16) for d in dils],
        scratch_shapes=[pltpu.VMEM((tm, D_MODEL), BF16), pltpu.VMEM((tm, D_MODEL), BF16)],
        compiler_params=pltpu.CompilerParams(
            dimension_semantics=("parallel", "arbitrary"), vmem_limit_bytes=VMEM_LIMIT),
        name="inproj",
    )(x2, g, w_p, w_if, *perms)


def _conv_shift_matrix(L):
    return np.concatenate([np.eye(L, k=-(CONV_K - 1 - j)) for j in range(CONV_K)], axis=1)


def _conv_silu(x_ref, tail_ref, shift_ref, cw, cb):
    L, C = x_ref.shape
    pack = 2 * SUBLANES
    x3 = x_ref[...].reshape(L // pack, pack, C)
    prods = []
    for j in range(CONV_K):
        wj = jnp.broadcast_to(cw[j:j + 1], (pack, C)).astype(BF16)
        prods.append((x3 * wj[None]).reshape(L, C))
    y = jnp.dot(shift_ref[...], jnp.concatenate(prods, axis=0),
                preferred_element_type=F32) + cb
    tail = tail_ref[...]
    row = lax.broadcasted_iota(jnp.int32, tail.shape, 0)
    fix = jnp.zeros(tail.shape, F32)
    for k in range(1, CONV_K):
        tap = pltpu.roll(tail, k, 0) * cw[CONV_K - 1 - k:CONV_K - k]
        fix = fix + jnp.where(row < k, tap, 0.0)
    y = jnp.concatenate([y[:SUBLANES] + fix, y[SUBLANES:]], axis=0)
    tail_ref[...] = x_ref[L - pack:].astype(F32)[pack - SUBLANES:]
    return y * jax.nn.sigmoid(y)


GATE_ROWS = 32


def _gates_kernel(ift_ref, bif_ref, out_ref):
    nseq, _, S = out_ref.shape
    nc = S // CHUNK
    g = ift_ref[...] + jnp.concatenate([bif_ref[...]] * (nseq * nc), axis=1)
    ig = g[:M_HEADS]
    lf = jax.nn.log_sigmoid(g[M_HEADS:])
    pos = lax.broadcasted_iota(jnp.int32, lf.shape, 1) % CHUNK

    def scan(x, op, fill):
        sh = 1
        while sh < CHUNK:
            x = op(x, jnp.where(pos >= sh, pltpu.roll(x, sh, 1), fill))
            sh *= 2
        return x

    def last(x):
        x = jnp.where(pos == CHUNK - 1, x, NEG)
        sh = 1
        while sh < CHUNK:
            x = jnp.maximum(x, jnp.where(pos < CHUNK - sh, pltpu.roll(x, nseq * S - sh, 1), NEG))
            sh *= 2
        return x

    b = scan(lf, jnp.add, 0.0)
    a = ig - b
    b_last = last(b)
    a_end = b_last + a
    a_max = last(scan(a_end, jnp.maximum, NEG))
    m_prev = []
    for c in range(nseq * nc):
        if c % nc == 0:
            m = jnp.zeros((M_HEADS, CHUNK), F32)
        m_prev.append(m)
        m = jnp.maximum(b_last[:, c * CHUNK:(c + 1) * CHUNK] + m,
                        a_max[:, c * CHUNK:(c + 1) * CHUNK])
    m_prev = jnp.concatenate(m_prev, axis=1)
    m_new = jnp.maximum(b_last + m_prev, a_max)
    mx = jnp.maximum(m_prev, scan(a, jnp.maximum, NEG))
    rows = jnp.concatenate(
        [-mx, jnp.exp(m_prev - mx), jnp.exp(-(b + mx)), jnp.exp(a_end - m_new), a,
         jnp.exp(b_last + m_prev - m_new),
         jnp.zeros((GATE_ROWS - 6 * M_HEADS, nseq * S), F32)], axis=0)
    for s in range(nseq):
        out_ref[s] = rows[:, s * S:(s + 1) * S]


def _gates(ift, bif, B, S, nseq=8):
    nseq = nseq if B % nseq == 0 else 1
    return pl.pallas_call(
        _gates_kernel,
        grid=(B // nseq,),
        in_specs=[pl.BlockSpec((SUBLANES, nseq * S), lambda b: (0, b)),
                  pl.BlockSpec((SUBLANES, LANES), lambda b: (0, 0))],
        out_specs=pl.BlockSpec((nseq, GATE_ROWS, S), lambda b: (b, 0, 0)),
        out_shape=jax.ShapeDtypeStruct((B, GATE_ROWS, S), F32),
        compiler_params=pltpu.CompilerParams(dimension_semantics=("parallel",)),
        name="gates",
    )(ift, bif)


def _mlstm_kernel(q_ref, k_ref, v_ref, g_ref, cw_ref, cb_ref, shift_ref,
                  y_ref, c_ref, n_ref, qt_ref, kt_ref):
    L = CHUNK
    nt_dims = (((1,), (1,)), ((), ()))

    @pl.when(pl.program_id(1) == 0)
    def _():
        c_ref[...] = jnp.zeros_like(c_ref)
        n_ref[...] = jnp.zeros_like(n_ref)
        qt_ref[...] = jnp.zeros_like(qt_ref)
        kt_ref[...] = jnp.zeros_like(kt_ref)

    cw = cw_ref[...]
    cb = cb_ref[...]
    tri = (lax.broadcasted_iota(jnp.int32, (L, L), 0)
           >= lax.broadcasted_iota(jnp.int32, (L, L), 1))

    units = []
    for s in range(q_ref.shape[0]):
        q_all = _conv_silu(q_ref.at[s], qt_ref.at[s], shift_ref, cw[:, :M_WIDTH], cb[:, :M_WIDTH])
        q_all = (q_all * (M_DH ** -0.5)).astype(BF16)
        k_all = _conv_silu(k_ref.at[s], kt_ref.at[s], shift_ref, cw[:, M_WIDTH:], cb[:, M_WIDTH:])
        rows = g_ref[s]
        cols = jnp.transpose(rows)
        for h in range(M_HEADS):
            sl = slice(h * M_DH, (h + 1) * M_DH)
            col = lambda i: cols[:, i * M_HEADS + h:i * M_HEADS + h + 1]
            units.append(dict(
                s=s, h=h, sl=sl, qb=q_all[:, sl], k=k_all[:, sl], vb=v_ref[s, :, sl],
                u_col=col(0), wi_col=col(1), en_col=col(2), wk_col=col(3),
                a_row=rows[4 * M_HEADS + h:4 * M_HEADS + h + 1],
                dec_row=rows[5 * M_HEADS + h:5 * M_HEADS + h + 1]))

    for u in units:
        c_old = c_ref[u["s"], u["h"]]
        n_old = n_ref[u["s"], u["h"]:u["h"] + 1]
        kb = u["k"].astype(BF16)
        kn = jnp.concatenate([kb, jnp.broadcast_to(n_old, (L, M_DH)).astype(BF16)], axis=0)
        u["s_aug"] = lax.dot_general(u["qb"], kn, nt_dims, preferred_element_type=F32)
        u["qc"] = jnp.dot(u["qb"], c_old.astype(BF16), preferred_element_type=F32)
        kw = u["k"] * u["wk_col"]
        u["kw"] = kw.astype(BF16)
        dec = jnp.concatenate([u["dec_row"], u["dec_row"]], axis=1)
        u["c_dec"] = dec * c_old
        n_ref[u["s"], u["h"]:u["h"] + 1] = dec * n_old + jnp.sum(kw, axis=0, keepdims=True)

    for u in units:
        dmat = jnp.where(tri, jnp.exp(u["u_col"] + u["a_row"]), 0.0)
        w_intra = dmat * u["s_aug"][:, :L]
        u["w_sum"] = jnp.sum(w_intra, axis=1, keepdims=True)
        u["w_intra"] = w_intra.astype(BF16)

    for u in units:
        u["pv"] = jnp.dot(u["w_intra"], u["vb"], preferred_element_type=F32)
        c_ref[u["s"], u["h"]] = u["c_dec"] + lax.dot_general(
            u["kw"], u["vb"], (((0,), (0,)), ((), ())), preferred_element_type=F32)

    for u in units:
        num = u["wi_col"] * u["qc"] + u["pv"]
        den = u["wi_col"] * u["s_aug"][:, L:L + 1] + u["w_sum"]
        y_ref[u["s"], :, u["sl"]] = num / jnp.maximum(jnp.abs(den), u["en_col"])


def _mlstm(p3, gates, conv_w, conv_b, nseq=8):
    B, S, _ = p3.shape
    nseq = nseq if B % nseq == 0 else 1
    wblk = lambda col: pl.BlockSpec((nseq, CHUNK, M_WIDTH), lambda b, c: (b, c, col // M_WIDTH))
    full = lambda shape: pl.BlockSpec(shape, lambda b, c: (0,) * len(shape))
    return pl.pallas_call(
        _mlstm_kernel,
        grid=(B // nseq, S // CHUNK),
        in_specs=[
            wblk(P_QM), wblk(P_KM), wblk(P_VM),
            pl.BlockSpec((nseq, GATE_ROWS, CHUNK), lambda b, c: (b, 0, c)),
            full((CONV_K, 2 * M_WIDTH)), full((1, 2 * M_WIDTH)),
            full((CHUNK, CONV_K * CHUNK)),
        ],
        out_specs=pl.BlockSpec((nseq, CHUNK, M_WIDTH), lambda b, c: (b, c, 0)),
        out_shape=jax.ShapeDtypeStruct((B, S, M_WIDTH), F32),
        scratch_shapes=[
            pltpu.VMEM((nseq, M_HEADS, M_DH, M_DH), F32),
            pltpu.VMEM((nseq, SUBLANES, M_DH), F32),
            pltpu.VMEM((nseq, SUBLANES, M_WIDTH), F32),
            pltpu.VMEM((nseq, SUBLANES, M_WIDTH), F32),
        ],
        compiler_params=pltpu.CompilerParams(
            dimension_semantics=("parallel", "arbitrary"), vmem_limit_bytes=VMEM_LIMIT),
        name="mlstm",
    )(p3, p3, p3, gates, conv_w, conv_b, jnp.asarray(_conv_shift_matrix(CHUNK), BF16))


def _attn_kernel(q_ref, kc_ref, kp_ref, vc_ref, vp_ref, bias_ref, o_ref, lse_ref, *, tq):
    for r in range(q_ref.shape[0]):
        _attn_rows(q_ref.at[r], kc_ref.at[r], kp_ref.at[r], vc_ref.at[r], vp_ref.at[r], bias_ref,
                   o_ref.at[r], lse_ref.at[r], tq=tq)


def _attn_rows(q_ref, kc_ref, kp_ref, vc_ref, vp_ref, bias_ref, o_ref, lse_ref, *, tq):
    first = (pl.program_id(2) == 0).astype(jnp.int32)
    lane = lax.broadcasted_iota(jnp.int32, (1, LANES), 1)
    low = lane < A_DH
    nt_dims = (((1,), (1,)), ((), ()))
    n_pairs = A_HG // 2
    krow = lax.broadcasted_iota(jnp.int32, (4 * A_BLK, LANES), 0)
    ones_blk = jnp.where((krow < 2 * A_BLK) == low, 1.0, 0.0).astype(BF16)
    for i in range(tq // A_BLK):
        rows = slice(i * A_BLK, (i + 1) * A_BLK)
        if i == 0:
            k2 = jnp.concatenate([kp_ref[...], kc_ref[rows]], axis=0)
            v2 = jnp.concatenate([vp_ref[...], vc_ref[rows]], axis=0)
            var = first
        else:
            k2 = kc_ref[(i - 1) * A_BLK:(i + 1) * A_BLK]
            v2 = vc_ref[(i - 1) * A_BLK:(i + 1) * A_BLK]
            var = 0
        q = q_ref[rows]
        zero = jnp.zeros((A_BLK, LANES), BF16)

        scores = []
        for p in range(n_pairs):
            cs = slice(p * LANES, (p + 1) * LANES)
            qp = q[:, cs]
            q2 = jnp.concatenate([jnp.where(low, qp, zero), jnp.where(low, zero, qp)], axis=0)
            scores.append(lax.dot_general(q2, k2[:, cs], nt_dims, preferred_element_type=F32))

        probs, maxes = [], []
        for p in range(n_pairs):
            for e in range(2):
                s = scores[p][e * A_BLK:(e + 1) * A_BLK] + bias_ref[var, 2 * p + e]
                mx = jnp.max(s, axis=1, keepdims=True)
                probs.append(jnp.exp2(s - mx).astype(BF16))
                maxes.append(mx)

        lse_blk = jnp.zeros((A_BLK, LANES), F32)
        for p in range(n_pairs):
            cs = slice(p * LANES, (p + 1) * LANES)
            vp2 = v2[:, cs]
            zero2 = jnp.zeros_like(vp2)
            v_cat = jnp.concatenate([jnp.where(low, vp2, zero2), jnp.where(low, zero2, vp2)],
                                    axis=0)
            p_cat = jnp.concatenate([probs[2 * p], probs[2 * p + 1]], axis=1)
            o_ext = jnp.dot(p_cat, jnp.concatenate([v_cat, ones_blk], axis=1),
                            preferred_element_type=F32)
            l_pair = o_ext[:, LANES:]
            o_ref[rows, cs] = (o_ext[:, :LANES] * (1.0 / l_pair)).astype(BF16)
            mx_pair = jnp.where(low, maxes[2 * p], maxes[2 * p + 1])
            lse_pair = mx_pair * (1.0 / LOG2E) + jnp.log(l_pair)
            lse_blk = jnp.where(lane % A_DH == p, lse_pair, lse_blk)

        packed = jnp.zeros((A_BLK, LANES), F32)
        rest = lse_blk
        for part in range(LSE_PARTS):
            hi = rest.astype(BF16).astype(F32)
            rest = rest - hi
            packed = packed + (hi if part == 0 else pltpu.roll(hi, part * SUBLANES, 1))
        lse_ref[rows] = packed.astype(BF16)


def _attn_group(src, col0, bias, g, rows=2048):
    B, d, sd, _ = src.shape
    tq = min(rows, sd)
    nres = min(rows // tq, d)
    cur = lambda c: pl.BlockSpec((None, nres, tq, A_GW), lambda b, r, n: (b, r, n, col0 + c))
    prev = lambda c: pl.BlockSpec(
        (None, nres, A_BLK, A_GW),
        lambda b, r, n: (b, r, jnp.maximum(n * (tq // A_BLK) - 1, 0), col0 + c))
    return pl.pallas_call(
        functools.partial(_attn_kernel, tq=tq),
        grid=(B, d // nres, sd // tq),
        in_specs=[
            cur(0), cur(1), prev(1), cur(2), prev(2),
            pl.BlockSpec((None, 2, A_HG, A_BLK, 2 * A_BLK), lambda b, r, n: (g, 0, 0, 0, 0)),
        ],
        out_specs=[
            pl.BlockSpec((None, nres, tq, A_GW), lambda b, r, n: (b, r, n, 0)),
            pl.BlockSpec((None, nres, tq, LANES), lambda b, r, n: (b, r, n, 0)),
        ],
        out_shape=[
            jax.ShapeDtypeStruct((B, d, sd, A_GW), BF16),
            jax.ShapeDtypeStruct((B, d, sd, LANES), BF16),
        ],
        compiler_params=pltpu.CompilerParams(
            dimension_semantics=("parallel", "parallel", "arbitrary"),
            vmem_limit_bytes=VMEM_LIMIT),
        name=f"attn_g{g}",
    )(src, src, src, src, src, bias)


MERGE_RB = 1024
UNPERM_RB = 256


def _merge_kernel(x_ref, gate_ref, hm_ref, og_ref, ng_ref, *rest):
    o_refs = rest[:N_GROUPS]
    l_refs = rest[N_GROUPS:2 * N_GROUPS]
    unperm_refs = rest[2 * N_GROUPS:2 * N_GROUPS + N_DIL]
    e_ref, wa_ref, wb_ref, wo_ref, out_ref = rest[2 * N_GROUPS + N_DIL:]
    rb = MERGE_RB

    for c in range(x_ref.shape[0] // rb):
        rows = slice(c * rb, (c + 1) * rb)
        outs, lses = [], []
        for g in range(N_GROUPS):
            d = o_refs[g].shape[0]
            if g == 0:
                o = o_refs[g][:, c * rb:(c + 1) * rb].reshape(rb, A_GW).astype(F32)
                lp = l_refs[g][:, c * rb:(c + 1) * rb].reshape(rb, LANES).astype(F32)
            else:
                per = UNPERM_RB // d
                parts = []
                for cc in range(c * rb // UNPERM_RB, (c + 1) * rb // UNPERM_RB):
                    o = o_refs[g][:, cc * per:(cc + 1) * per].reshape(UNPERM_RB, A_GW)
                    lp = l_refs[g][:, cc * per:(cc + 1) * per].reshape(UNPERM_RB, LANES)
                    parts.append(jnp.dot(unperm_refs[g - 1][...],
                                         jnp.concatenate([o, lp], axis=1),
                                         preferred_element_type=F32))
                both = jnp.concatenate(parts, axis=0)
                o, lp = both[:, :A_GW], both[:, A_GW:]
            outs.append(o)
            lses.append(sum(lp if part == 0 else pltpu.roll(lp, LANES - part * SUBLANES, 1)
                            for part in range(LSE_PARTS)))

        lm = functools.reduce(jnp.maximum, lses)
        es = [jnp.exp(l - lm) for l in lses]
        den = functools.reduce(jnp.add, es)
        yb = jnp.zeros((rb, A_GW), F32)
        for e, o in zip(es, outs):
            wide = jnp.dot((e / den).astype(BF16), e_ref[...], preferred_element_type=F32)
            yb = yb + wide * o
        heads = []
        for h in range(M_HEADS):
            hm = hm_ref[rows, h * M_DH:(h + 1) * M_DH]
            cen = hm - jnp.mean(hm, axis=1, keepdims=True)
            var = jnp.mean(cen * cen, axis=1, keepdims=True)
            heads.append(cen * lax.rsqrt(var + EPS))
        ya = jnp.concatenate(heads, axis=1) * ng_ref[...] * og_ref[rows].astype(F32)
        pa = jnp.dot(ya.astype(BF16), wa_ref[...], preferred_element_type=F32)
        pb = jnp.dot(yb.astype(BF16), wb_ref[...], preferred_element_type=F32)
        ga = gate_ref[rows, :D_MODEL].astype(F32)
        gb = gate_ref[rows, D_MODEL:].astype(F32)
        merged = (ga * pa + gb * pb).astype(BF16)
        out_ref[rows] = x_ref[rows] + jnp.dot(merged, wo_ref[...], preferred_element_type=F32)


def _merge(x2, p_flat, hm, ng, os_, ls_, wa, wb, wo, S, tm=1024):
    n = x2.shape[0]
    nt = S // tm
    row = lambda w: pl.BlockSpec((tm, w), lambda i: (i, 0))
    full = lambda a: pl.BlockSpec(a.shape, lambda i: (0, 0))
    grp = lambda a: pl.BlockSpec((None, a.shape[1], tm // a.shape[1], a.shape[3]),
                                 lambda i: (i // nt, 0, i % nt, 0))
    unperms = [jnp.asarray(_deinterleave_matrix(UNPERM_RB, d).T, BF16) for _, d in GROUPS[1:]]
    head_lane = (np.arange(A_HG) % 2) * A_DH + np.arange(A_HG) // 2
    expand = jnp.asarray(
        np.arange(LANES)[:, None] == head_lane[np.arange(A_GW) // A_DH][None, :], BF16)
    consts = unperms + [expand, wa, wb, wo]
    return pl.pallas_call(
        _merge_kernel,
        grid=(n // tm,),
        in_specs=[row(D_MODEL),
                  pl.BlockSpec((tm, 2 * D_MODEL), lambda i: (i, P_GATE // (2 * D_MODEL))),
                  row(M_WIDTH),
                  pl.BlockSpec((tm, M_WIDTH), lambda i: (i, P_OM // M_WIDTH)),
                  pl.BlockSpec((1, M_WIDTH), lambda i: (0, 0))]
                 + [grp(a) for a in os_] + [grp(a) for a in ls_] + [full(a) for a in consts],
        out_specs=row(D_MODEL),
        out_shape=jax.ShapeDtypeStruct((n, D_MODEL), F32),
        compiler_params=pltpu.CompilerParams(
            dimension_semantics=("parallel",), vmem_limit_bytes=VMEM_LIMIT),
        name="merge",
    )(x2, p_flat, hm, p_flat, ng, *os_, *ls_, *consts)


FF_CHUNKS = ((0, 1024), (1024, 1024), (2048, 768))
FF_RB = 256


def _rms(x, g):
    return x * lax.rsqrt(jnp.mean(x * x, axis=-1, keepdims=True) + EPS) * g


def _ffn_kernel(x_ref, gf_ref, wg_ref, wu_ref, wd_ref, gl_ref, out_ref):
    for rc in range(x_ref.shape[0] // FF_RB):
        rows = slice(rc * FF_RB, (rc + 1) * FF_RB)
        x = x_ref[rows]
        hf = _rms(x, gf_ref[...]).astype(BF16)
        acc = x
        for start, size in FF_CHUNKS:
            cs = slice(start, start + size)
            gt = jnp.dot(hf, wg_ref[:, cs], preferred_element_type=F32)
            up = jnp.dot(hf, wu_ref[:, cs], preferred_element_type=F32)
            act = (gt * jax.nn.sigmoid(gt) * up).astype(BF16)
            acc = acc + jnp.dot(act, wd_ref[cs, :], preferred_element_type=F32)
        out_ref[rows] = _rms(acc, gl_ref[...])


def _ffn(x1, gf, wg, wu, wd, gl, tm=1024):
    n = x1.shape[0]
    row = pl.BlockSpec((tm, D_MODEL), lambda i: (i, 0))
    once = lambda a: pl.BlockSpec(a.shape, lambda i: (0, 0), pipeline_mode=pl.Buffered(1))
    return pl.pallas_call(
        _ffn_kernel,
        grid=(n // tm,),
        in_specs=[row, once(gf), once(wg), once(wu), once(wd), once(gl)],
        out_specs=row,
        out_shape=jax.ShapeDtypeStruct((n, D_MODEL), F32),
        compiler_params=pltpu.CompilerParams(
            dimension_semantics=("parallel",), vmem_limit_bytes=VMEM_LIMIT),
        name="ffn",
    )(x1, gf, wg, wu, wd, gl)


def _layer(x2, B, S, bias, norm_mix_g, w_in, b_gate_if, conv_w, conv_b, mlstm_norm_g,
           w_proj_a, w_proj_b, w_out, norm_ffn_g, w_gate, w_up, w_down, final_g):
    sizes = (M_WIDTH, M_WIDTH, M_WIDTH, 2 * M_HEADS, M_WIDTH, A_WIDTH, A_WIDTH, A_WIDTH,
             2 * D_MODEL)
    offs = np.concatenate([[0], np.cumsum(sizes)])
    piece = lambda i: w_in[:, offs[i]:offs[i + 1]]
    qkv = lambda g: [piece(i)[:, g * A_GW:(g + 1) * A_GW] * sc
                     for i, sc in ((5, QK_SCALE), (6, 1.0), (7, 1.0))]
    w_p = jnp.concatenate([piece(0), piece(1), piece(2), piece(4), piece(8)]
                          + [w for g in range(N_GROUPS) for w in qkv(g)], axis=1).astype(BF16).T
    w_if = jnp.concatenate(_split_bf16(piece(3).T, 2), axis=0)

    p_flat, if_t, *dilated = _inproj(x2, norm_mix_g[None], w_p, w_if, B, S)

    bif = jnp.broadcast_to(b_gate_if[:, None], (SUBLANES, LANES))
    gates = _gates(if_t, bif, B, S)
    hm = _mlstm(p_flat.reshape(B, S, P_WIDTH), gates, conv_w,
                conv_b[None]).reshape(B * S, M_WIDTH)

    os_, ls_ = [], []
    for g in range(N_GROUPS):
        if g == 0:
            o, lse = _attn_group(p_flat.reshape(B, 1, S, P_WIDTH), P_A0 // A_GW, bias, g)
        else:
            o, lse = _attn_group(dilated[g - 1], 0, bias, g)
        os_.append(o)
        ls_.append(lse)

    x1 = _merge(x2, p_flat, hm, mlstm_norm_g[None], os_, ls_, w_proj_a.astype(BF16),
                w_proj_b.astype(BF16), w_out.astype(BF16), S)
    return _ffn(x1, norm_ffn_g[None], w_gate.astype(BF16), w_up.astype(BF16),
                w_down.astype(BF16), final_g[None])


def kernel(x, norm_mix_g, w_in, b_gate_if, conv_w, conv_b, mlstm_norm_g, w_proj_a, w_proj_b,
           w_out, norm_ffn_g, w_gate, w_up, w_down, rel_bias, norm_final_g):
    B, S, _ = x.shape
    depth = w_in.shape[0]
    assert depth == 1, "the final norm is fused into the (single) layer's channel mixer"
    bias = _bias_tables(rel_bias)
    out = _layer(x.reshape(B * S, D_MODEL), B, S, bias, norm_mix_g[0], w_in[0], b_gate_if[0],
                 conv_w[0], conv_b[0], mlstm_norm_g[0], w_proj_a[0], w_proj_b[0], w_out[0],
                 norm_ffn_g[0], w_gate[0], w_up[0], w_down[0], norm_final_g)
    return out.reshape(B, S, D_MODEL)
```

```python
import functools
import math

import numpy as np
import jax
import jax.numpy as jnp
from jax import lax
from jax.experimental import pallas as pl
from jax.experimental.pallas import tpu as pltpu

F32 = jnp.float32
BF16 = jnp.bfloat16

D_MODEL = 1024
M_HEADS = 4
M_DH = 256
M_WIDTH = M_HEADS * M_DH
CONV_K = 4
CHUNK = 128
GROUPS = ((128, 1), (512, 4), (2048, 16))
N_GROUPS = len(GROUPS)
A_HG = 8
A_DH = 64
A_GW = A_HG * A_DH
A_WIDTH = N_GROUPS * A_GW
A_BLK = 128
N_BUCKETS = 32
MAX_DISTANCE = 2048
D_FF = 2816
EPS = 1e-6
NEG = -1e30
LOG2E = math.log2(math.e)
QK_SCALE = A_DH ** -0.5 * LOG2E
LSE_PARTS = 3

LANES = 128
SUBLANES = 8

P_QM = 0
P_KM = 1024
P_VM = 2048
P_OM = 3072
P_GATE = 4096
P_A0 = 6144
P_WIDTH = P_A0 + 3 * A_GW
IN_TN = 3 * A_GW
N_DIL = N_GROUPS - 1

VMEM_LIMIT = 56 * 1024 * 1024


def _split_bf16(x, parts):
    out = []
    for _ in range(parts):
        hi = x.astype(BF16)
        out.append(hi)
        x = x - hi.astype(F32)
    return out


def _bucket_tables():
    i = np.arange(A_BLK)[:, None]
    j = np.arange(2 * A_BLK)[None, :]
    dist = i + A_BLK - j
    buckets = []
    for window, dil in GROUPS:
        n = np.maximum(dist, 0) * dil
        nf = np.maximum(n, 1).astype(np.float32)
        max_exact = N_BUCKETS // 2
        large = max_exact + (np.log(nf / max_exact) / math.log(MAX_DISTANCE / max_exact)
                             * (N_BUCKETS - max_exact)).astype(np.int32)
        large = np.minimum(large, N_BUCKETS - 1)
        buckets.append(np.where(n < max_exact, n, large).astype(np.int32))
    span = GROUPS[0][0] // GROUPS[0][1]
    assert all(w // d == span for w, d in GROUPS)
    valid = ((dist >= 0) & (dist <= span)).astype(np.int32)
    valid_first = (valid.astype(bool) & (j >= A_BLK)).astype(np.int32)
    return np.stack(buckets), np.stack([valid, valid_first])


def _bias_kernel(tab_ref, bucket_ref, valid_ref, out_ref):
    g = pl.program_id(0)
    bucket = bucket_ref[...]
    for h in range(A_HG):
        acc = jnp.zeros(bucket.shape, F32)
        for b in range(N_BUCKETS):
            acc = jnp.where(bucket == b, tab_ref[b, g * A_HG + h], acc)
        acc = acc * LOG2E
        out_ref[0, h] = jnp.where(valid_ref[0] > 0, acc, NEG)
        out_ref[1, h] = jnp.where(valid_ref[1] > 0, acc, NEG)


def _bias_tables(rel_bias):
    buckets, valid = _bucket_tables()
    return pl.pallas_call(
        _bias_kernel,
        grid=(N_GROUPS,),
        in_specs=[
            pl.BlockSpec(memory_space=pltpu.SMEM),
            pl.BlockSpec((None, A_BLK, 2 * A_BLK), lambda g: (g, 0, 0)),
            pl.BlockSpec((2, A_BLK, 2 * A_BLK), lambda g: (0, 0, 0)),
        ],
        out_specs=pl.BlockSpec((None, 2, A_HG, A_BLK, 2 * A_BLK), lambda g: (g, 0, 0, 0, 0)),
        out_shape=jax.ShapeDtypeStruct((N_GROUPS, 2, A_HG, A_BLK, 2 * A_BLK), F32),
        name="bias_tables",
    )(rel_bias, jnp.asarray(buckets), jnp.asarray(valid))


def _deinterleave_matrix(rows, d):
    out = np.arange(rows)
    src = d * (out % (rows // d)) + out // (rows // d)
    return (src[:, None] == np.arange(rows)[None, :]).astype(np.float32)


PERM_BLK = 256
NORM_RB = 256


def _inproj_kernel(x_ref, g_ref, w_ref, wif_ref, *rest, n_nat, dils):
    perm_refs = rest[:N_DIL]
    p_ref, if_ref = rest[N_DIL:N_DIL + 2]
    a_refs = rest[N_DIL + 2:2 * N_DIL + 2]
    h_ref, hp_ref = rest[-2:]
    tm = x_ref.shape[0]
    j = pl.program_id(1)
    nt_dims = (((1,), (1,)), ((), ()))

    @pl.when(j == 0)
    def _():
        for rc in range(tm // NORM_RB):
            rows = slice(rc * NORM_RB, (rc + 1) * NORM_RB)
            x = x_ref[rows]
            r = lax.rsqrt(jnp.mean(x * x, axis=-1, keepdims=True) + EPS)
            h = x * r * g_ref[...]
            h_hi, h_lo = _split_bf16(h, 2)
            h_ref[rows] = h_hi
            acc = (lax.dot_general(wif_ref[...], h_hi, nt_dims, preferred_element_type=F32)
                   + lax.dot_general(wif_ref[...], h_lo, nt_dims, preferred_element_type=F32))
            if_ref[:, rows] = acc[:SUBLANES] + acc[SUBLANES:]
            acc = lax.dot_general(h_hi, w_ref[...], nt_dims, preferred_element_type=F32)
            p_ref[rows] = acc.astype(BF16)

    is_gate = (j >= P_OM // IN_TN) & (j < P_A0 // IN_TN)

    @pl.when((j > 0) & (j < n_nat) & jnp.logical_not(is_gate))
    def _():
        acc = lax.dot_general(h_ref[...], w_ref[...], nt_dims, preferred_element_type=F32)
        p_ref[...] = acc.astype(BF16)

    @pl.when(is_gate)
    def _():
        acc = lax.dot_general(h_ref[...], w_ref[...], nt_dims, preferred_element_type=F32)
        p_ref[...] = (0.5 * jnp.tanh(0.5 * acc) + 0.5).astype(BF16)

    for idx, d in enumerate(dils):
        @pl.when(j == n_nat + idx)
        def _(perm_ref=perm_refs[idx], a_ref=a_refs[idx], d=d):
            piece = PERM_BLK // d
            for c in range(tm // PERM_BLK):
                hp = jnp.dot(perm_ref[...], h_ref[c * PERM_BLK:(c + 1) * PERM_BLK],
                             preferred_element_type=F32).astype(BF16)
                for r in range(d):
                    dst = r * (tm // d) + c * piece
                    hp_ref[dst:dst + piece] = hp[r * piece:(r + 1) * piece]
            a = lax.dot_general(hp_ref[...], w_ref[...], nt_dims, preferred_element_type=F32)
            a_ref[...] = a.astype(BF16).reshape(a_ref.shape)


def _inproj(x2, g, w_p, w_if, B, S, tm=1024):
    n = x2.shape[0]
    nt = S // tm
    n_nat = P_WIDTH // IN_TN
    dils = tuple(d for _, d in GROUPS[1:])
    perms = [jnp.asarray(_deinterleave_matrix(PERM_BLK, d), BF16) for d in dils]
    return pl.pallas_call(
        functools.partial(_inproj_kernel, n_nat=n_nat, dils=dils),
        grid=(n // tm, n_nat + N_DIL),
        in_specs=[
            pl.BlockSpec((tm, D_MODEL), lambda i, j: (i, 0)),
            pl.BlockSpec((1, D_MODEL), lambda i, j: (0, 0)),
            pl.BlockSpec((IN_TN, D_MODEL), lambda i, j: (j, 0)),
            pl.BlockSpec((2 * SUBLANES, D_MODEL), lambda i, j: (0, 0)),
        ] + [pl.BlockSpec((PERM_BLK, PERM_BLK), lambda i, j: (0, 0)) for _ in dils],
        out_specs=[
            pl.BlockSpec((tm, IN_TN), lambda i, j: (i, jnp.minimum(j, n_nat - 1))),
            pl.BlockSpec((SUBLANES, tm), lambda i, j: (0, i)),
        ] + [pl.BlockSpec((None, d, tm // d, IN_TN), lambda i, j: (i // nt, 0, i % nt, 0))
             for d in dils],
        out_shape=[
            jax.ShapeDtypeStruct((n, P_WIDTH), BF16),
            jax.ShapeDtypeStruct((SUBLANES, n), F32),
        ] + [jax.ShapeDtypeStruct((B, d, S // d, IN_TN), BF16) for d in dils],
        scratch_shapes=[pltpu.VMEM((tm, D_MODEL), BF16), pltpu.VMEM((tm, D_MODEL), BF16)],
        compiler_params=pltpu.CompilerParams(
            dimension_semantics=("parallel", "arbitrary"), vmem_limit_bytes=VMEM_LIMIT),
        name="inproj",
    )(x2, g, w_p, w_if, *perms)


def _conv_shift_matrix(L):
    return np.concatenate([np.eye(L, k=-(CONV_K - 1 - j)) for j in range(CONV_K)], axis=1)


def _conv_silu(x_ref, tail_ref, shift_ref, cw, cb):
    L, C = x_ref.shape
    pack = 2 * SUBLANES
    x3 = x_ref[...].reshape(L // pack, pack, C)
    prods = []
    for j in range(CONV_K):
        wj = jnp.broadcast_to(cw[j:j + 1], (pack, C)).astype(BF16)
        prods.append((x3 * wj[None]).reshape(L, C))
    y = jnp.dot(shift_ref[...], jnp.concatenate(prods, axis=0),
                preferred_element_type=F32) + cb
    tail = tail_ref[...]
    row = lax.broadcasted_iota(jnp.int32, tail.shape, 0)
    fix = jnp.zeros(tail.shape, F32)
    for k in range(1, CONV_K):
        tap = pltpu.roll(tail, k, 0) * cw[CONV_K - 1 - k:CONV_K - k]
        fix = fix + jnp.where(row < k, tap, 0.0)
    y = jnp.concatenate([y[:SUBLANES] + fix, y[SUBLANES:]], axis=0)
    tail_ref[...] = x_ref[L - pack:].astype(F32)[pack - SUBLANES:]
    return y * jax.nn.sigmoid(y)


GATE_ROWS = 32


def _gates_kernel(ift_ref, bif_ref, out_ref):
    nseq, _, S = out_ref.shape
    nc = S // CHUNK
    g = ift_ref[...] + jnp.concatenate([bif_ref[...]] * (nseq * nc), axis=1)
    ig = g[:M_HEADS]
    lf = jax.nn.log_sigmoid(g[M_HEADS:])
    pos = lax.broadcasted_iota(jnp.int32, lf.shape, 1) % CHUNK

    def scan(x, op, fill):
        sh = 1
        while sh < CHUNK:
            x = op(x, jnp.where(pos >= sh, pltpu.roll(x, sh, 1), fill))
            sh *= 2
        return x

    def last(x):
        x = jnp.where(pos == CHUNK - 1, x, NEG)
        sh = 1
        while sh < CHUNK:
            x = jnp.maximum(x, jnp.where(pos < CHUNK - sh, pltpu.roll(x, nseq * S - sh, 1), NEG))
            sh *= 2
        return x

    b = scan(lf, jnp.add, 0.0)
    a = ig - b
    b_last = last(b)
    a_end = b_last + a
    a_max = last(scan(a_end, jnp.maximum, NEG))
    m_prev = []
    for c in range(nseq * nc):
        if c % nc == 0:
            m = jnp.zeros((M_HEADS, CHUNK), F32)
        m_prev.append(m)
        m = jnp.maximum(b_last[:, c * CHUNK:(c + 1) * CHUNK] + m,
                        a_max[:, c * CHUNK:(c + 1) * CHUNK])
    m_prev = jnp.concatenate(m_prev, axis=1)
    m_new = jnp.maximum(b_last + m_prev, a_max)
    mx = jnp.maximum(m_prev, scan(a, jnp.maximum, NEG))
    rows = jnp.concatenate(
        [-mx, jnp.exp(m_prev - mx), jnp.exp(-(b + mx)), jnp.exp(a_end - m_new), a,
         jnp.exp(b_last + m_prev - m_new),
         jnp.zeros((GATE_ROWS - 6 * M_HEADS, nseq * S), F32)], axis=0)
    for s in range(nseq):
        out_ref[s] = rows[:, s * S:(s + 1) * S]


def _gates(ift, bif, B, S, nseq=8):
    nseq = nseq if B % nseq == 0 else 1
    return pl.pallas_call(
        _gates_kernel,
        grid=(B // nseq,),
        in_specs=[pl.BlockSpec((SUBLANES, nseq * S), lambda b: (0, b)),
                  pl.BlockSpec((SUBLANES, LANES), lambda b: (0, 0))],
        out_specs=pl.BlockSpec((nseq, GATE_ROWS, S), lambda b: (b, 0, 0)),
        out_shape=jax.ShapeDtypeStruct((B, GATE_ROWS, S), F32),
        compiler_params=pltpu.CompilerParams(dimension_semantics=("parallel",)),
        name="gates",
    )(ift, bif)


def _mlstm_kernel(q_ref, k_ref, v_ref, g_ref, cw_ref, cb_ref, shift_ref,
                  y_ref, c_ref, n_ref, qt_ref, kt_ref):
    L = CHUNK
    nt_dims = (((1,), (1,)), ((), ()))

    @pl.when(pl.program_id(1) == 0)
    def _():
        c_ref[...] = jnp.zeros_like(c_ref)
        n_ref[...] = jnp.zeros_like(n_ref)
        qt_ref[...] = jnp.zeros_like(qt_ref)
        kt_ref[...] = jnp.zeros_like(kt_ref)

    cw = cw_ref[...]
    cb = cb_ref[...]
    tri = (lax.broadcasted_iota(jnp.int32, (L, L), 0)
           >= lax.broadcasted_iota(jnp.int32, (L, L), 1))

    units = []
    for s in range(q_ref.shape[0]):
        q_all = _conv_silu(q_ref.at[s], qt_ref.at[s], shift_ref, cw[:, :M_WIDTH], cb[:, :M_WIDTH])
        q_all = (q_all * (M_DH ** -0.5)).astype(BF16)
        k_all = _conv_silu(k_ref.at[s], kt_ref.at[s], shift_ref, cw[:, M_WIDTH:], cb[:, M_WIDTH:])
        rows = g_ref[s]
        cols = jnp.transpose(rows)
        for h in range(M_HEADS):
            sl = slice(h * M_DH, (h + 1) * M_DH)
            col = lambda i: cols[:, i * M_HEADS + h:i * M_HEADS + h + 1]
            units.append(dict(
                s=s, h=h, sl=sl, qb=q_all[:, sl], k=k_all[:, sl], vb=v_ref[s, :, sl],
                u_col=col(0), wi_col=col(1), en_col=col(2), wk_col=col(3),
                a_row=rows[4 * M_HEADS + h:4 * M_HEADS + h + 1],
                dec_row=rows[5 * M_HEADS + h:5 * M_HEADS + h + 1]))

    for u in units:
        c_old = c_ref[u["s"], u["h"]]
        n_old = n_ref[u["s"], u["h"]:u["h"] + 1]
        kb = u["k"].astype(BF16)
        kn = jnp.concatenate([kb, jnp.broadcast_to(n_old, (L, M_DH)).astype(BF16)], axis=0)
        u["s_aug"] = lax.dot_general(u["qb"], kn, nt_dims, preferred_element_type=F32)
        u["qc"] = jnp.dot(u["qb"], c_old.astype(BF16), preferred_element_type=F32)
        kw = u["k"] * u["wk_col"]
        u["kw"] = kw.astype(BF16)
        dec = jnp.concatenate([u["dec_row"], u["dec_row"]], axis=1)
        u["c_dec"] = dec * c_old
        n_ref[u["s"], u["h"]:u["h"] + 1] = dec * n_old + jnp.sum(kw, axis=0, keepdims=True)

    for u in units:
        dmat = jnp.where(tri, jnp.exp(u["u_col"] + u["a_row"]), 0.0)
        w_intra = dmat * u["s_aug"][:, :L]
        u["w_sum"] = jnp.sum(w_intra, axis=1, keepdims=True)
        u["w_intra"] = w_intra.astype(BF16)

    for u in units:
        u["pv"] = jnp.dot(u["w_intra"], u["vb"], preferred_element_type=F32)
        c_ref[u["s"], u["h"]] = u["c_dec"] + lax.dot_general(
            u["kw"], u["vb"], (((0,), (0,)), ((), ())), preferred_element_type=F32)

    for u in units:
        num = u["wi_col"] * u["qc"] + u["pv"]
        den = u["wi_col"] * u["s_aug"][:, L:L + 1] + u["w_sum"]
        y_ref[u["s"], :, u["sl"]] = num / jnp.maximum(jnp.abs(den), u["en_col"])


def _mlstm(p3, gates, conv_w, conv_b, nseq=8):
    B, S, _ = p3.shape
    nseq = nseq if B % nseq == 0 else 1
    wblk = lambda col: pl.BlockSpec((nseq, CHUNK, M_WIDTH), lambda b, c: (b, c, col // M_WIDTH))
    full = lambda shape: pl.BlockSpec(shape, lambda b, c: (0,) * len(shape))
    return pl.pallas_call(
        _mlstm_kernel,
        grid=(B // nseq, S // CHUNK),
        in_specs=[
            wblk(P_QM), wblk(P_KM), wblk(P_VM),
            pl.BlockSpec((nseq, GATE_ROWS, CHUNK), lambda b, c: (b, 0, c)),
            full((CONV_K, 2 * M_WIDTH)), full((1, 2 * M_WIDTH)),
            full((CHUNK, CONV_K * CHUNK)),
        ],
        out_specs=pl.BlockSpec((nseq, CHUNK, M_WIDTH), lambda b, c: (b, c, 0)),
        out_shape=jax.ShapeDtypeStruct((B, S, M_WIDTH), F32),
        scratch_shapes=[
            pltpu.VMEM((nseq, M_HEADS, M_DH, M_DH), F32),
            pltpu.VMEM((nseq, SUBLANES, M_DH), F32),
            pltpu.VMEM((nseq, SUBLANES, M_WIDTH), F32),
            pltpu.VMEM((nseq, SUBLANES, M_WIDTH), F32),
        ],
        compiler_params=pltpu.CompilerParams(
            dimension_semantics=("parallel", "arbitrary"), vmem_limit_bytes=VMEM_LIMIT),
        name="mlstm",
    )(p3, p3, p3, gates, conv_w, conv_b, jnp.asarray(_conv_shift_matrix(CHUNK), BF16))


def _attn_kernel(q_ref, kc_ref, kp_ref, vc_ref, vp_ref, bias_ref, o_ref, lse_ref, *, tq):
    for r in range(q_ref.shape[0]):
        _attn_rows(q_ref.at[r], kc_ref.at[r], kp_ref.at[r], vc_ref.at[r], vp_ref.at[r], bias_ref,
                   o_ref.at[r], lse_ref.at[r], tq=tq)


def _attn_rows(q_ref, kc_ref, kp_ref, vc_ref, vp_ref, bias_ref, o_ref, lse_ref, *, tq):
    first = (pl.program_id(2) == 0).astype(jnp.int32)
    lane = lax.broadcasted_iota(jnp.int32, (1, LANES), 1)
    low = lane < A_DH
    nt_dims = (((1,), (1,)), ((), ()))
    n_pairs = A_HG // 2
    krow = lax.broadcasted_iota(jnp.int32, (4 * A_BLK, LANES), 0)
    ones_blk = jnp.where((krow < 2 * A_BLK) == low, 1.0, 0.0).astype(BF16)
    for i in range(tq // A_BLK):
        rows = slice(i * A_BLK, (i + 1) * A_BLK)
        if i == 0:
            k2 = jnp.concatenate([kp_ref[...], kc_ref[rows]], axis=0)
            v2 = jnp.concatenate([vp_ref[...], vc_ref[rows]], axis=0)
            var = first
        else:
            k2 = kc_ref[(i - 1) * A_BLK:(i + 1) * A_BLK]
            v2 = vc_ref[(i - 1) * A_BLK:(i + 1) * A_BLK]
            var = 0
        q = q_ref[rows]
        zero = jnp.zeros((A_BLK, LANES), BF16)

        scores = []
        for p in range(n_pairs):
            cs = slice(p * LANES, (p + 1) * LANES)
            qp = q[:, cs]
            q2 = jnp.concatenate([jnp.where(low, qp, zero), jnp.where(low, zero, qp)], axis=0)
            scores.append(lax.dot_general(q2, k2[:, cs], nt_dims, preferred_element_type=F32))

        probs, maxes = [], []
        for p in range(n_pairs):
            for e in range(2):
                s = scores[p][e * A_BLK:(e + 1) * A_BLK] + bias_ref[var, 2 * p + e]
                mx = jnp.max(s, axis=1, keepdims=True)
                probs.append(jnp.exp2(s - mx).astype(BF16))
                maxes.append(mx)

        lse_blk = jnp.zeros((A_BLK, LANES), F32)
        for p in range(n_pairs):
            cs = slice(p * LANES, (p + 1) * LANES)
            vp2 = v2[:, cs]
            zero2 = jnp.zeros_like(vp2)
            v_cat = jnp.concatenate([jnp.where(low, vp2, zero2), jnp.where(low, zero2, vp2)],
                                    axis=0)
            p_cat = jnp.concatenate([probs[2 * p], probs[2 * p + 1]], axis=1)
            o_ext = jnp.dot(p_cat, jnp.concatenate([v_cat, ones_blk], axis=1),
                            preferred_element_type=F32)
            l_pair = o_ext[:, LANES:]
            o_ref[rows, cs] = (o_ext[:, :LANES] * (1.0 / l_pair)).astype(BF16)
            mx_pair = jnp.where(low, maxes[2 * p], maxes[2 * p + 1])
            lse_pair = mx_pair * (1.0 / LOG2E) + jnp.log(l_pair)
            lse_blk = jnp.where(lane % A_DH == p, lse_pair, lse_blk)

        packed = jnp.zeros((A_BLK, LANES), F32)
        rest = lse_blk
        for part in range(LSE_PARTS):
            hi = rest.astype(BF16).astype(F32)
            rest = rest - hi
            packed = packed + (hi if part == 0 else pltpu.roll(hi, part * SUBLANES, 1))
        lse_ref[rows] = packed.astype(BF16)


def _attn_group(src, col0, bias, g, rows=2048):
    B, d, sd, _ = src.shape
    tq = min(rows, sd)
    nres = min(rows // tq, d)
    cur = lambda c: pl.BlockSpec((None, nres, tq, A_GW), lambda b, r, n: (b, r, n, col0 + c))
    prev = lambda c: pl.BlockSpec(
        (None, nres, A_BLK, A_GW),
        lambda b, r, n: (b, r, jnp.maximum(n * (tq // A_BLK) - 1, 0), col0 + c))
    return pl.pallas_call(
        functools.partial(_attn_kernel, tq=tq),
        grid=(B, d // nres, sd // tq),
        in_specs=[
            cur(0), cur(1), prev(1), cur(2), prev(2),
            pl.BlockSpec((None, 2, A_HG, A_BLK, 2 * A_BLK), lambda b, r, n: (g, 0, 0, 0, 0)),
        ],
        out_specs=[
            pl.BlockSpec((None, nres, tq, A_GW), lambda b, r, n: (b, r, n, 0)),
            pl.BlockSpec((None, nres, tq, LANES), lambda b, r, n: (b, r, n, 0)),
        ],
        out_shape=[
            jax.ShapeDtypeStruct((B, d, sd, A_GW), BF16),
            jax.ShapeDtypeStruct((B, d, sd, LANES), BF16),
        ],
        compiler_params=pltpu.CompilerParams(
            dimension_semantics=("parallel", "parallel", "arbitrary"),
            vmem_limit_bytes=VMEM_LIMIT),
        name=f"attn_g{g}",
    )(src, src, src, src, src, bias)


MERGE_RB = 1024
UNPERM_RB = 256


def _merge_kernel(x_ref, gate_ref, hm_ref, og_ref, ng_ref, *rest):
    o_refs = rest[:N_GROUPS]
    l_refs = rest[N_GROUPS:2 * N_GROUPS]
    unperm_refs = rest[2 * N_GROUPS:2 * N_GROUPS + N_DIL]
    e_ref, wa_ref, wb_ref, wo_ref, out_ref = rest[2 * N_GROUPS + N_DIL:]
    rb = MERGE_RB

    for c in range(x_ref.shape[0] // rb):
        rows = slice(c * rb, (c + 1) * rb)
        outs, lses = [], []
        for g in range(N_GROUPS):
            d = o_refs[g].shape[0]
            if g == 0:
                o = o_refs[g][:, c * rb:(c + 1) * rb].reshape(rb, A_GW).astype(F32)
                lp = l_refs[g][:, c * rb:(c + 1) * rb].reshape(rb, LANES).astype(F32)
            else:
                per = UNPERM_RB // d
                parts = []
                for cc in range(c * rb // UNPERM_RB, (c + 1) * rb // UNPERM_RB):
                    o = o_refs[g][:, cc * per:(cc + 1) * per].reshape(UNPERM_RB, A_GW)
                    lp = l_refs[g][:, cc * per:(cc + 1) * per].reshape(UNPERM_RB, LANES)
                    parts.append(jnp.dot(unperm_refs[g - 1][...],
                                         jnp.concatenate([o, lp], axis=1),
                                         preferred_element_type=F32))
                both = jnp.concatenate(parts, axis=0)
                o, lp = both[:, :A_GW], both[:, A_GW:]
            outs.append(o)
            lses.append(sum(lp if part == 0 else pltpu.roll(lp, LANES - part * SUBLANES, 1)
                            for part in range(LSE_PARTS)))

        lm = functools.reduce(jnp.maximum, lses)
        es = [jnp.exp(l - lm) for l in lses]
        den = functools.reduce(jnp.add, es)
        yb = jnp.zeros((rb, A_GW), F32)
        for e, o in zip(es, outs):
            wide = jnp.dot((e / den).astype(BF16), e_ref[...], preferred_element_type=F32)
            yb = yb + wide * o
        heads = []
        for h in range(M_HEADS):
            hm = hm_ref[rows, h * M_DH:(h + 1) * M_DH]
            cen = hm - jnp.mean(hm, axis=1, keepdims=True)
            var = jnp.mean(cen * cen, axis=1, keepdims=True)
            heads.append(cen * lax.rsqrt(var + EPS))
        ya = jnp.concatenate(heads, axis=1) * ng_ref[...] * og_ref[rows].astype(F32)
        pa = jnp.dot(ya.astype(BF16), wa_ref[...], preferred_element_type=F32)
        pb = jnp.dot(yb.astype(BF16), wb_ref[...], preferred_element_type=F32)
        ga = gate_ref[rows, :D_MODEL].astype(F32)
        gb = gate_ref[rows, D_MODEL:].astype(F32)
        merged = (ga * pa + gb * pb).astype(BF16)
        out_ref[rows] = x_ref[rows] + jnp.dot(merged, wo_ref[...], preferred_element_type=F32)


def _merge(x2, p_flat, hm, ng, os_, ls_, wa, wb, wo, S, tm=1024):
    n = x2.shape[0]
    nt = S // tm
    row = lambda w: pl.BlockSpec((tm, w), lambda i: (i, 0))
    full = lambda a: pl.BlockSpec(a.shape, lambda i: (0, 0))
    grp = lambda a: pl.BlockSpec((None, a.shape[1], tm // a.shape[1], a.shape[3]),
                                 lambda i: (i // nt, 0, i % nt, 0))
    unperms = [jnp.asarray(_deinterleave_matrix(UNPERM_RB, d).T, BF16) for _, d in GROUPS[1:]]
    head_lane = (np.arange(A_HG) % 2) * A_DH + np.arange(A_HG) // 2
    expand = jnp.asarray(
        np.arange(LANES)[:, None] == head_lane[np.arange(A_GW) // A_DH][None, :], BF16)
    consts = unperms + [expand, wa, wb, wo]
    return pl.pallas_call(
        _merge_kernel,
        grid=(n // tm,),
        in_specs=[row(D_MODEL),
                  pl.BlockSpec((tm, 2 * D_MODEL), lambda i: (i, P_GATE // (2 * D_MODEL))),
                  row(M_WIDTH),
                  pl.BlockSpec((tm, M_WIDTH), lambda i: (i, P_OM // M_WIDTH)),
                  pl.BlockSpec((1, M_WIDTH), lambda i: (0, 0))]
                 + [grp(a) for a in os_] + [grp(a) for a in ls_] + [full(a) for a in consts],
        out_specs=row(D_MODEL),
        out_shape=jax.ShapeDtypeStruct((n, D_MODEL), F32),
        compiler_params=pltpu.CompilerParams(
            dimension_semantics=("parallel",), vmem_limit_bytes=VMEM_LIMIT),
        name="merge",
    )(x2, p_flat, hm, p_flat, ng, *os_, *ls_, *consts)


FF_CHUNKS = ((0, 1024), (1024, 1024), (2048, 768))
FF_RB = 256


def _rms(x, g):
    return x * lax.rsqrt(jnp.mean(x * x, axis=-1, keepdims=True) + EPS) * g


def _ffn_kernel(x_ref, gf_ref, wg_ref, wu_ref, wd_ref, gl_ref, out_ref):
    for rc in range(x_ref.shape[0] // FF_RB):
        rows = slice(rc * FF_RB, (rc + 1) * FF_RB)
        x = x_ref[rows]
        hf = _rms(x, gf_ref[...]).astype(BF16)
        acc = x
        for start, size in FF_CHUNKS:
            cs = slice(start, start + size)
            gt = jnp.dot(hf, wg_ref[:, cs], preferred_element_type=F32)
            up = jnp.dot(hf, wu_ref[:, cs], preferred_element_type=F32)
            act = (gt * jax.nn.sigmoid(gt) * up).astype(BF16)
            acc = acc + jnp.dot(act, wd_ref[cs, :], preferred_element_type=F32)
        out_ref[rows] = _rms(acc, gl_ref[...])


def _ffn(x1, gf, wg, wu, wd, gl, tm=1024):
    n = x1.shape[0]
    row = pl.BlockSpec((tm, D_MODEL), lambda i: (i, 0))
    once = lambda a: pl.BlockSpec(a.shape, lambda i: (0, 0), pipeline_mode=pl.Buffered(1))
    return pl.pallas_call(
        _ffn_kernel,
        grid=(n // tm,),
        in_specs=[row, once(gf), once(wg), once(wu), once(wd), once(gl)],
        out_specs=row,
        out_shape=jax.ShapeDtypeStruct((n, D_MODEL), F32),
        compiler_params=pltpu.CompilerParams(
            dimension_semantics=("parallel",), vmem_limit_bytes=VMEM_LIMIT),
        name="ffn",
    )(x1, gf, wg, wu, wd, gl)


def _layer(x2, B, S, bias, norm_mix_g, w_in, b_gate_if, conv_w, conv_b, mlstm_norm_g,
           w_proj_a, w_proj_b, w_out, norm_ffn_g, w_gate, w_up, w_down, final_g):
    sizes = (M_WIDTH, M_WIDTH, M_WIDTH, 2 * M_HEADS, M_WIDTH, A_WIDTH, A_WIDTH, A_WIDTH,
             2 * D_MODEL)
    offs = np.concatenate([[0], np.cumsum(sizes)])
    piece = lambda i: w_in[:, offs[i]:offs[i + 1]]
    qkv = lambda g: [piece(i)[:, g * A_GW:(g + 1) * A_GW] * sc
                     for i, sc in ((5, QK_SCALE), (6, 1.0), (7, 1.0))]
    w_p = jnp.concatenate([w_in[:, :offs[3]], piece(4), piece(8)]
                          + [w for g in range(N_GROUPS) for w in qkv(g)], axis=1).astype(BF16).T
    w_if = jnp.concatenate(_split_bf16(piece(3).T, 2), axis=0)

    p_flat, if_t, *dilated = _inproj(x2, norm_mix_g[None], w_p, w_if, B, S)

    bif = jnp.broadcast_to(b_gate_if[:, None], (SUBLANES, LANES))
    gates = _gates(if_t, bif, B, S)
    hm = _mlstm(p_flat.reshape(B, S, P_WIDTH), gates, conv_w,
                conv_b[None]).reshape(B * S, M_WIDTH)

    os_, ls_ = [], []
    for g in range(N_GROUPS):
        if g == 0:
            o, lse = _attn_group(p_flat.reshape(B, 1, S, P_WIDTH), P_A0 // A_GW, bias, g)
        else:
            o, lse = _attn_group(dilated[g - 1], 0, bias, g)
        os_.append(o)
        ls_.append(lse)

    x1 = _merge(x2, p_flat, hm, mlstm_norm_g[None], os_, ls_, w_proj_a.astype(BF16),
                w_proj_b.astype(BF16), w_out.astype(BF16), S)
    return _ffn(x1, norm_ffn_g[None], w_gate.astype(BF16), w_up.astype(BF16),
                w_down.astype(BF16), final_g[None])


def kernel(x, norm_mix_g, w_in, b_gate_if, conv_w, conv_b, mlstm_norm_g, w_proj_a, w_proj_b,
           w_out, norm_ffn_g, w_gate, w_up, w_down, rel_bias, norm_final_g):
    B, S, _ = x.shape
    depth = w_in.shape[0]
    assert depth == 1, "the final norm is fused into the (single) layer's channel mixer"
    bias = _bias_tables(rel_bias)
    out = _layer(x.reshape(B * S, D_MODEL), B, S, bias, norm_mix_g[0], w_in[0], b_gate_if[0],
                 conv_w[0], conv_b[0], mlstm_norm_g[0], w_proj_a[0], w_proj_b[0], w_out[0],
                 norm_ffn_g[0], w_gate[0], w_up[0], w_down[0], norm_final_g)
    return out.reshape(B, S, D_MODEL)
```

```python
import functools
import math

import numpy as np
import jax
import jax.numpy as jnp
from jax import lax
from jax.experimental import pallas as pl
from jax.experimental.pallas import tpu as pltpu

F32 = jnp.float32
BF16 = jnp.bfloat16

D_MODEL = 1024
M_HEADS = 4
M_DH = 256
M_WIDTH = M_HEADS * M_DH
CONV_K = 4
CHUNK = 128
GROUPS = ((128, 1), (512, 4), (2048, 16))
N_GROUPS = len(GROUPS)
A_HG = 8
A_DH = 64
A_GW = A_HG * A_DH
A_WIDTH = N_GROUPS * A_GW
A_BLK = 128
N_BUCKETS = 32
MAX_DISTANCE = 2048
D_FF = 2816
EPS = 1e-6
NEG = -1e30
LOG2E = math.log2(math.e)
QK_SCALE = A_DH ** -0.5 * LOG2E
LSE_PARTS = 3

LANES = 128
SUBLANES = 8

P_QM = 0
P_KM = 1024
P_VM = 2048
P_OM = 3072
P_GATE = 4096
P_A0 = 6144
P_WIDTH = P_A0 + 3 * A_GW
IN_TN = 3 * A_GW
N_DIL = N_GROUPS - 1

VMEM_LIMIT = 56 * 1024 * 1024


def _split_bf16(x, parts):
    out = []
    for _ in range(parts):
        hi = x.astype(BF16)
        out.append(hi)
        x = x - hi.astype(F32)
    return out


def _bucket_tables():
    i = np.arange(A_BLK)[:, None]
    j = np.arange(2 * A_BLK)[None, :]
    dist = i + A_BLK - j
    buckets = []
    for window, dil in GROUPS:
        n = np.maximum(dist, 0) * dil
        nf = np.maximum(n, 1).astype(np.float32)
        max_exact = N_BUCKETS // 2
        large = max_exact + (np.log(nf / max_exact) / math.log(MAX_DISTANCE / max_exact)
                             * (N_BUCKETS - max_exact)).astype(np.int32)
        large = np.minimum(large, N_BUCKETS - 1)
        buckets.append(np.where(n < max_exact, n, large).astype(np.int32))
    span = GROUPS[0][0] // GROUPS[0][1]
    assert all(w // d == span for w, d in GROUPS)
    valid = ((dist >= 0) & (dist <= span)).astype(np.int32)
    valid_first = (valid.astype(bool) & (j >= A_BLK)).astype(np.int32)
    return np.stack(buckets), np.stack([valid, valid_first])


def _bias_kernel(tab_ref, bucket_ref, valid_ref, out_ref):
    g = pl.program_id(0)
    bucket = bucket_ref[...]
    for h in range(A_HG):
        acc = jnp.zeros(bucket.shape, F32)
        for b in range(N_BUCKETS):
            acc = jnp.where(bucket == b, tab_ref[b, g * A_HG + h], acc)
        acc = acc * LOG2E
        out_ref[0, h] = jnp.where(valid_ref[0] > 0, acc, NEG)
        out_ref[1, h] = jnp.where(valid_ref[1] > 0, acc, NEG)


def _bias_tables(rel_bias):
    buckets, valid = _bucket_tables()
    return pl.pallas_call(
        _bias_kernel,
        grid=(N_GROUPS,),
        in_specs=[
            pl.BlockSpec(memory_space=pltpu.SMEM),
            pl.BlockSpec((None, A_BLK, 2 * A_BLK), lambda g: (g, 0, 0)),
            pl.BlockSpec((2, A_BLK, 2 * A_BLK), lambda g: (0, 0, 0)),
        ],
        out_specs=pl.BlockSpec((None, 2, A_HG, A_BLK, 2 * A_BLK), lambda g: (g, 0, 0, 0, 0)),
        out_shape=jax.ShapeDtypeStruct((N_GROUPS, 2, A_HG, A_BLK, 2 * A_BLK), F32),
        name="bias_tables",
    )(rel_bias, jnp.asarray(buckets), jnp.asarray(valid))


def _deinterleave_matrix(rows, d):
    out = np.arange(rows)
    src = d * (out % (rows // d)) + out // (rows // d)
    return (src[:, None] == np.arange(rows)[None, :]).astype(np.float32)


PERM_BLK = 256
NORM_RB = 256


def _inproj_kernel(x_ref, g_ref, w_ref, wif_ref, *rest, n_nat, dils):
    perm_refs = rest[:N_DIL]
    p_ref, if_ref = rest[N_DIL:N_DIL + 2]
    a_refs = rest[N_DIL + 2:2 * N_DIL + 2]
    h_ref, hp_ref = rest[-2:]
    tm = x_ref.shape[0]
    j = pl.program_id(1)
    nt_dims = (((1,), (1,)), ((), ()))

    @pl.when(j == 0)
    def _():
        for rc in range(tm // NORM_RB):
            rows = slice(rc * NORM_RB, (rc + 1) * NORM_RB)
            x = x_ref[rows]
            r = lax.rsqrt(jnp.mean(x * x, axis=-1, keepdims=True) + EPS)
            h = x * r * g_ref[...]
            h_hi, h_lo = _split_bf16(h, 2)
            h_ref[rows] = h_hi
            acc = (lax.dot_general(wif_ref[...], h_hi, nt_dims, preferred_element_type=F32)
                   + lax.dot_general(wif_ref[...], h_lo, nt_dims, preferred_element_type=F32))
            if_ref[:, rows] = acc[:SUBLANES] + acc[SUBLANES:]
            acc = lax.dot_general(h_hi, w_ref[...], nt_dims, preferred_element_type=F32)
            p_ref[rows] = acc.astype(BF16)

    is_gate = (j >= P_OM // IN_TN) & (j < P_A0 // IN_TN)

    @pl.when((j > 0) & (j < n_nat) & jnp.logical_not(is_gate))
    def _():
        acc = lax.dot_general(h_ref[...], w_ref[...], nt_dims, preferred_element_type=F32)
        p_ref[...] = acc.astype(BF16)

    @pl.when(is_gate)
    def _():
        acc = lax.dot_general(h_ref[...], w_ref[...], nt_dims, preferred_element_type=F32)
        p_ref[...] = (0.5 * jnp.tanh(0.5 * acc) + 0.5).astype(BF16)

    for idx, d in enumerate(dils):
        @pl.when(j == n_nat + idx)
        def _(perm_ref=perm_refs[idx], a_ref=a_refs[idx], d=d):
            piece = PERM_BLK // d
            for c in range(tm // PERM_BLK):
                hp = jnp.dot(perm_ref[...], h_ref[c * PERM_BLK:(c + 1) * PERM_BLK],
                             preferred_element_type=F32).astype(BF16)
                for r in range(d):
                    dst = r * (tm // d) + c * piece
                    hp_ref[dst:dst + piece] = hp[r * piece:(r + 1) * piece]
            a = lax.dot_general(hp_ref[...], w_ref[...], nt_dims, preferred_element_type=F32)
            a_ref[...] = a.astype(BF16).reshape(a_ref.shape)


def _inproj(x2, g, w_p, w_if, B, S, tm=1024):
    n = x2.shape[0]
    nt = S // tm
    n_nat = P_WIDTH // IN_TN
    dils = tuple(d for _, d in GROUPS[1:])
    perms = [jnp.asarray(_deinterleave_matrix(PERM_BLK, d), BF16) for d in dils]
    return pl.pallas_call(
        functools.partial(_inproj_kernel, n_nat=n_nat, dils=dils),
        grid=(n // tm, n_nat + N_DIL),
        in_specs=[
            pl.BlockSpec((tm, D_MODEL), lambda i, j: (i, 0)),
            pl.BlockSpec((1, D_MODEL), lambda i, j: (0, 0)),
            pl.BlockSpec((IN_TN, D_MODEL), lambda i, j: (j, 0)),
            pl.BlockSpec((2 * SUBLANES, D_MODEL), lambda i, j: (0, 0)),
        ] + [pl.BlockSpec((PERM_BLK, PERM_BLK), lambda i, j: (0, 0)) for _ in dils],
        out_specs=[
            pl.BlockSpec((tm, IN_TN), lambda i, j: (i, jnp.minimum(j, n_nat - 1))),
            pl.BlockSpec((SUBLANES, tm), lambda i, j: (0, i)),
        ] + [pl.BlockSpec((None, d, tm // d, IN_TN), lambda i, j: (i // nt, 0, i % nt, 0))
             for d in dils],
        out_shape=[
            jax.ShapeDtypeStruct((n, P_WIDTH), BF16),
            jax.ShapeDtypeStruct((SUBLANES, n), F32),
        ] + [jax.ShapeDtypeStruct((B, d, S // d, IN_TN), BF16) for d in dils],
        scratch_shapes=[pltpu.VMEM((tm, D_MODEL), BF16), pltpu.VMEM((tm, D_MODEL), BF16)],
        compiler_params=pltpu.CompilerParams(
            dimension_semantics=("parallel", "arbitrary"), vmem_limit_bytes=VMEM_LIMIT),
        name="inproj",
    )(x2, g, w_p, w_if, *perms)


def _conv_shift_matrix(L):
    return np.concatenate([np.eye(L, k=-(CONV_K - 1 - j)) for j in range(CONV_K)], axis=1)


def _conv_silu(x_ref, tail_ref, shift_ref, cw, cb):
    L, C = x_ref.shape
    pack = 2 * SUBLANES
    x3 = x_ref[...].reshape(L // pack, pack, C)
    prods = []
    for j in range(CONV_K):
        wj = jnp.broadcast_to(cw[j:j + 1], (pack, C)).astype(BF16)
        prods.append((x3 * wj[None]).reshape(L, C))
    y = jnp.dot(shift_ref[...], jnp.concatenate(prods, axis=0),
                preferred_element_type=F32) + cb
    tail = tail_ref[...]
    row = lax.broadcasted_iota(jnp.int32, tail.shape, 0)
    fix = jnp.zeros(tail.shape, F32)
    for k in range(1, CONV_K):
        tap = pltpu.roll(tail, k, 0) * cw[CONV_K - 1 - k:CONV_K - k]
        fix = fix + jnp.where(row < k, tap, 0.0)
    y = jnp.concatenate([y[:SUBLANES] + fix, y[SUBLANES:]], axis=0)
    tail_ref[...] = x_ref[L - pack:].astype(F32)[pack - SUBLANES:]
    return y * jax.nn.sigmoid(y)


GATE_ROWS = 32


def _gates_kernel(ift_ref, bif_ref, out_ref):
    nseq, _, S = out_ref.shape
    nc = S // CHUNK
    g = ift_ref[...] + jnp.concatenate([bif_ref[...]] * (nseq * nc), axis=1)
    ig = g[:M_HEADS]
    lf = jax.nn.log_sigmoid(g[M_HEADS:])
    pos = lax.broadcasted_iota(jnp.int32, lf.shape, 1) % CHUNK

    def scan(x, op, fill):
        sh = 1
        while sh < CHUNK:
            x = op(x, jnp.where(pos >= sh, pltpu.roll(x, sh, 1), fill))
            sh *= 2
        return x

    def last(x):
        x = jnp.where(pos == CHUNK - 1, x, NEG)
        sh = 1
        while sh < CHUNK:
            x = jnp.maximum(x, jnp.where(pos < CHUNK - sh, pltpu.roll(x, nseq * S - sh, 1), NEG))
            sh *= 2
        return x

    b = scan(lf, jnp.add, 0.0)
    a = ig - b
    b_last = last(b)
    a_end = b_last + a
    a_max = last(scan(a_end, jnp.maximum, NEG))
    m_prev = []
    for c in range(nseq * nc):
        if c % nc == 0:
            m = jnp.zeros((M_HEADS, CHUNK), F32)
        m_prev.append(m)
        m = jnp.maximum(b_last[:, c * CHUNK:(c + 1) * CHUNK] + m,
                        a_max[:, c * CHUNK:(c + 1) * CHUNK])
    m_prev = jnp.concatenate(m_prev, axis=1)
    m_new = jnp.maximum(b_last + m_prev, a_max)
    mx = jnp.maximum(m_prev, scan(a, jnp.maximum, NEG))
    rows = jnp.concatenate(
        [-mx, jnp.exp(m_prev - mx), jnp.exp(-(b + mx)), jnp.exp(a_end - m_new), a,
         jnp.exp(b_last + m_prev - m_new),
         jnp.zeros((GATE_ROWS - 6 * M_HEADS, nseq * S), F32)], axis=0)
    for s in range(nseq):
        out_ref[s] = rows[:, s * S:(s + 1) * S]


def _gates(ift, bif, B, S, nseq=8):
    nseq = nseq if B % nseq == 0 else 1
    return pl.pallas_call(
        _gates_kernel,
        grid=(B // nseq,),
        in_specs=[pl.BlockSpec((SUBLANES, nseq * S), lambda b: (0, b)),
                  pl.BlockSpec((SUBLANES, LANES), lambda b: (0, 0))],
        out_specs=pl.BlockSpec((nseq, GATE_ROWS, S), lambda b: (b, 0, 0)),
        out_shape=jax.ShapeDtypeStruct((B, GATE_ROWS, S), F32),
        compiler_params=pltpu.CompilerParams(dimension_semantics=("parallel",)),
        name="gates",
    )(ift, bif)


def _mlstm_kernel(q_ref, k_ref, v_ref, g_ref, cw_ref, cb_ref, shift_ref,
                  y_ref, c_ref, n_ref, qt_ref, kt_ref):
    L = CHUNK
    nt_dims = (((1,), (1,)), ((), ()))

    @pl.when(pl.program_id(1) == 0)
    def _():
        c_ref[...] = jnp.zeros_like(c_ref)
        n_ref[...] = jnp.zeros_like(n_ref)
        qt_ref[...] = jnp.zeros_like(qt_ref)
        kt_ref[...] = jnp.zeros_like(kt_ref)

    cw = cw_ref[...]
    cb = cb_ref[...]
    tri = (lax.broadcasted_iota(jnp.int32, (L, L), 0)
           >= lax.broadcasted_iota(jnp.int32, (L, L), 1))

    units = []
    for s in range(q_ref.shape[0]):
        q_all = _conv_silu(q_ref.at[s], qt_ref.at[s], shift_ref, cw[:, :M_WIDTH], cb[:, :M_WIDTH])
        q_all = (q_all * (M_DH ** -0.5)).astype(BF16)
        k_all = _conv_silu(k_ref.at[s], kt_ref.at[s], shift_ref, cw[:, M_WIDTH:], cb[:, M_WIDTH:])
        rows = g_ref[s]
        cols = jnp.transpose(rows)
        for h in range(M_HEADS):
            sl = slice(h * M_DH, (h + 1) * M_DH)
            col = lambda i: cols[:, i * M_HEADS + h:i * M_HEADS + h + 1]
            units.append(dict(
                s=s, h=h, sl=sl, qb=q_all[:, sl], k=k_all[:, sl], vb=v_ref[s, :, sl],
                u_col=col(0), wi_col=col(1), en_col=col(2), wk_col=col(3),
                a_row=rows[4 * M_HEADS + h:4 * M_HEADS + h + 1],
                dec_row=rows[5 * M_HEADS + h:5 * M_HEADS + h + 1]))

    for u in units:
        c_old = c_ref[u["s"], u["h"]]
        n_old = n_ref[u["s"], u["h"]:u["h"] + 1]
        kb = u["k"].astype(BF16)
        kn = jnp.concatenate([kb, jnp.broadcast_to(n_old, (L, M_DH)).astype(BF16)], axis=0)
        u["s_aug"] = lax.dot_general(u["qb"], kn, nt_dims, preferred_element_type=F32)
        u["qc"] = jnp.dot(u["qb"], c_old.astype(BF16), preferred_element_type=F32)
        kw = u["k"] * u["wk_col"]
        u["kw"] = kw.astype(BF16)
        dec = jnp.concatenate([u["dec_row"], u["dec_row"]], axis=1)
        u["c_dec"] = dec * c_old
        n_ref[u["s"], u["h"]:u["h"] + 1] = dec * n_old + jnp.sum(kw, axis=0, keepdims=True)

    for u in units:
        dmat = jnp.where(tri, jnp.exp(u["u_col"] + u["a_row"]), 0.0)
        w_intra = dmat * u["s_aug"][:, :L]
        u["w_sum"] = jnp.sum(w_intra, axis=1, keepdims=True)
        u["w_intra"] = w_intra.astype(BF16)

    for u in units:
        u["pv"] = jnp.dot(u["w_intra"], u["vb"], preferred_element_type=F32)
        c_ref[u["s"], u["h"]] = u["c_dec"] + lax.dot_general(
            u["kw"], u["vb"], (((0,), (0,)), ((), ())), preferred_element_type=F32)

    for u in units:
        num = u["wi_col"] * u["qc"] + u["pv"]
        den = u["wi_col"] * u["s_aug"][:, L:L + 1] + u["w_sum"]
        y_ref[u["s"], :, u["sl"]] = num / jnp.maximum(jnp.abs(den), u["en_col"])


def _mlstm(p3, gates, conv_w, conv_b, nseq=8):
    B, S, _ = p3.shape
    nseq = nseq if B % nseq == 0 else 1
    wblk = lambda col: pl.BlockSpec((nseq, CHUNK, M_WIDTH), lambda b, c: (b, c, col // M_WIDTH))
    full = lambda shape: pl.BlockSpec(shape, lambda b, c: (0,) * len(shape))
    return pl.pallas_call(
        _mlstm_kernel,
        grid=(B // nseq, S // CHUNK),
        in_specs=[
            wblk(P_QM), wblk(P_KM), wblk(P_VM),
            pl.BlockSpec((nseq, GATE_ROWS, CHUNK), lambda b, c: (b, 0, c)),
            full((CONV_K, 2 * M_WIDTH)), full((1, 2 * M_WIDTH)),
            full((CHUNK, CONV_K * CHUNK)),
        ],
        out_specs=pl.BlockSpec((nseq, CHUNK, M_WIDTH), lambda b, c: (b, c, 0)),
        out_shape=jax.ShapeDtypeStruct((B, S, M_WIDTH), F32),
        scratch_shapes=[
            pltpu.VMEM((nseq, M_HEADS, M_DH, M_DH), F32),
            pltpu.VMEM((nseq, SUBLANES, M_DH), F32),
            pltpu.VMEM((nseq, SUBLANES, M_WIDTH), F32),
            pltpu.VMEM((nseq, SUBLANES, M_WIDTH), F32),
        ],
        compiler_params=pltpu.CompilerParams(
            dimension_semantics=("parallel", "arbitrary"), vmem_limit_bytes=VMEM_LIMIT),
        name="mlstm",
    )(p3, p3, p3, gates, conv_w, conv_b, jnp.asarray(_conv_shift_matrix(CHUNK), BF16))


def _attn_kernel(q_ref, kc_ref, kp_ref, vc_ref, vp_ref, bias_ref, o_ref, lse_ref, *, tq):
    for r in range(q_ref.shape[0]):
        _attn_rows(q_ref.at[r], kc_ref.at[r], kp_ref.at[r], vc_ref.at[r], vp_ref.at[r], bias_ref,
                   o_ref.at[r], lse_ref.at[r], tq=tq)


def _attn_rows(q_ref, kc_ref, kp_ref, vc_ref, vp_ref, bias_ref, o_ref, lse_ref, *, tq):
    first = (pl.program_id(2) == 0).astype(jnp.int32)
    lane = lax.broadcasted_iota(jnp.int32, (1, LANES), 1)
    low = lane < A_DH
    nt_dims = (((1,), (1,)), ((), ()))
    n_pairs = A_HG // 2
    krow = lax.broadcasted_iota(jnp.int32, (4 * A_BLK, LANES), 0)
    ones_blk = jnp.where((krow < 2 * A_BLK) == low, 1.0, 0.0).astype(BF16)
    low_w = lax.broadcasted_iota(jnp.int32, (1, A_GW), 1) % LANES < A_DH
    k_all = jnp.concatenate([kp_ref[...], kc_ref[...]], axis=0)
    v_all = jnp.concatenate([vp_ref[...], vc_ref[...]], axis=0)
    q_all = q_ref[...]
    zq, zv = jnp.zeros_like(q_all), jnp.zeros_like(v_all)
    q_heads = (jnp.where(low_w, q_all, zq), jnp.where(low_w, zq, q_all))
    v_heads = (jnp.where(low_w, v_all, zv), jnp.where(low_w, zv, v_all))
    for i in range(tq // A_BLK):
        rows = slice(i * A_BLK, (i + 1) * A_BLK)
        keys = slice(i * A_BLK, (i + 2) * A_BLK)
        k2 = k_all[keys]
        var = first if i == 0 else 0

        scores = []
        for p in range(n_pairs):
            cs = slice(p * LANES, (p + 1) * LANES)
            q2 = jnp.concatenate([q_heads[0][rows, cs], q_heads[1][rows, cs]], axis=0)
            scores.append(lax.dot_general(q2, k2[:, cs], nt_dims, preferred_element_type=F32))

        probs, maxes = [], []
        for p in range(n_pairs):
            for e in range(2):
                s = scores[p][e * A_BLK:(e + 1) * A_BLK] + bias_ref[var, 2 * p + e]
                mx = jnp.max(s, axis=1, keepdims=True)
                probs.append(jnp.exp2(s - mx).astype(BF16))
                maxes.append(mx)

        lse_blk = jnp.zeros((A_BLK, LANES), F32)
        for p in range(n_pairs):
            cs = slice(p * LANES, (p + 1) * LANES)
            v_cat = jnp.concatenate([v_heads[0][keys, cs], v_heads[1][keys, cs]],
                                    axis=0)
            p_cat = jnp.concatenate([probs[2 * p], probs[2 * p + 1]], axis=1)
            o_ext = jnp.dot(p_cat, jnp.concatenate([v_cat, ones_blk], axis=1),
                            preferred_element_type=F32)
            l_pair = o_ext[:, LANES:]
            o_ref[rows, cs] = (o_ext[:, :LANES] * (1.0 / l_pair)).astype(BF16)
            mx_pair = jnp.where(low, maxes[2 * p], maxes[2 * p + 1])
            lse_pair = mx_pair * (1.0 / LOG2E) + jnp.log(l_pair)
            lse_blk = jnp.where(lane % A_DH == p, lse_pair, lse_blk)

        packed = jnp.zeros((A_BLK, LANES), F32)
        rest = lse_blk
        for part in range(LSE_PARTS):
            hi = rest.astype(BF16).astype(F32)
            rest = rest - hi
            packed = packed + (hi if part == 0 else pltpu.roll(hi, part * SUBLANES, 1))
        lse_ref[rows] = packed.astype(BF16)


def _attn_group(src, col0, bias, g, rows=2048):
    B, d, sd, _ = src.shape
    tq = min(rows, sd)
    nres = min(rows // tq, d)
    cur = lambda c: pl.BlockSpec((None, nres, tq, A_GW), lambda b, r, n: (b, r, n, col0 + c))
    prev = lambda c: pl.BlockSpec(
        (None, nres, A_BLK, A_GW),
        lambda b, r, n: (b, r, jnp.maximum(n * (tq // A_BLK) - 1, 0), col0 + c))
    return pl.pallas_call(
        functools.partial(_attn_kernel, tq=tq),
        grid=(B, d // nres, sd // tq),
        in_specs=[
            cur(0), cur(1), prev(1), cur(2), prev(2),
            pl.BlockSpec((None, 2, A_HG, A_BLK, 2 * A_BLK), lambda b, r, n: (g, 0, 0, 0, 0)),
        ],
        out_specs=[
            pl.BlockSpec((None, nres, tq, A_GW), lambda b, r, n: (b, r, n, 0)),
            pl.BlockSpec((None, nres, tq, LANES), lambda b, r, n: (b, r, n, 0)),
        ],
        out_shape=[
            jax.ShapeDtypeStruct((B, d, sd, A_GW), BF16),
            jax.ShapeDtypeStruct((B, d, sd, LANES), BF16),
        ],
        compiler_params=pltpu.CompilerParams(
            dimension_semantics=("parallel", "parallel", "arbitrary"),
            vmem_limit_bytes=VMEM_LIMIT),
        name=f"attn_g{g}",
    )(src, src, src, src, src, bias)


MERGE_RB = 1024
UNPERM_RB = 256


def _merge_kernel(x_ref, gate_ref, hm_ref, og_ref, ng_ref, *rest):
    o_refs = rest[:N_GROUPS]
    l_refs = rest[N_GROUPS:2 * N_GROUPS]
    unperm_refs = rest[2 * N_GROUPS:2 * N_GROUPS + N_DIL]
    e_ref, wa_ref, wb_ref, wo_ref, out_ref = rest[2 * N_GROUPS + N_DIL:]
    rb = MERGE_RB

    for c in range(x_ref.shape[0] // rb):
        rows = slice(c * rb, (c + 1) * rb)
        outs, lses = [], []
        for g in range(N_GROUPS):
            d = o_refs[g].shape[0]
            if g == 0:
                o = o_refs[g][:, c * rb:(c + 1) * rb].reshape(rb, A_GW).astype(F32)
                lp = l_refs[g][:, c * rb:(c + 1) * rb].reshape(rb, LANES).astype(F32)
            else:
                per = UNPERM_RB // d
                parts = []
                for cc in range(c * rb // UNPERM_RB, (c + 1) * rb // UNPERM_RB):
                    o = o_refs[g][:, cc * per:(cc + 1) * per].reshape(UNPERM_RB, A_GW)
                    lp = l_refs[g][:, cc * per:(cc + 1) * per].reshape(UNPERM_RB, LANES)
                    parts.append(jnp.dot(unperm_refs[g - 1][...],
                                         jnp.concatenate([o, lp], axis=1),
                                         preferred_element_type=F32))
                both = jnp.concatenate(parts, axis=0)
                o, lp = both[:, :A_GW], both[:, A_GW:]
            outs.append(o)
            lses.append(sum(lp if part == 0 else pltpu.roll(lp, LANES - part * SUBLANES, 1)
                            for part in range(LSE_PARTS)))

        lm = functools.reduce(jnp.maximum, lses)
        es = [jnp.exp(l - lm) for l in lses]
        den = functools.reduce(jnp.add, es)
        yb = jnp.zeros((rb, A_GW), F32)
        for e, o in zip(es, outs):
            wide = jnp.dot((e / den).astype(BF16), e_ref[...], preferred_element_type=F32)
            yb = yb + wide * o
        heads = []
        for h in range(M_HEADS):
            hm = hm_ref[rows, h * M_DH:(h + 1) * M_DH]
            cen = hm - jnp.mean(hm, axis=1, keepdims=True)
            var = jnp.mean(cen * cen, axis=1, keepdims=True)
            heads.append(cen * lax.rsqrt(var + EPS))
        ya = jnp.concatenate(heads, axis=1) * ng_ref[...] * og_ref[rows].astype(F32)
        pa = jnp.dot(ya.astype(BF16), wa_ref[...], preferred_element_type=F32)
        pb = jnp.dot(yb.astype(BF16), wb_ref[...], preferred_element_type=F32)
        ga = gate_ref[rows, :D_MODEL].astype(F32)
        gb = gate_ref[rows, D_MODEL:].astype(F32)
        merged = (ga * pa + gb * pb).astype(BF16)
        out_ref[rows] = x_ref[rows] + jnp.dot(merged, wo_ref[...], preferred_element_type=F32)


def _merge(x2, p_flat, hm, ng, os_, ls_, wa, wb, wo, S, tm=1024):
    n = x2.shape[0]
    nt = S // tm
    row = lambda w: pl.BlockSpec((tm, w), lambda i: (i, 0))
    full = lambda a: pl.BlockSpec(a.shape, lambda i: (0, 0))
    grp = lambda a: pl.BlockSpec((None, a.shape[1], tm // a.shape[1], a.shape[3]),
                                 lambda i: (i // nt, 0, i % nt, 0))
    unperms = [jnp.asarray(_deinterleave_matrix(UNPERM_RB, d).T, BF16) for _, d in GROUPS[1:]]
    head_lane = (np.arange(A_HG) % 2) * A_DH + np.arange(A_HG) // 2
    expand = jnp.asarray(
        np.arange(LANES)[:, None] == head_lane[np.arange(A_GW) // A_DH][None, :], BF16)
    consts = unperms + [expand, wa, wb, wo]
    return pl.pallas_call(
        _merge_kernel,
        grid=(n // tm,),
        in_specs=[row(D_MODEL),
                  pl.BlockSpec((tm, 2 * D_MODEL), lambda i: (i, P_GATE // (2 * D_MODEL))),
                  row(M_WIDTH),
                  pl.BlockSpec((tm, M_WIDTH), lambda i: (i, P_OM // M_WIDTH)),
                  pl.BlockSpec((1, M_WIDTH), lambda i: (0, 0))]
                 + [grp(a) for a in os_] + [grp(a) for a in ls_] + [full(a) for a in consts],
        out_specs=row(D_MODEL),
        out_shape=jax.ShapeDtypeStruct((n, D_MODEL), F32),
        compiler_params=pltpu.CompilerParams(
            dimension_semantics=("parallel",), vmem_limit_bytes=VMEM_LIMIT),
        name="merge",
    )(x2, p_flat, hm, p_flat, ng, *os_, *ls_, *consts)


FF_CHUNKS = ((0, 1024), (1024, 1024), (2048, 768))
FF_RB = 256


def _rms(x, g):
    return x * lax.rsqrt(jnp.mean(x * x, axis=-1, keepdims=True) + EPS) * g


def _ffn_kernel(x_ref, gf_ref, wg_ref, wu_ref, wd_ref, gl_ref, out_ref):
    for rc in range(x_ref.shape[0] // FF_RB):
        rows = slice(rc * FF_RB, (rc + 1) * FF_RB)
        x = x_ref[rows]
        hf = _rms(x, gf_ref[...]).astype(BF16)
        acc = x
        for start, size in FF_CHUNKS:
            cs = slice(start, start + size)
            gt = jnp.dot(hf, wg_ref[:, cs], preferred_element_type=F32)
            up = jnp.dot(hf, wu_ref[:, cs], preferred_element_type=F32)
            act = (gt * jax.nn.sigmoid(gt) * up).astype(BF16)
            acc = acc + jnp.dot(act, wd_ref[cs, :], preferred_element_type=F32)
        out_ref[rows] = _rms(acc, gl_ref[...])


def _ffn(x1, gf, wg, wu, wd, gl, tm=1024):
    n = x1.shape[0]
    row = pl.BlockSpec((tm, D_MODEL), lambda i: (i, 0))
    once = lambda a: pl.BlockSpec(a.shape, lambda i: (0, 0), pipeline_mode=pl.Buffered(1))
    return pl.pallas_call(
        _ffn_kernel,
        grid=(n // tm,),
        in_specs=[row, once(gf), once(wg), once(wu), once(wd), once(gl)],
        out_specs=row,
        out_shape=jax.ShapeDtypeStruct((n, D_MODEL), F32),
        compiler_params=pltpu.CompilerParams(
            dimension_semantics=("parallel",), vmem_limit_bytes=VMEM_LIMIT),
        name="ffn",
    )(x1, gf, wg, wu, wd, gl)


def _layer(x2, B, S, bias, norm_mix_g, w_in, b_gate_if, conv_w, conv_b, mlstm_norm_g,
           w_proj_a, w_proj_b, w_out, norm_ffn_g, w_gate, w_up, w_down, final_g):
    sizes = (M_WIDTH, M_WIDTH, M_WIDTH, 2 * M_HEADS, M_WIDTH, A_WIDTH, A_WIDTH, A_WIDTH,
             2 * D_MODEL)
    offs = np.concatenate([[0], np.cumsum(sizes)])
    piece = lambda i: w_in[:, offs[i]:offs[i + 1]]
    qkv = lambda g: [piece(i)[:, g * A_GW:(g + 1) * A_GW] * sc
                     for i, sc in ((5, QK_SCALE), (6, 1.0), (7, 1.0))]
    w_p = jnp.concatenate([w_in[:, :offs[3]], piece(4), piece(8)]
                          + [w for g in range(N_GROUPS) for w in qkv(g)], axis=1).astype(BF16).T
    w_if = jnp.concatenate(_split_bf16(piece(3).T, 2), axis=0)

    p_flat, if_t, *dilated = _inproj(x2, norm_mix_g[None], w_p, w_if, B, S)

    bif = jnp.broadcast_to(b_gate_if[:, None], (SUBLANES, LANES))
    gates = _gates(if_t, bif, B, S)
    hm = _mlstm(p_flat.reshape(B, S, P_WIDTH), gates, conv_w,
                conv_b[None]).reshape(B * S, M_WIDTH)

    os_, ls_ = [], []
    for g in range(N_GROUPS):
        if g == 0:
            o, lse = _attn_group(p_flat.reshape(B, 1, S, P_WIDTH), P_A0 // A_GW, bias, g)
        else:
            o, lse = _attn_group(dilated[g - 1], 0, bias, g)
        os_.append(o)
        ls_.append(lse)

    x1 = _merge(x2, p_flat, hm, mlstm_norm_g[None], os_, ls_, w_proj_a.astype(BF16),
                w_proj_b.astype(BF16), w_out.astype(BF16), S)
    return _ffn(x1, norm_ffn_g[None], w_gate.astype(BF16), w_up.astype(BF16),
                w_down.astype(BF16), final_g[None])


def kernel(x, norm_mix_g, w_in, b_gate_if, conv_w, conv_b, mlstm_norm_g, w_proj_a, w_proj_b,
           w_out, norm_ffn_g, w_gate, w_up, w_down, rel_bias, norm_final_g):
    B, S, _ = x.shape
    depth = w_in.shape[0]
    assert depth == 1, "the final norm is fused into the (single) layer's channel mixer"
    bias = _bias_tables(rel_bias)
    out = _layer(x.reshape(B * S, D_MODEL), B, S, bias, norm_mix_g[0], w_in[0], b_gate_if[0],
                 conv_w[0], conv_b[0], mlstm_norm_g[0], w_proj_a[0], w_proj_b[0], w_out[0],
                 norm_ffn_g[0], w_gate[0], w_up[0], w_down[0], norm_final_g)
    return out.reshape(B, S, D_MODEL)
```

```python
import functools
import math

import numpy as np
import jax
import jax.numpy as jnp
from jax import lax
from jax.experimental import pallas as pl
from jax.experimental.pallas import tpu as pltpu

F32 = jnp.float32
BF16 = jnp.bfloat16

D_MODEL = 1024
M_HEADS = 4
M_DH = 256
M_WIDTH = M_HEADS * M_DH
CONV_K = 4
CHUNK = 128
GROUPS = ((128, 1), (512, 4), (2048, 16))
N_GROUPS = len(GROUPS)
A_HG = 8
A_DH = 64
A_GW = A_HG * A_DH
A_WIDTH = N_GROUPS * A_GW
A_BLK = 128
N_BUCKETS = 32
MAX_DISTANCE = 2048
D_FF = 2816
EPS = 1e-6
NEG = -1e30
LOG2E = math.log2(math.e)
QK_SCALE = A_DH ** -0.5 * LOG2E
LSE_PARTS = 3

LANES = 128
SUBLANES = 8

P_QM = 0
P_KM = 1024
P_VM = 2048
P_OM = 3072
P_GATE = 4096
P_A0 = 6144
P_WIDTH = P_A0 + 3 * A_GW
IN_TN = 3 * A_GW
N_DIL = N_GROUPS - 1

VMEM_LIMIT = 56 * 1024 * 1024


def _split_bf16(x, parts):
    out = []
    for _ in range(parts):
        hi = x.astype(BF16)
        out.append(hi)
        x = x - hi.astype(F32)
    return out


def _bucket_tables():
    i = np.arange(A_BLK)[:, None]
    j = np.arange(2 * A_BLK)[None, :]
    dist = i + A_BLK - j
    buckets = []
    for window, dil in GROUPS:
        n = np.maximum(dist, 0) * dil
        nf = np.maximum(n, 1).astype(np.float32)
        max_exact = N_BUCKETS // 2
        large = max_exact + (np.log(nf / max_exact) / math.log(MAX_DISTANCE / max_exact)
                             * (N_BUCKETS - max_exact)).astype(np.int32)
        large = np.minimum(large, N_BUCKETS - 1)
        buckets.append(np.where(n < max_exact, n, large).astype(np.int32))
    span = GROUPS[0][0] // GROUPS[0][1]
    assert all(w // d == span for w, d in GROUPS)
    valid = ((dist >= 0) & (dist <= span)).astype(np.int32)
    valid_first = (valid.astype(bool) & (j >= A_BLK)).astype(np.int32)
    return np.stack(buckets), np.stack([valid, valid_first])


def _bias_kernel(tab_ref, bucket_ref, valid_ref, out_ref):
    g = pl.program_id(0)
    bucket = bucket_ref[...]
    for h in range(A_HG):
        acc = jnp.zeros(bucket.shape, F32)
        for b in range(N_BUCKETS):
            acc = jnp.where(bucket == b, tab_ref[b, g * A_HG + h], acc)
        acc = acc * LOG2E
        out_ref[0, h] = jnp.where(valid_ref[0] > 0, acc, NEG)
        out_ref[1, h] = jnp.where(valid_ref[1] > 0, acc, NEG)


def _bias_tables(rel_bias):
    buckets, valid = _bucket_tables()
    return pl.pallas_call(
        _bias_kernel,
        grid=(N_GROUPS,),
        in_specs=[
            pl.BlockSpec(memory_space=pltpu.SMEM),
            pl.BlockSpec((None, A_BLK, 2 * A_BLK), lambda g: (g, 0, 0)),
            pl.BlockSpec((2, A_BLK, 2 * A_BLK), lambda g: (0, 0, 0)),
        ],
        out_specs=pl.BlockSpec((None, 2, A_HG, A_BLK, 2 * A_BLK), lambda g: (g, 0, 0, 0, 0)),
        out_shape=jax.ShapeDtypeStruct((N_GROUPS, 2, A_HG, A_BLK, 2 * A_BLK), F32),
        name="bias_tables",
    )(rel_bias, jnp.asarray(buckets), jnp.asarray(valid))


def _deinterleave_matrix(rows, d):
    out = np.arange(rows)
    src = d * (out % (rows // d)) + out // (rows // d)
    return (src[:, None] == np.arange(rows)[None, :]).astype(np.float32)


PERM_BLK = 256
NORM_RB = 256


def _inproj_kernel(x_ref, g_ref, w_ref, wif_ref, *rest, n_nat, dils):
    perm_refs = rest[:N_DIL]
    p_ref, if_ref = rest[N_DIL:N_DIL + 2]
    a_refs = rest[N_DIL + 2:2 * N_DIL + 2]
    h_ref, hp_ref = rest[-2:]
    tm = x_ref.shape[0]
    j = pl.program_id(1)
    nt_dims = (((1,), (1,)), ((), ()))

    @pl.when(j == 0)
    def _():
        for rc in range(tm // NORM_RB):
            rows = slice(rc * NORM_RB, (rc + 1) * NORM_RB)
            x = x_ref[rows]
            r = lax.rsqrt(jnp.mean(x * x, axis=-1, keepdims=True) + EPS)
            h = x * r * g_ref[...]
            h_hi, h_lo = _split_bf16(h, 2)
            h_ref[rows] = h_hi
            acc = (lax.dot_general(wif_ref[...], h_hi, nt_dims, preferred_element_type=F32)
                   + lax.dot_general(wif_ref[...], h_lo, nt_dims, preferred_element_type=F32))
            if_ref[:, rows] = acc[:SUBLANES] + acc[SUBLANES:]
            acc = lax.dot_general(h_hi, w_ref[...], nt_dims, preferred_element_type=F32)
            p_ref[rows] = acc.astype(BF16)

    is_gate = (j >= P_OM // IN_TN) & (j < P_A0 // IN_TN)

    @pl.when((j > 0) & (j < n_nat) & jnp.logical_not(is_gate))
    def _():
        acc = lax.dot_general(h_ref[...], w_ref[...], nt_dims, preferred_element_type=F32)
        p_ref[...] = acc.astype(BF16)

    @pl.when(is_gate)
    def _():
        acc = lax.dot_general(h_ref[...], w_ref[...], nt_dims, preferred_element_type=F32)
        p_ref[...] = (0.5 * jnp.tanh(0.5 * acc) + 0.5).astype(BF16)

    for idx, d in enumerate(dils):
        @pl.when(j == n_nat + idx)
        def _(perm_ref=perm_refs[idx], a_ref=a_refs[idx], d=d):
            piece = PERM_BLK // d
            for c in range(tm // PERM_BLK):
                hp = jnp.dot(perm_ref[...], h_ref[c * PERM_BLK:(c + 1) * PERM_BLK],
                             preferred_element_type=F32).astype(BF16)
                for r in range(d):
                    dst = r * (tm // d) + c * piece
                    hp_ref[dst:dst + piece] = hp[r * piece:(r + 1) * piece]
            a = lax.dot_general(hp_ref[...], w_ref[...], nt_dims, preferred_element_type=F32)
            a_ref[...] = a.astype(BF16).reshape(a_ref.shape)


def _inproj(x2, g, w_p, w_if, B, S, tm=1024):
    n = x2.shape[0]
    nt = S // tm
    n_nat = P_WIDTH // IN_TN
    dils = tuple(d for _, d in GROUPS[1:])
    perms = [jnp.asarray(_deinterleave_matrix(PERM_BLK, d), BF16) for d in dils]
    return pl.pallas_call(
        functools.partial(_inproj_kernel, n_nat=n_nat, dils=dils),
        grid=(n // tm, n_nat + N_DIL),
        in_specs=[
            pl.BlockSpec((tm, D_MODEL), lambda i, j: (i, 0)),
            pl.BlockSpec((1, D_MODEL), lambda i, j: (0, 0)),
            pl.BlockSpec((IN_TN, D_MODEL), lambda i, j: (j, 0)),
            pl.BlockSpec((2 * SUBLANES, D_MODEL), lambda i, j: (0, 0)),
        ] + [pl.BlockSpec((PERM_BLK, PERM_BLK), lambda i, j: (0, 0)) for _ in dils],
        out_specs=[
            pl.BlockSpec((tm, IN_TN), lambda i, j: (i, jnp.minimum(j, n_nat - 1))),
            pl.BlockSpec((SUBLANES, tm), lambda i, j: (0, i)),
        ] + [pl.BlockSpec((None, d, tm // d, IN_TN), lambda i, j: (i // nt, 0, i % nt, 0))
             for d in dils],
        out_shape=[
            jax.ShapeDtypeStruct((n, P_WIDTH), BF16),
            jax.ShapeDtypeStruct((SUBLANES, n), F32),
        ] + [jax.ShapeDtypeStruct((B, d, S // d, IN_TN), BF16) for d in dils],
        scratch_shapes=[pltpu.VMEM((tm, D_MODEL), BF16), pltpu.VMEM((tm, D_MODEL), BF16)],
        compiler_params=pltpu.CompilerParams(
            dimension_semantics=("parallel", "arbitrary"), vmem_limit_bytes=VMEM_LIMIT),
        name="inproj",
    )(x2, g, w_p, w_if, *perms)


def _conv_shift_matrix(L):
    return np.concatenate([np.eye(L, k=-(CONV_K - 1 - j)) for j in range(CONV_K)], axis=1)


def _conv_silu(x_ref, tail_ref, shift_ref, cw, cb):
    L, C = x_ref.shape
    pack = 2 * SUBLANES
    x3 = x_ref[...].reshape(L // pack, pack, C)
    prods = []
    for j in range(CONV_K):
        wj = jnp.broadcast_to(cw[j:j + 1], (pack, C)).astype(BF16)
        prods.append((x3 * wj[None]).reshape(L, C))
    y = jnp.dot(shift_ref[...], jnp.concatenate(prods, axis=0),
                preferred_element_type=F32) + cb
    tail = tail_ref[...]
    row = lax.broadcasted_iota(jnp.int32, tail.shape, 0)
    fix = jnp.zeros(tail.shape, F32)
    for k in range(1, CONV_K):
        tap = pltpu.roll(tail, k, 0) * cw[CONV_K - 1 - k:CONV_K - k]
        fix = fix + jnp.where(row < k, tap, 0.0)
    y = jnp.concatenate([y[:SUBLANES] + fix, y[SUBLANES:]], axis=0)
    tail_ref[...] = x_ref[L - pack:].astype(F32)[pack - SUBLANES:]
    return y * jax.nn.sigmoid(y)


GATE_ROWS = 32


def _gates_kernel(ift_ref, bif_ref, out_ref):
    nseq, _, S = out_ref.shape
    nc = S // CHUNK
    g = ift_ref[...] + jnp.concatenate([bif_ref[...]] * (nseq * nc), axis=1)
    ig = g[:M_HEADS]
    lf = jax.nn.log_sigmoid(g[M_HEADS:])
    pos = lax.broadcasted_iota(jnp.int32, lf.shape, 1) % CHUNK

    def scan(x, op, fill):
        sh = 1
        while sh < CHUNK:
            x = op(x, jnp.where(pos >= sh, pltpu.roll(x, sh, 1), fill))
            sh *= 2
        return x

    def last(x):
        x = jnp.where(pos == CHUNK - 1, x, NEG)
        sh = 1
        while sh < CHUNK:
            x = jnp.maximum(x, jnp.where(pos < CHUNK - sh, pltpu.roll(x, nseq * S - sh, 1), NEG))
            sh *= 2
        return x

    b = scan(lf, jnp.add, 0.0)
    a = ig - b
    b_last = last(b)
    a_end = b_last + a
    a_max = last(scan(a_end, jnp.maximum, NEG))
    m_prev = []
    for c in range(nseq * nc):
        if c % nc == 0:
            m = jnp.zeros((M_HEADS, CHUNK), F32)
        m_prev.append(m)
        m = jnp.maximum(b_last[:, c * CHUNK:(c + 1) * CHUNK] + m,
                        a_max[:, c * CHUNK:(c + 1) * CHUNK])
    m_prev = jnp.concatenate(m_prev, axis=1)
    m_new = jnp.maximum(b_last + m_prev, a_max)
    mx = jnp.maximum(m_prev, scan(a, jnp.maximum, NEG))
    rows = jnp.concatenate(
        [-mx, jnp.exp(m_prev - mx), jnp.exp(-(b + mx)), jnp.exp(a_end - m_new), a,
         jnp.exp(b_last + m_prev - m_new),
         jnp.zeros((GATE_ROWS - 6 * M_HEADS, nseq * S), F32)], axis=0)
    for s in range(nseq):
        out_ref[s] = rows[:, s * S:(s + 1) * S]


def _gates(ift, bif, B, S, nseq=8):
    nseq = nseq if B % nseq == 0 else 1
    return pl.pallas_call(
        _gates_kernel,
        grid=(B // nseq,),
        in_specs=[pl.BlockSpec((SUBLANES, nseq * S), lambda b: (0, b)),
                  pl.BlockSpec((SUBLANES, LANES), lambda b: (0, 0))],
        out_specs=pl.BlockSpec((nseq, GATE_ROWS, S), lambda b: (b, 0, 0)),
        out_shape=jax.ShapeDtypeStruct((B, GATE_ROWS, S), F32),
        compiler_params=pltpu.CompilerParams(dimension_semantics=("parallel",)),
        name="gates",
    )(ift, bif)


def _mlstm_kernel(q_ref, k_ref, v_ref, g_ref, cw_ref, cb_ref, shift_ref,
                  y_ref, c_ref, n_ref, qt_ref, kt_ref):
    L = CHUNK
    nt_dims = (((1,), (1,)), ((), ()))

    @pl.when(pl.program_id(1) == 0)
    def _():
        c_ref[...] = jnp.zeros_like(c_ref)
        n_ref[...] = jnp.zeros_like(n_ref)
        qt_ref[...] = jnp.zeros_like(qt_ref)
        kt_ref[...] = jnp.zeros_like(kt_ref)

    cw = cw_ref[...]
    cb = cb_ref[...]
    tri = (lax.broadcasted_iota(jnp.int32, (L, L), 0)
           >= lax.broadcasted_iota(jnp.int32, (L, L), 1))

    units = []
    for s in range(q_ref.shape[0]):
        q_all = _conv_silu(q_ref.at[s], qt_ref.at[s], shift_ref, cw[:, :M_WIDTH], cb[:, :M_WIDTH])
        q_all = (q_all * (M_DH ** -0.5)).astype(BF16)
        k_all = _conv_silu(k_ref.at[s], kt_ref.at[s], shift_ref, cw[:, M_WIDTH:], cb[:, M_WIDTH:])
        rows = g_ref[s]
        cols = jnp.transpose(rows)
        for h in range(M_HEADS):
            sl = slice(h * M_DH, (h + 1) * M_DH)
            col = lambda i: cols[:, i * M_HEADS + h:i * M_HEADS + h + 1]
            units.append(dict(
                s=s, h=h, sl=sl, qb=q_all[:, sl], k=k_all[:, sl], vb=v_ref[s, :, sl],
                u_col=col(0), wi_col=col(1), en_col=col(2), wk_col=col(3),
                a_row=rows[4 * M_HEADS + h:4 * M_HEADS + h + 1],
                dec_row=rows[5 * M_HEADS + h:5 * M_HEADS + h + 1]))

    for u in units:
        c_old = c_ref[u["s"], u["h"]]
        n_old = n_ref[u["s"], u["h"]:u["h"] + 1]
        kb = u["k"].astype(BF16)
        kn = jnp.concatenate([kb, jnp.broadcast_to(n_old, (L, M_DH)).astype(BF16)], axis=0)
        u["s_aug"] = lax.dot_general(u["qb"], kn, nt_dims, preferred_element_type=F32)
        u["qc"] = jnp.dot(u["qb"], c_old.astype(BF16), preferred_element_type=F32)
        kw = u["k"] * u["wk_col"]
        u["kw"] = kw.astype(BF16)
        dec = jnp.concatenate([u["dec_row"], u["dec_row"]], axis=1)
        u["c_dec"] = dec * c_old
        n_ref[u["s"], u["h"]:u["h"] + 1] = dec * n_old + jnp.sum(kw, axis=0, keepdims=True)

    for u in units:
        dmat = jnp.where(tri, jnp.exp(u["u_col"] + u["a_row"]), 0.0)
        w_intra = dmat * u["s_aug"][:, :L]
        u["w_sum"] = jnp.sum(w_intra, axis=1, keepdims=True)
        u["w_intra"] = w_intra.astype(BF16)

    for u in units:
        u["pv"] = jnp.dot(u["w_intra"], u["vb"], preferred_element_type=F32)
        c_ref[u["s"], u["h"]] = u["c_dec"] + lax.dot_general(
            u["kw"], u["vb"], (((0,), (0,)), ((), ())), preferred_element_type=F32)

    for u in units:
        num = u["wi_col"] * u["qc"] + u["pv"]
        den = u["wi_col"] * u["s_aug"][:, L:L + 1] + u["w_sum"]
        y_ref[u["s"], :, u["sl"]] = num / jnp.maximum(jnp.abs(den), u["en_col"])


def _mlstm(p3, gates, conv_w, conv_b, nseq=8):
    B, S, _ = p3.shape
    nseq = nseq if B % nseq == 0 else 1
    wblk = lambda col: pl.BlockSpec((nseq, CHUNK, M_WIDTH), lambda b, c: (b, c, col // M_WIDTH))
    full = lambda shape: pl.BlockSpec(shape, lambda b, c: (0,) * len(shape))
    return pl.pallas_call(
        _mlstm_kernel,
        grid=(B // nseq, S // CHUNK),
        in_specs=[
            wblk(P_QM), wblk(P_KM), wblk(P_VM),
            pl.BlockSpec((nseq, GATE_ROWS, CHUNK), lambda b, c: (b, 0, c)),
            full((CONV_K, 2 * M_WIDTH)), full((1, 2 * M_WIDTH)),
            full((CHUNK, CONV_K * CHUNK)),
        ],
        out_specs=pl.BlockSpec((nseq, CHUNK, M_WIDTH), lambda b, c: (b, c, 0)),
        out_shape=jax.ShapeDtypeStruct((B, S, M_WIDTH), F32),
        scratch_shapes=[
            pltpu.VMEM((nseq, M_HEADS, M_DH, M_DH), F32),
            pltpu.VMEM((nseq, SUBLANES, M_DH), F32),
            pltpu.VMEM((nseq, SUBLANES, M_WIDTH), F32),
            pltpu.VMEM((nseq, SUBLANES, M_WIDTH), F32),
        ],
        compiler_params=pltpu.CompilerParams(
            dimension_semantics=("parallel", "arbitrary"), vmem_limit_bytes=VMEM_LIMIT),
        name="mlstm",
    )(p3, p3, p3, gates, conv_w, conv_b, jnp.asarray(_conv_shift_matrix(CHUNK), BF16))


def _attn_kernel(q_ref, kc_ref, kp_ref, vc_ref, vp_ref, bias_ref, o_ref, lse_ref, *, tq):
    for r in range(q_ref.shape[0]):
        _attn_rows(q_ref.at[r], kc_ref.at[r], kp_ref.at[r], vc_ref.at[r], vp_ref.at[r], bias_ref,
                   o_ref.at[r], lse_ref.at[r], tq=tq)


def _attn_rows(q_ref, kc_ref, kp_ref, vc_ref, vp_ref, bias_ref, o_ref, lse_ref, *, tq):
    first = (pl.program_id(2) == 0).astype(jnp.int32)
    lane = lax.broadcasted_iota(jnp.int32, (1, LANES), 1)
    low = lane < A_DH
    nt_dims = (((1,), (1,)), ((), ()))
    n_pairs = A_HG // 2
    krow = lax.broadcasted_iota(jnp.int32, (4 * A_BLK, LANES), 0)
    ones_blk = jnp.where((krow < 2 * A_BLK) == low, 1.0, 0.0).astype(BF16)
    low_w = lax.broadcasted_iota(jnp.int32, (1, A_GW), 1) % LANES < A_DH
    k_all = jnp.concatenate([kp_ref[...], kc_ref[...]], axis=0)
    v_all = jnp.concatenate([vp_ref[...], vc_ref[...]], axis=0)
    q_all = q_ref[...]
    zq, zv = jnp.zeros_like(q_all), jnp.zeros_like(v_all)
    q_heads = (jnp.where(low_w, q_all, zq), jnp.where(low_w, zq, q_all))
    v_heads = (jnp.where(low_w, v_all, zv), jnp.where(low_w, zv, v_all))
    for i in range(tq // A_BLK):
        rows = slice(i * A_BLK, (i + 1) * A_BLK)
        keys = slice(i * A_BLK, (i + 2) * A_BLK)
        k2 = k_all[keys]
        var = first if i == 0 else 0

        scores = []
        for p in range(n_pairs):
            cs = slice(p * LANES, (p + 1) * LANES)
            q2 = jnp.concatenate([q_heads[0][rows, cs], q_heads[1][rows, cs]], axis=0)
            scores.append(lax.dot_general(q2, k2[:, cs], nt_dims, preferred_element_type=F32))

        probs, maxes = [], []
        for p in range(n_pairs):
            for e in range(2):
                s = scores[p][e * A_BLK:(e + 1) * A_BLK] + bias_ref[var, 2 * p + e]
                mx = jnp.max(s, axis=1, keepdims=True)
                probs.append(jnp.exp2(s - mx).astype(BF16))
                maxes.append(mx)

        lse_blk = jnp.zeros((A_BLK, LANES), F32)
        for p in range(n_pairs):
            cs = slice(p * LANES, (p + 1) * LANES)
            v_cat = jnp.concatenate([v_heads[0][keys, cs], v_heads[1][keys, cs]],
                                    axis=0)
            p_cat = jnp.concatenate([probs[2 * p], probs[2 * p + 1]], axis=1)
            o_ext = jnp.dot(p_cat, jnp.concatenate([v_cat, ones_blk], axis=1),
                            preferred_element_type=F32)
            l_pair = o_ext[:, LANES:]
            o_ref[rows, cs] = (o_ext[:, :LANES] * (1.0 / l_pair)).astype(BF16)
            mx_pair = jnp.where(low, maxes[2 * p], maxes[2 * p + 1])
            lse_pair = mx_pair * (1.0 / LOG2E) + jnp.log(l_pair)
            lse_blk = jnp.where(lane % A_DH == p, lse_pair, lse_blk)

        packed = jnp.zeros((A_BLK, LANES), F32)
        rest = lse_blk
        for part in range(LSE_PARTS):
            hi = rest.astype(BF16).astype(F32)
            rest = rest - hi
            packed = packed + (hi if part == 0 else pltpu.roll(hi, part * SUBLANES, 1))
        lse_ref[rows] = packed.astype(BF16)


def _attn_group(src, col0, bias, g, rows=2048):
    B, d, sd, _ = src.shape
    tq = min(rows, sd)
    nres = min(rows // tq, d)
    cur = lambda c: pl.BlockSpec((None, nres, tq, A_GW), lambda b, r, n: (b, r, n, col0 + c))
    prev = lambda c: pl.BlockSpec(
        (None, nres, A_BLK, A_GW),
        lambda b, r, n: (b, r, jnp.maximum(n * (tq // A_BLK) - 1, 0), col0 + c))
    return pl.pallas_call(
        functools.partial(_attn_kernel, tq=tq),
        grid=(B, d // nres, sd // tq),
        in_specs=[
            cur(0), cur(1), prev(1), cur(2), prev(2),
            pl.BlockSpec((None, 2, A_HG, A_BLK, 2 * A_BLK), lambda b, r, n: (g, 0, 0, 0, 0)),
        ],
        out_specs=[
            pl.BlockSpec((None, nres, tq, A_GW), lambda b, r, n: (b, r, n, 0)),
            pl.BlockSpec((None, nres, tq, LANES), lambda b, r, n: (b, r, n, 0)),
        ],
        out_shape=[
            jax.ShapeDtypeStruct((B, d, sd, A_GW), BF16),
            jax.ShapeDtypeStruct((B, d, sd, LANES), BF16),
        ],
        compiler_params=pltpu.CompilerParams(
            dimension_semantics=("parallel", "parallel", "arbitrary"),
            vmem_limit_bytes=VMEM_LIMIT),
        name=f"attn_g{g}",
    )(src, src, src, src, src, bias)


MERGE_RB = 1024
UNPERM_RB = 256


def _merge_kernel(x_ref, gate_ref, hm_ref, og_ref, ng_ref, *rest):
    o_refs = rest[:N_GROUPS]
    l_refs = rest[N_GROUPS:2 * N_GROUPS]
    unperm_refs = rest[2 * N_GROUPS:2 * N_GROUPS + N_DIL]
    e_ref, wa_ref, wb_ref, wo_ref, out_ref = rest[2 * N_GROUPS + N_DIL:]
    rb = MERGE_RB

    for c in range(x_ref.shape[0] // rb):
        rows = slice(c * rb, (c + 1) * rb)
        outs, lses = [], []
        for g in range(N_GROUPS):
            d = o_refs[g].shape[0]
            if g == 0:
                o = o_refs[g][:, c * rb:(c + 1) * rb].reshape(rb, A_GW).astype(F32)
                lp = l_refs[g][:, c * rb:(c + 1) * rb].reshape(rb, LANES).astype(F32)
            else:
                per = UNPERM_RB // d
                parts = []
                for cc in range(c * rb // UNPERM_RB, (c + 1) * rb // UNPERM_RB):
                    o = o_refs[g][:, cc * per:(cc + 1) * per].reshape(UNPERM_RB, A_GW)
                    lp = l_refs[g][:, cc * per:(cc + 1) * per].reshape(UNPERM_RB, LANES)
                    parts.append(jnp.dot(unperm_refs[g - 1][...],
                                         jnp.concatenate([o, lp], axis=1),
                                         preferred_element_type=F32))
                both = jnp.concatenate(parts, axis=0)
                o, lp = both[:, :A_GW], both[:, A_GW:]
            outs.append(o)
            lses.append(sum(lp if part == 0 else pltpu.roll(lp, LANES - part * SUBLANES, 1)
                            for part in range(LSE_PARTS)))

        lm = functools.reduce(jnp.maximum, lses)
        es = [jnp.exp(l - lm) for l in lses]
        den = functools.reduce(jnp.add, es)
        yb = jnp.zeros((rb, A_GW), F32)
        for e, o in zip(es, outs):
            wide = jnp.dot((e / den).astype(BF16), e_ref[...], preferred_element_type=F32)
            yb = yb + wide * o
        pieces = []
        for r0 in range(c * rb, (c + 1) * rb, UNPERM_RB):
            sub = slice(r0, r0 + UNPERM_RB)
            heads = []
            for h in range(M_HEADS):
                hm = hm_ref[sub, h * M_DH:(h + 1) * M_DH]
                cen = hm - jnp.mean(hm, axis=1, keepdims=True)
                var = jnp.mean(cen * cen, axis=1, keepdims=True)
                heads.append(cen * lax.rsqrt(var + EPS))
            ya = jnp.concatenate(heads, axis=1) * ng_ref[...] * og_ref[sub].astype(F32)
            pieces.append(ya.astype(BF16))
        pa = jnp.dot(jnp.concatenate(pieces, axis=0), wa_ref[...], preferred_element_type=F32)
        pb = jnp.dot(yb.astype(BF16), wb_ref[...], preferred_element_type=F32)
        ga = gate_ref[rows, :D_MODEL].astype(F32)
        gb = gate_ref[rows, D_MODEL:].astype(F32)
        merged = (ga * pa + gb * pb).astype(BF16)
        out_ref[rows] = x_ref[rows] + jnp.dot(merged, wo_ref[...], preferred_element_type=F32)


def _merge(x2, p_flat, hm, ng, os_, ls_, wa, wb, wo, S, tm=1024):
    n = x2.shape[0]
    nt = S // tm
    row = lambda w: pl.BlockSpec((tm, w), lambda i: (i, 0))
    full = lambda a: pl.BlockSpec(a.shape, lambda i: (0, 0))
    grp = lambda a: pl.BlockSpec((None, a.shape[1], tm // a.shape[1], a.shape[3]),
                                 lambda i: (i // nt, 0, i % nt, 0))
    unperms = [jnp.asarray(_deinterleave_matrix(UNPERM_RB, d).T, BF16) for _, d in GROUPS[1:]]
    head_lane = (np.arange(A_HG) % 2) * A_DH + np.arange(A_HG) // 2
    expand = jnp.asarray(
        np.arange(LANES)[:, None] == head_lane[np.arange(A_GW) // A_DH][None, :], BF16)
    consts = unperms + [expand, wa, wb, wo]
    return pl.pallas_call(
        _merge_kernel,
        grid=(n // tm,),
        in_specs=[row(D_MODEL),
                  pl.BlockSpec((tm, 2 * D_MODEL), lambda i: (i, P_GATE // (2 * D_MODEL))),
                  row(M_WIDTH),
                  pl.BlockSpec((tm, M_WIDTH), lambda i: (i, P_OM // M_WIDTH)),
                  pl.BlockSpec((1, M_WIDTH), lambda i: (0, 0))]
                 + [grp(a) for a in os_] + [grp(a) for a in ls_] + [full(a) for a in consts],
        out_specs=row(D_MODEL),
        out_shape=jax.ShapeDtypeStruct((n, D_MODEL), F32),
        compiler_params=pltpu.CompilerParams(
            dimension_semantics=("parallel",), vmem_limit_bytes=VMEM_LIMIT),
        name="merge",
    )(x2, p_flat, hm, p_flat, ng, *os_, *ls_, *consts)


FF_CHUNKS = ((0, 1024), (1024, 1024), (2048, 768))
FF_RB = 256


def _rms(x, g):
    return x * lax.rsqrt(jnp.mean(x * x, axis=-1, keepdims=True) + EPS) * g


def _ffn_kernel(x_ref, gf_ref, wg_ref, wu_ref, wd_ref, gl_ref, out_ref):
    for rc in range(x_ref.shape[0] // FF_RB):
        rows = slice(rc * FF_RB, (rc + 1) * FF_RB)
        x = x_ref[rows]
        hf = _rms(x, gf_ref[...]).astype(BF16)
        acc = x
        for start, size in FF_CHUNKS:
            cs = slice(start, start + size)
            gt = jnp.dot(hf, wg_ref[:, cs], preferred_element_type=F32)
            up = jnp.dot(hf, wu_ref[:, cs], preferred_element_type=F32)
            act = (gt * jax.nn.sigmoid(gt) * up).astype(BF16)
            acc = acc + jnp.dot(act, wd_ref[cs, :], preferred_element_type=F32)
        out_ref[rows] = _rms(acc, gl_ref[...])


def _ffn(x1, gf, wg, wu, wd, gl, tm=1024):
    n = x1.shape[0]
    row = pl.BlockSpec((tm, D_MODEL), lambda i: (i, 0))
    once = lambda a: pl.BlockSpec(a.shape, lambda i: (0, 0), pipeline_mode=pl.Buffered(1))
    return pl.pallas_call(
        _ffn_kernel,
        grid=(n // tm,),
        in_specs=[row, once(gf), once(wg), once(wu), once(wd), once(gl)],
        out_specs=row,
        out_shape=jax.ShapeDtypeStruct((n, D_MODEL), F32),
        compiler_params=pltpu.CompilerParams(
            dimension_semantics=("parallel",), vmem_limit_bytes=VMEM_LIMIT),
        name="ffn",
    )(x1, gf, wg, wu, wd, gl)


def _layer(x2, B, S, bias, norm_mix_g, w_in, b_gate_if, conv_w, conv_b, mlstm_norm_g,
           w_proj_a, w_proj_b, w_out, norm_ffn_g, w_gate, w_up, w_down, final_g):
    sizes = (M_WIDTH, M_WIDTH, M_WIDTH, 2 * M_HEADS, M_WIDTH, A_WIDTH, A_WIDTH, A_WIDTH,
             2 * D_MODEL)
    offs = np.concatenate([[0], np.cumsum(sizes)])
    piece = lambda i: w_in[:, offs[i]:offs[i + 1]]
    qkv = lambda g: [piece(i)[:, g * A_GW:(g + 1) * A_GW] * sc
                     for i, sc in ((5, QK_SCALE), (6, 1.0), (7, 1.0))]
    w_p = jnp.concatenate([w_in[:, :offs[3]], piece(4), piece(8)]
                          + [w for g in range(N_GROUPS) for w in qkv(g)], axis=1).astype(BF16).T
    w_if = jnp.concatenate(_split_bf16(piece(3).T, 2), axis=0)

    p_flat, if_t, *dilated = _inproj(x2, norm_mix_g[None], w_p, w_if, B, S)

    bif = jnp.broadcast_to(b_gate_if[:, None], (SUBLANES, LANES))
    gates = _gates(if_t, bif, B, S)
    hm = _mlstm(p_flat.reshape(B, S, P_WIDTH), gates, conv_w,
                conv_b[None]).reshape(B * S, M_WIDTH)

    os_, ls_ = [], []
    for g in range(N_GROUPS):
        if g == 0:
            o, lse = _attn_group(p_flat.reshape(B, 1, S, P_WIDTH), P_A0 // A_GW, bias, g)
        else:
            o, lse = _attn_group(dilated[g - 1], 0, bias, g)
        os_.append(o)
        ls_.append(lse)

    x1 = _merge(x2, p_flat, hm, mlstm_norm_g[None], os_, ls_, w_proj_a.astype(BF16),
                w_proj_b.astype(BF16), w_out.astype(BF16), S)
    return _ffn(x1, norm_ffn_g[None], w_gate.astype(BF16), w_up.astype(BF16),
                w_down.astype(BF16), final_g[None])


def kernel(x, norm_mix_g, w_in, b_gate_if, conv_w, conv_b, mlstm_norm_g, w_proj_a, w_proj_b,
           w_out, norm_ffn_g, w_gate, w_up, w_down, rel_bias, norm_final_g):
    B, S, _ = x.shape
    depth = w_in.shape[0]
    assert depth == 1, "the final norm is fused into the (single) layer's channel mixer"
    bias = _bias_tables(rel_bias)
    out = _layer(x.reshape(B * S, D_MODEL), B, S, bias, norm_mix_g[0], w_in[0], b_gate_if[0],
                 conv_w[0], conv_b[0], mlstm_norm_g[0], w_proj_a[0], w_proj_b[0], w_out[0],
                 norm_ffn_g[0], w_gate[0], w_up[0], w_down[0], norm_final_g)
    return out.reshape(B, S, D_MODEL)
```
